```python
import jax, jax.numpy as jnp
from jax import lax
import numpy as np

D_MODEL = 1024
BATCH = 8
SEQ = 4096
DEPTH = 2

GRID_W = 64
CTX_LEN = 256
N_MIXERS = 2
N_CONV_LAYERS = (DEPTH + N_MIXERS - 1) // N_MIXERS
N_ATTN_LAYERS = DEPTH // N_MIXERS
CONV_WIDTH = 31
N_HEADS = 16
HEAD_DIM = D_MODEL // N_HEADS
WIN_H = 8
WIN_W = 16
N_EXPERTS = 16
N_GROUPS = 4
EXPERTS_PER_GROUP = N_EXPERTS // N_GROUPS
TOP_K = 2
D_EXPERT = D_MODEL // 2
N_MOD = 6
EPS = 1e-6

kernel_name = "hybrid_conv_natten_grouped_moe_dit"


def rmsnorm(x, g):
    xf = x.astype(jnp.float32)
    y = xf * lax.rsqrt(jnp.mean(xf * xf, axis=-1, keepdims=True) + EPS)
    return (y * g.astype(jnp.float32)).astype(x.dtype)


def layernorm(x, g, b):
    xf = x.astype(jnp.float32)
    mu = jnp.mean(xf, axis=-1, keepdims=True)
    var = jnp.mean(jnp.square(xf - mu), axis=-1, keepdims=True)
    y = (xf - mu) * lax.rsqrt(var + EPS) * g.astype(jnp.float32) + b.astype(jnp.float32)
    return y.astype(x.dtype)


def modulate(x, g, shift, scale):
    return rmsnorm(x, g) * (1 + scale) + shift


def depthwise_conv(u, w, b):
    y = lax.conv_general_dilated(u, w[:, None, :], window_strides=(1,), padding='SAME',
                                 dimension_numbers=('NWC', 'WIO', 'NWC'),
                                 feature_group_count=u.shape[-1])
    return y + b


def conv_module(h, w_in, b_in, w_dw, b_dw, ln_g, ln_b, w_out, b_out):
    a, gt = jnp.split(h @ w_in + b_in, 2, axis=-1)
    u = a * jax.nn.sigmoid(gt)
    u = depthwise_conv(u, w_dw, b_dw)
    u = jax.nn.silu(layernorm(u, ln_g, ln_b))
    return u @ w_out + b_out


def dense_attention(q, k, v):
    s = jnp.einsum('bqhd,bkhd->bhqk', q, k).astype(jnp.float32) * (HEAD_DIM ** -0.5)
    p = jax.nn.softmax(s, axis=-1).astype(v.dtype)
    return jnp.einsum('bhqk,bkhd->bqhd', p, v)


def neighbourhood_attention(q, k, v, k_ctx, v_ctx, rpb):
    B, S, H, Dh = q.shape
    rows = S // GRID_W
    kh = min(WIN_H, rows)
    n_loc = kh * WIN_W
    qg = q.reshape(B, rows, GRID_W, H, Dh)
    kg = k.reshape(B, rows, GRID_W, H, Dh)
    vg = v.reshape(B, rows, GRID_W, H, Dh)
    col = jnp.arange(GRID_W)
    col_start = jnp.clip(col - WIN_W // 2, 0, GRID_W - WIN_W)
    col_idx = col_start[:, None] + jnp.arange(WIN_W)[None, :]
    dcol = col_idx - col[:, None]
    scale = Dh ** -0.5

    def row_block(r):
        r0 = jnp.clip(r - kh // 2, 0, rows - kh)
        qr = lax.dynamic_index_in_dim(qg, r, axis=1, keepdims=False)
        kb = lax.dynamic_slice_in_dim(kg, r0, kh, axis=1)
        vb = lax.dynamic_slice_in_dim(vg, r0, kh, axis=1)
        kw = kb[:, :, col_idx]
        vw = vb[:, :, col_idx]
        drow = r0 + jnp.arange(kh) - r
        ri = (drow + WIN_H - 1)[None, :, None]
        ci = (dcol + WIN_W - 1)[:, None, :]
        bias = rpb[:, ri, ci].astype(jnp.float32)
        s_loc = jnp.einsum('bqhd,biqjhd->bhqij', qr, kw).astype(jnp.float32) * scale + bias[None]
        s_ctx = jnp.einsum('bqhd,blhd->bhql', qr, k_ctx).astype(jnp.float32) * scale
        s = jnp.concatenate([s_loc.reshape(B, H, GRID_W, n_loc), s_ctx], axis=-1)
        p = jax.nn.softmax(s, axis=-1).astype(v.dtype)
        p_loc = p[..., :n_loc].reshape(B, H, GRID_W, kh, WIN_W)
        p_ctx = p[..., n_loc:]
        return (jnp.einsum('bhqij,biqjhd->bqhd', p_loc, vw)
                + jnp.einsum('bhql,blhd->bqhd', p_ctx, v_ctx))

    out = lax.map(row_block, jnp.arange(rows))
    return out.transpose(1, 0, 2, 3, 4).reshape(B, S, H, Dh)


def na_mixer(h, hc, w_qkv, w_o, rpb, with_ctx_out):
    B, S, D = h.shape
    L = hc.shape[1]
    q, k, v = jnp.split(h @ w_qkv, 3, axis=-1)
    q = q.reshape(B, S, N_HEADS, HEAD_DIM)
    k = k.reshape(B, S, N_HEADS, HEAD_DIM)
    v = v.reshape(B, S, N_HEADS, HEAD_DIM)
    if with_ctx_out:
        qc, kc, vc = jnp.split(hc @ w_qkv, 3, axis=-1)
    else:
        kc, vc = jnp.split(hc @ w_qkv[:, D:], 2, axis=-1)
    kc = kc.reshape(B, L, N_HEADS, HEAD_DIM)
    vc = vc.reshape(B, L, N_HEADS, HEAD_DIM)
    o = neighbourhood_attention(q, k, v, kc, vc, rpb).reshape(B, S, D) @ w_o
    if with_ctx_out:
        oc = dense_attention(qc.reshape(B, L, N_HEADS, HEAD_DIM), kc, vc).reshape(B, L, D) @ w_o
        return o, oc
    return o, None


def grouped_moe(h, router_w, router_b, w_gate, w_up, w_down):
    N = h.shape[0]
    s = jax.nn.sigmoid((h @ router_w).astype(jnp.float32))
    s_sel = s + router_b.astype(jnp.float32)
    group_score = lax.top_k(s_sel.reshape(N, N_GROUPS, EXPERTS_PER_GROUP), TOP_K)[0].sum(-1)
    g = jnp.argmax(group_score, axis=-1)
    in_group = (jnp.arange(N_EXPERTS) // EXPERTS_PER_GROUP)[None, :] == g[:, None]
    _, idx = lax.top_k(jnp.where(in_group, s_sel, -jnp.inf), TOP_K)
    w = jnp.take_along_axis(s, idx, axis=-1)
    w = w / jnp.sum(w, axis=-1, keepdims=True)
    combine = jnp.sum(jax.nn.one_hot(idx, N_EXPERTS, dtype=jnp.float32) * w[..., None], axis=1)
    combine = combine.astype(h.dtype)
    out = jnp.zeros_like(h)
    for e in range(N_EXPERTS):
        he = jax.nn.silu(h @ w_gate[e]) * (h @ w_up[e])
        out = out + combine[:, e:e + 1] * (he @ w_down[e])
    return out


def _normal(key, shape, scale):
    return jax.random.normal(key, shape, jnp.float32) * scale


def setup_inputs(seed: int = 0) -> dict:
    key = jax.random.key(seed)
    ks = jax.random.split(key, 32)
    D = D_MODEL
    return {
        "x": _normal(ks[0], (BATCH, SEQ, D), 1.0),
        "c": _normal(ks[1], (BATCH, D), 1.0),
        "ctx": _normal(ks[2], (BATCH, CTX_LEN, D), 1.0),
        "c_ctx": _normal(ks[3], (D,), 1.0),
        "w_ada": _normal(ks[4], (DEPTH, D, N_MOD * D), 0.4 * D ** -0.5),
        "b_ada": _normal(ks[5], (DEPTH, N_MOD * D), 0.02),
        "g_mix_pre": 1.0 + _normal(ks[6], (DEPTH, D), 0.05),
        "g_mix_post": 1.0 + _normal(ks[7], (DEPTH, D), 0.05),
        "g_ffn_pre": 1.0 + _normal(ks[8], (DEPTH, D), 0.05),
        "g_ffn_post": 1.0 + _normal(ks[9], (DEPTH, D), 0.05),
        "conv_w_in": _normal(ks[10], (N_CONV_LAYERS, D, 2 * D), D ** -0.5),
        "conv_b_in": _normal(ks[11], (N_CONV_LAYERS, 2 * D), 0.02),
        "conv_w_dw": _normal(ks[12], (N_CONV_LAYERS, CONV_WIDTH, D), CONV_WIDTH ** -0.5),
        "conv_b_dw": _normal(ks[13], (N_CONV_LAYERS, D), 0.02),
        "conv_ln_g": 1.0 + _normal(ks[14], (N_CONV_LAYERS, D), 0.05),
        "conv_ln_b": _normal(ks[15], (N_CONV_LAYERS, D), 0.02),
        "conv_w_out": _normal(ks[16], (N_CONV_LAYERS, D, D), D ** -0.5),
        "conv_b_out": _normal(ks[17], (N_CONV_LAYERS, D), 0.02),
        "attn_w_qkv": _normal(ks[18], (N_ATTN_LAYERS, D, 3 * D), D ** -0.5),
        "attn_w_o": _normal(ks[19], (N_ATTN_LAYERS, D, D), D ** -0.5),
        "attn_rpb": _normal(ks[20], (N_ATTN_LAYERS, N_HEADS, 2 * WIN_H - 1, 2 * WIN_W - 1), 0.2),
        "router_w": _normal(ks[21], (D, N_EXPERTS), D ** -0.5),
        "router_b": _normal(ks[22], (N_EXPERTS,), 0.01),
        "exp_w_gate": _normal(ks[23], (DEPTH, N_EXPERTS, D, D_EXPERT), D ** -0.5),
        "exp_w_up": _normal(ks[24], (DEPTH, N_EXPERTS, D, D_EXPERT), D ** -0.5),
        "exp_w_down": _normal(ks[25], (DEPTH, N_EXPERTS, D_EXPERT, D), D_EXPERT ** -0.5),
    }


def reference(x, c, ctx, c_ctx, w_ada, b_ada, g_mix_pre, g_mix_post, g_ffn_pre, g_ffn_post,
              conv_w_in, conv_b_in, conv_w_dw, conv_b_dw, conv_ln_g, conv_ln_b, conv_w_out,
              conv_b_out, attn_w_qkv, attn_w_o, attn_rpb, router_w, router_b,
              exp_w_gate, exp_w_up, exp_w_down):
    B, S, D = x.shape
    L = ctx.shape[1]
    xc = ctx
    for i in range(DEPTH):
        kind = i % N_MIXERS
        j = i // N_MIXERS
        update_ctx = i < DEPTH - 1
        mod = jax.nn.silu(c) @ w_ada[i] + b_ada[i]
        mod_c = jax.nn.silu(c_ctx) @ w_ada[i] + b_ada[i]
        sh1, sc1, ga1, sh2, sc2, ga2 = [m[:, None, :] for m in jnp.split(mod, N_MOD, axis=-1)]
        sh1c, sc1c, ga1c, sh2c, sc2c, ga2c = jnp.split(mod_c, N_MOD, axis=-1)

        h = modulate(x, g_mix_pre[i], sh1, sc1)
        hc = modulate(xc, g_mix_pre[i], sh1c, sc1c)
        if kind == 0:
            conv_args = (conv_w_in[j], conv_b_in[j], conv_w_dw[j], conv_b_dw[j],
                         conv_ln_g[j], conv_ln_b[j], conv_w_out[j], conv_b_out[j])
            y = conv_module(h, *conv_args)
            yc = conv_module(hc, *conv_args) if update_ctx else None
        else:
            y, yc = na_mixer(h, hc, attn_w_qkv[j], attn_w_o[j], attn_rpb[j], update_ctx)
        x = x + ga1 * rmsnorm(y, g_mix_post[i])
        if update_ctx:
            xc = xc + ga1c * rmsnorm(yc, g_mix_post[i])

        tokens = modulate(x, g_ffn_pre[i], sh2, sc2).reshape(B * S, D)
        if update_ctx:
            tokens_c = modulate(xc, g_ffn_pre[i], sh2c, sc2c).reshape(B * L, D)
            tokens = jnp.concatenate([tokens, tokens_c], axis=0)
        f = grouped_moe(tokens, router_w, router_b, exp_w_gate[i], exp_w_up[i], exp_w_down[i])
        x = x + ga2 * rmsnorm(f[:B * S].reshape(B, S, D), g_ffn_post[i])
        if update_ctx:
            xc = xc + ga2c * rmsnorm(f[B * S:].reshape(B, L, D), g_ffn_post[i])
    return x
```

```python
import functools

import jax
import jax.numpy as jnp
from jax import lax
from jax.experimental import pallas as pl
from jax.experimental.pallas import tpu as pltpu

F32 = jnp.float32
BF16 = jnp.bfloat16

EPS = 1e-6
GRID_W = 64
WIN_H = 8
WIN_W = 16
N_HEADS = 16
HEAD_DIM = 64
N_EXPERTS = 16
N_GROUPS = 4
GROUP_SIZE = N_EXPERTS // N_GROUPS
N_PAIRS = 6
N_BUCKETS = N_GROUPS * N_PAIRS
PAIR_LO = (0, 0, 0, 1, 1, 2)
PAIR_HI = (1, 2, 3, 2, 3, 3)
CONV_WIDTH = 31
CONV_PAD = (CONV_WIDTH - 1) // 2
HALO = 16
MASK_BIAS = -1e30
MOD_ROWS = 16

TM_LAT = 512
TM_CTX = 256
TM_MOE = 256
VMEM_LIMIT = 56 * 1024 * 1024

_NT = (((1,), (1,)), ((), ()))


def _cparams(n_axes):
    return pltpu.CompilerParams(dimension_semantics=("arbitrary",) * n_axes,
                                vmem_limit_bytes=VMEM_LIMIT)


def _sigmoid(v):
    return 1.0 / (1.0 + jnp.exp(-v))


def _rms(v, g):
    return v * lax.rsqrt(jnp.mean(v * v, axis=-1, keepdims=True) + EPS) * g


def _split_bf16(v):
    hi = v.astype(BF16)
    lo = (v - hi.astype(F32)).astype(BF16)
    return hi, lo


def _ada_kernel(c_ref, w_ref, b_ref, o_ref):
    a = c_ref[...]
    a = a * _sigmoid(a)
    a_hi, a_lo = _split_bf16(a)
    w_hi, w_lo = _split_bf16(w_ref[...])
    acc = jnp.dot(a_hi, w_hi, preferred_element_type=F32)
    acc += jnp.dot(a_lo, w_hi, preferred_element_type=F32)
    acc += jnp.dot(a_hi, w_lo, preferred_element_type=F32)
    o_ref[...] = acc + b_ref[...]


def _ada(cc, w_ada, b_ada):
    depth, d, n = w_ada.shape
    bn = 1024
    return pl.pallas_call(
        _ada_kernel,
        out_shape=jax.ShapeDtypeStruct((depth, MOD_ROWS, n), F32),
        grid=(depth, n // bn),
        in_specs=[pl.BlockSpec((MOD_ROWS, d), lambda l, j: (0, 0)),
                  pl.BlockSpec((None, d, bn), lambda l, j: (l, 0, j)),
                  pl.BlockSpec((None, 1, bn), lambda l, j: (l, 0, j))],
        out_specs=pl.BlockSpec((None, MOD_ROWS, bn), lambda l, j: (l, 0, j)),
        compiler_params=_cparams(2),
        name="ada",
    )(cc, w_ada, b_ada.reshape(depth, 1, n))


def _proj_kernel(*refs, fuse_resid, glu, d):
    refs = list(refs)
    x_ref = refs.pop(0)
    x = x_ref[...]
    if fuse_resid:
        f_ref, modp_ref, gpost_ref = refs.pop(0), refs.pop(0), refs.pop(0)
    mod_ref, gpre_ref, w_ref = refs.pop(0), refs.pop(0), refs.pop(0)
    if glu:
        b_ref = refs.pop(0)
    if fuse_resid:
        xn_ref = refs.pop(0)
        x = x + modp_ref[5:6, :] * _rms(f_ref[...], gpost_ref[...])
        xn_ref[...] = x
    h = _rms(x, gpre_ref[...]) * (1.0 + mod_ref[1:2, :]) + mod_ref[0:1, :]
    z = jnp.dot(h.astype(BF16), w_ref[...], preferred_element_type=F32)
    if glu:
        z = z + b_ref[...]
        refs[0][...] = (z[:, :d] * _sigmoid(z[:, d:])).astype(BF16)
    else:
        n_out = len(refs)
        for j, o_ref in enumerate(refs):
            part = z[:, j * d:(j + 1) * d]
            if n_out == 3 and j == 0:
                part = part * (HEAD_DIM ** -0.5)
            o_ref[...] = part.astype(BF16)


def _proj(x, mods, layer, gpre, w, *, tm, rows_per_batch, ctx, bias=None, resid=None, n_out=1, name):
    n, d = x.shape
    tiles_per_batch = rows_per_batch // tm
    if ctx:
        mod_idx = lambda i: (layer, 8, 0, 0)
    else:
        mod_idx = lambda i: (layer, i // tiles_per_batch, 0, 0)
    row = lambda i: (i, 0)
    const = lambda i: (0, 0)
    args, specs = [x], [pl.BlockSpec((tm, d), row)]
    if resid is not None:
        f, gpost = resid
        if ctx:
            modp_idx = lambda i: (layer - 1, 8, 0, 0)
        else:
            modp_idx = lambda i: (layer - 1, i // tiles_per_batch, 0, 0)
        args += [f, mods, gpost]
        specs += [pl.BlockSpec((tm, d), row), pl.BlockSpec((None, None, 6, d), modp_idx),
                  pl.BlockSpec((1, d), const)]
    args += [mods, gpre, w]
    specs += [pl.BlockSpec((None, None, 6, d), mod_idx), pl.BlockSpec((1, d), const),
              pl.BlockSpec(w.shape, const)]
    glu = bias is not None
    if glu:
        args.append(bias)
        specs.append(pl.BlockSpec(bias.shape, const))
    out_shape, out_specs = [], []
    if resid is not None:
        out_shape.append(jax.ShapeDtypeStruct((n, d), F32))
        out_specs.append(pl.BlockSpec((tm, d), row))
    for _ in range(n_out):
        out_shape.append(jax.ShapeDtypeStruct((n, d), BF16))
        out_specs.append(pl.BlockSpec((tm, d), row))
    return pl.pallas_call(
        functools.partial(_proj_kernel, fuse_resid=resid is not None, glu=glu, d=d),
        out_shape=out_shape, grid=(n // tm,), in_specs=specs, out_specs=out_specs,
        compiler_params=_cparams(1), name=name,
    )(*args)


def _route(lg, rb):
    s = _sigmoid(lg)
    ssel = s + rb
    sel = [ssel[e:e + 1, :] for e in range(N_EXPERTS)]
    raw = [s[e:e + 1, :] for e in range(N_EXPERTS)]
    scores = []
    for g in range(N_GROUPS):
        v0, v1, v2, v3 = sel[4 * g:4 * g + 4]
        a, b = jnp.maximum(v0, v1), jnp.minimum(v0, v1)
        c, e = jnp.maximum(v2, v3), jnp.minimum(v2, v3)
        scores.append(jnp.maximum(a, c) + jnp.maximum(jnp.minimum(a, c), jnp.maximum(b, e)))
    best = scores[0]
    gi = jnp.zeros_like(best)
    for g in range(1, N_GROUPS):
        better = scores[g] > best
        gi = jnp.where(better, float(g), gi)
        best = jnp.where(better, scores[g], best)

    def pick(rows, j):
        return jnp.where(gi == 0.0, rows[j],
                         jnp.where(gi == 1.0, rows[4 + j], jnp.where(gi == 2.0, rows[8 + j], rows[12 + j])))

    v = [pick(sel, j) for j in range(GROUP_SIZE)]
    r = [pick(raw, j) for j in range(GROUP_SIZE)]
    m = []
    for j in range(GROUP_SIZE):
        ahead = jnp.zeros_like(best)
        for k in range(GROUP_SIZE):
            if k == j:
                continue
            beats = (v[k] >= v[j]) if k < j else (v[k] > v[j])
            ahead = ahead + jnp.where(beats, 1.0, 0.0)
        m.append(ahead < 2.0)
    pair = jnp.where(m[0], jnp.where(m[1], 0.0, jnp.where(m[2], 1.0, 2.0)),
                     jnp.where(m[1], jnp.where(m[2], 3.0, 4.0), 5.0))
    sa = jnp.where(m[0], r[0], jnp.where(m[1], r[1], r[2]))
    sb = jnp.where(m[3], r[3], jnp.where(m[2], r[2], r[1]))
    den = sa + sb
    bucket = gi * float(N_PAIRS) + pair
    zeros = jnp.zeros((5, lg.shape[1]), F32)
    return jnp.concatenate([bucket, sa / den, sb / den, zeros], axis=0)


def _tail(y, x, mod_ref, gpost_ref, gffn_ref, rwh_ref, rwl_ref, rb_ref, x1_ref, tok_ref, route_ref):
    x1 = x + mod_ref[2:3, :] * _rms(y, gpost_ref[...])
    x1_ref[...] = x1
    t = _rms(x1, gffn_ref[...]) * (1.0 + mod_ref[4:5, :]) + mod_ref[3:4, :]
    tok_ref[...] = t
    t_hi, t_lo = _split_bf16(t)
    lg = lax.dot_general(rwh_ref[...], t_hi, _NT, preferred_element_type=F32)
    lg += lax.dot_general(rwl_ref[...], t_hi, _NT, preferred_element_type=F32)
    lg += lax.dot_general(rwh_ref[...], t_lo, _NT, preferred_element_type=F32)
    route_ref[...] = _route(lg, rb_ref[...])


def _conv_tail_kernel(u_ref, up_ref, un_ref, x_ref, mod_ref, wdw_ref, bdw_ref, lng_ref, lnb_ref,
                      wout_ref, bout_ref, gpost_ref, gffn_ref, rwh_ref, rwl_ref, rb_ref,
                      x1_ref, tok_ref, route_ref, ubuf, ush, cacc, *, tm, tiles_per_seq, chunk):
    t = pl.program_id(0) % tiles_per_seq
    prev_ok = jnp.where(t > 0, 1.0, 0.0)
    next_ok = jnp.where(t < tiles_per_seq - 1, 1.0, 0.0)
    ubuf[0:HALO, :] = up_ref[...].astype(F32) * prev_ok
    ubuf[HALO:HALO + tm, :] = u_ref[...].astype(F32)
    ubuf[HALO + tm:, :] = un_ref[...].astype(F32) * next_ok

    base = HALO - CONV_PAD
    first = True
    for s in range(8):
        taps = [k for k in range(CONV_WIDTH) if (k + base) % 8 == s]
        if not taps:
            continue
        span = tm + 8 * ((CONV_WIDTH - 1 + base) // 8)
        if s == 0:
            src = ubuf
        else:
            ush[0:span, :] = ubuf[s:s + span, :]
            src = ush

        def body(c, carry, taps=taps, src=src, first=first):
            r0 = pl.multiple_of(c * chunk, chunk)
            acc = None if first else cacc[pl.ds(r0, chunk), :]
            for k in taps:
                q = (k + base) // 8
                term = src[pl.ds(r0 + 8 * q, chunk), :] * wdw_ref[k:k + 1, :]
                acc = term if acc is None else acc + term
            cacc[pl.ds(r0, chunk), :] = acc
            return carry

        lax.fori_loop(0, tm // chunk, body, 0)
        first = False

    y = cacc[...] + bdw_ref[...]
    mu = jnp.mean(y, axis=-1, keepdims=True)
    yc = y - mu
    var = jnp.mean(yc * yc, axis=-1, keepdims=True)
    yn = yc * lax.rsqrt(var + EPS) * lng_ref[...] + lnb_ref[...]
    act = yn * _sigmoid(yn)
    z = jnp.dot(act.astype(BF16), wout_ref[...], preferred_element_type=F32) + bout_ref[...]
    _tail(z, x_ref[...], mod_ref, gpost_ref, gffn_ref, rwh_ref, rwl_ref, rb_ref, x1_ref, tok_ref, route_ref)


def _tail_common_specs(d, tm, mod_idx):
    const = lambda i: (0, 0)
    return [pl.BlockSpec((1, d), const), pl.BlockSpec((1, d), const),
            pl.BlockSpec((N_EXPERTS, d), const), pl.BlockSpec((N_EXPERTS, d), const),
            pl.BlockSpec((N_EXPERTS, 1), const)]


def _tail_outs(n, d, tm):
    out_shape = [jax.ShapeDtypeStruct((n, d), F32), jax.ShapeDtypeStruct((n, d), F32),
                 jax.ShapeDtypeStruct((8, n), F32)]
    out_specs = [pl.BlockSpec((tm, d), lambda i: (i, 0)), pl.BlockSpec((tm, d), lambda i: (i, 0)),
                 pl.BlockSpec((8, tm), lambda i: (0, i))]
    return out_shape, out_specs


def _conv_tail(u, x, mods, layer, conv, tailp, *, tm, rows_per_batch, ctx, name):
    n, d = x.shape
    wdw, bdw, lng, lnb, wout, bout = conv
    gpost, gffn, rwh, rwl, rb = tailp
    tiles_per_seq = rows_per_batch // tm
    hb = tm // HALO
    n_hb = n // HALO
    if ctx:
        mod_idx = lambda i: (layer, 8, 0, 0)
    else:
        mod_idx = lambda i: (layer, i // tiles_per_seq, 0, 0)
    row = lambda i: (i, 0)
    const = lambda i: (0, 0)
    specs = [pl.BlockSpec((tm, d), row),
             pl.BlockSpec((HALO, d), lambda i: (jnp.maximum(i * hb - 1, 0), 0)),
             pl.BlockSpec((HALO, d), lambda i: (jnp.minimum((i + 1) * hb, n_hb - 1), 0)),
             pl.BlockSpec((tm, d), row),
             pl.BlockSpec((None, None, 6, d), mod_idx),
             pl.BlockSpec(wdw.shape, const), pl.BlockSpec((1, d), const),
             pl.BlockSpec((1, d), const), pl.BlockSpec((1, d), const),
             pl.BlockSpec(wout.shape, const), pl.BlockSpec((1, d), const)]
    specs += _tail_common_specs(d, tm, mod_idx)
    out_shape, out_specs = _tail_outs(n, d, tm)
    chunk = 16
    return pl.pallas_call(
        functools.partial(_conv_tail_kernel, tm=tm, tiles_per_seq=tiles_per_seq, chunk=chunk),
        out_shape=out_shape, grid=(n // tm,), in_specs=specs, out_specs=out_specs,
        scratch_shapes=[pltpu.VMEM((tm + 2 * HALO, d), F32), pltpu.VMEM((tm + 2 * HALO, d), F32),
                        pltpu.VMEM((tm, d), F32)],
        compiler_params=_cparams(1), name=name,
    )(u, u, u, x, mods, wdw, bdw, lng, lnb, wout, bout, gpost, gffn, rwh, rwl, rb)


def _attn_tail_kernel(o_ref, x_ref, mod_ref, wo_ref, gpost_ref, gffn_ref, rwh_ref, rwl_ref, rb_ref,
                      x1_ref, tok_ref, route_ref):
    y = jnp.dot(o_ref[...], wo_ref[...], preferred_element_type=F32)
    _tail(y, x_ref[...], mod_ref, gpost_ref, gffn_ref, rwh_ref, rwl_ref, rb_ref, x1_ref, tok_ref, route_ref)


def _attn_tail(o, x, mods, layer, wo, tailp, *, tm, rows_per_batch, name):
    n, d = x.shape
    gpost, gffn, rwh, rwl, rb = tailp
    tiles_per_batch = rows_per_batch // tm
    mod_idx = lambda i: (layer, i // tiles_per_batch, 0, 0)
    row = lambda i: (i, 0)
    const = lambda i: (0, 0)
    specs = [pl.BlockSpec((tm, d), row), pl.BlockSpec((tm, d), row),
             pl.BlockSpec((None, None, 6, d), mod_idx), pl.BlockSpec(wo.shape, const)]
    specs += _tail_common_specs(d, tm, mod_idx)
    out_shape, out_specs = _tail_outs(n, d, tm)
    return pl.pallas_call(
        _attn_tail_kernel, out_shape=out_shape, grid=(n // tm,), in_specs=specs, out_specs=out_specs,
        compiler_params=_cparams(1), name=name,
    )(o, x, mods, wo, gpost, gffn, rwh, rwl, rb)


def _natten_kernel(q_ref, k_ref, v_ref, kc_ref, vc_ref, bias_ref, o_ref, *, rows):
    lane = lax.broadcasted_iota(jnp.int32, (1, 2 * HEAD_DIM), 1)
    n_loc = WIN_H * GRID_W
    kc = kc_ref[...]
    vc = vc_ref[...]

    def body(r, carry):
        r0 = jnp.clip(r - WIN_H // 2, 0, rows - WIN_H)
        shift_idx = r0 - r + WIN_H // 2 + (WIN_H // 2 - 1)
        q2 = q_ref[pl.ds(pl.multiple_of(r * GRID_W, GRID_W), GRID_W), :]
        kw = k_ref[pl.ds(pl.multiple_of(r0 * GRID_W, GRID_W), n_loc), :]
        vw = v_ref[pl.ds(pl.multiple_of(r0 * GRID_W, GRID_W), n_loc), :]
        outs = []
        for h in range(2):
            in_head = (lane >= h * HEAD_DIM) & (lane < (h + 1) * HEAD_DIM)
            qm = jnp.where(in_head, q2, jnp.zeros_like(q2))
            s_loc = lax.dot_general(qm, kw, _NT, preferred_element_type=F32) + bias_ref[h, shift_idx]
            s_ctx = lax.dot_general(qm, kc, _NT, preferred_element_type=F32)
            m = jnp.maximum(jnp.max(s_loc, axis=-1, keepdims=True), jnp.max(s_ctx, axis=-1, keepdims=True))
            p_loc = jnp.exp(s_loc - m)
            p_ctx = jnp.exp(s_ctx - m)
            den = jnp.sum(p_loc, axis=-1, keepdims=True) + jnp.sum(p_ctx, axis=-1, keepdims=True)
            acc = jnp.dot(p_loc.astype(BF16), vw, preferred_element_type=F32)
            acc += jnp.dot(p_ctx.astype(BF16), vc, preferred_element_type=F32)
            outs.append(acc / den)
        o = jnp.where(lane < HEAD_DIM, outs[0], outs[1])
        o_ref[pl.ds(pl.multiple_of(r * GRID_W, GRID_W), GRID_W), :] = o.astype(BF16)
        return carry

    lax.fori_loop(0, rows, body, 0)


def _natten(q, k, v, kc, vc, bias):
    b, s, d = q.shape
    l = kc.shape[1]
    rows = s // GRID_W
    hw = 2 * HEAD_DIM
    lat = pl.BlockSpec((None, s, hw), lambda hp, bi: (bi, 0, hp))
    cspec = pl.BlockSpec((None, l, hw), lambda hp, bi: (bi, 0, hp))
    return pl.pallas_call(
        functools.partial(_natten_kernel, rows=rows),
        out_shape=jax.ShapeDtypeStruct((b, s, d), BF16),
        grid=(d // hw, b),
        in_specs=[lat, lat, lat, cspec, cspec,
                  pl.BlockSpec((2, WIN_H, GRID_W, WIN_H * GRID_W), lambda hp, bi: (hp, 0, 0, 0))],
        out_specs=lat,
        compiler_params=_cparams(2), name="natten",
    )(q, k, v, kc, vc, bias)


def _bias_table(rpb):
    col = jnp.arange(GRID_W)
    cs = jnp.clip(col - WIN_W // 2, 0, GRID_W - WIN_W)
    kcol = jnp.arange(GRID_W)
    valid = (kcol[None, :] >= cs[:, None]) & (kcol[None, :] < cs[:, None] + WIN_W)
    ci = jnp.clip(kcol[None, :] - col[:, None] + WIN_W - 1, 0, 2 * WIN_W - 2)
    shift = jnp.arange(WIN_H) - (WIN_H // 2 - 1)
    ri = jnp.arange(WIN_H)[None, :] - WIN_H // 2 + shift[:, None] + WIN_H - 1
    tab = rpb[:, ri[:, :, None, None], ci[None, None, :, :]]
    tab = jnp.where(valid[None, None, None], tab, MASK_BIAS)
    tab = tab.transpose(0, 1, 3, 2, 4)
    return tab.reshape(rpb.shape[0], WIN_H, GRID_W, WIN_H * GRID_W).astype(F32)


def _moe_kernel(ea_ref, eb_ref, nt_ref, xs_ref, ws_ref, wga_ref, wua_ref, wda_ref,
                wgb_ref, wub_ref, wdb_ref, y_ref):
    i = pl.program_id(0)

    @pl.when(i < nt_ref[0])
    def _():
        xt = xs_ref[...].astype(BF16)

        def expert(wg_ref, wu_ref, wd_ref):
            g = jnp.dot(xt, wg_ref[...], preferred_element_type=F32)
            u = jnp.dot(xt, wu_ref[...], preferred_element_type=F32)
            h = (g * _sigmoid(g)) * u
            return jnp.dot(h.astype(BF16), wd_ref[...], preferred_element_type=F32)

        ya = expert(wga_ref, wua_ref, wda_ref)
        yb = expert(wgb_ref, wub_ref, wdb_ref)
        y_ref[...] = ws_ref[:, 0:1] * ya + ws_ref[:, 1:2] * yb

    @pl.when(i >= nt_ref[0])
    def _():
        y_ref[...] = jnp.zeros_like(y_ref)


def _moe(xs, ws, ea, eb, nt, wg, wu, wd):
    npad, d = xs.shape
    de = wg.shape[2]
    row = lambda i, ea, eb, nt: (i, 0)
    sel_a = lambda i, ea, eb, nt: (ea[i], 0, 0)
    sel_b = lambda i, ea, eb, nt: (eb[i], 0, 0)
    up = (None, d, de)
    down = (None, de, d)
    grid_spec = pltpu.PrefetchScalarGridSpec(
        num_scalar_prefetch=3, grid=(npad // TM_MOE,),
        in_specs=[pl.BlockSpec((TM_MOE, d), row), pl.BlockSpec((TM_MOE, 2), row),
                  pl.BlockSpec(up, sel_a), pl.BlockSpec(up, sel_a), pl.BlockSpec(down, sel_a),
                  pl.BlockSpec(up, sel_b), pl.BlockSpec(up, sel_b), pl.BlockSpec(down, sel_b)],
        out_specs=pl.BlockSpec((TM_MOE, d), row))
    return pl.pallas_call(
        _moe_kernel, out_shape=jax.ShapeDtypeStruct((npad, d), F32), grid_spec=grid_spec,
        compiler_params=_cparams(1), name="moe",
    )(ea, eb, nt, xs, ws, wg, wu, wd, wg, wu, wd)


def _sort_plan(route, n_tiles_max):
    n = route.shape[1]
    bucket = route[0].astype(jnp.int32)
    onehot = (bucket[:, None] == jnp.arange(N_BUCKETS)[None, :]).astype(jnp.int32)
    csum = jnp.cumsum(onehot, axis=0)
    counts = csum[-1]
    rank = jnp.take_along_axis(csum, bucket[:, None], axis=1)[:, 0] - 1
    tiles = (counts + TM_MOE - 1) // TM_MOE
    tile_end = jnp.cumsum(tiles)
    pos = (tile_end - tiles)[bucket] * TM_MOE + rank
    perm = jnp.zeros((n_tiles_max * TM_MOE,), jnp.int32).at[pos].set(jnp.arange(n, dtype=jnp.int32))
    tile_bucket = jnp.minimum(jnp.searchsorted(tile_end, jnp.arange(n_tiles_max), side="right"),
                              N_BUCKETS - 1).astype(jnp.int32)
    last_bucket = tile_bucket[jnp.maximum(tile_end[-1] - 1, 0)]
    tile_bucket = jnp.where(jnp.arange(n_tiles_max) < tile_end[-1], tile_bucket, last_bucket)
    grp, pair = tile_bucket // N_PAIRS, tile_bucket % N_PAIRS
    ea = grp * GROUP_SIZE + jnp.asarray(PAIR_LO, jnp.int32)[pair]
    eb = grp * GROUP_SIZE + jnp.asarray(PAIR_HI, jnp.int32)[pair]
    return pos, perm, ea, eb, tile_end[-1:].astype(jnp.int32)


def _moe_layer(tok, route, wg, wu, wd):
    n = tok.shape[0]
    n_tiles_max = n // TM_MOE + N_BUCKETS
    pos, perm, ea, eb, nt = _sort_plan(route, n_tiles_max)
    xs = tok[perm]
    ws = route[1:3].T[perm]
    ys = _moe(xs, ws, ea, eb, nt, wg, wu, wd)
    return ys[pos]


def _final_kernel(x_ref, f_ref, mod_ref, gpost_ref, o_ref):
    o_ref[...] = x_ref[...] + mod_ref[5:6, :] * _rms(f_ref[...], gpost_ref[...])


def _final(x, f, mods, layer, gpost, *, tm, rows_per_batch):
    n, d = x.shape
    tiles_per_batch = rows_per_batch // tm
    row = lambda i: (i, 0)
    return pl.pallas_call(
        _final_kernel, out_shape=jax.ShapeDtypeStruct((n, d), F32), grid=(n // tm,),
        in_specs=[pl.BlockSpec((tm, d), row), pl.BlockSpec((tm, d), row),
                  pl.BlockSpec((None, None, 6, d), lambda i: (layer, i // tiles_per_batch, 0, 0)),
                  pl.BlockSpec((1, d), lambda i: (0, 0))],
        out_specs=pl.BlockSpec((tm, d), row),
        compiler_params=_cparams(1), name="final",
    )(x, f, mods, gpost)


def kernel(x, c, ctx, c_ctx, w_ada, b_ada, g_mix_pre, g_mix_post, g_ffn_pre, g_ffn_post, conv_w_in, conv_b_in, conv_w_dw, conv_b_dw, conv_ln_g, conv_ln_b, conv_w_out, conv_b_out, attn_w_qkv, attn_w_o, attn_rpb, router_w, router_b, exp_w_gate, exp_w_up, exp_w_down):
    b, s, d = x.shape
    l = ctx.shape[1]
    n, nc = b * s, b * l
    depth = w_ada.shape[0]
    assert depth == 2 and b <= 8 and s % TM_LAT == 0 and l == TM_CTX and s // GRID_W >= WIN_H
    tm_lat = TM_LAT

    cc = jnp.zeros((MOD_ROWS, d), F32).at[:b].set(c).at[8].set(c_ctx)
    mods = _ada(cc, w_ada, b_ada).reshape(depth, MOD_ROWS, 6, d)

    row = lambda v: v.reshape(1, d)
    rwt = router_w.T
    rwh = rwt.astype(BF16)
    rwl = (rwt - rwh.astype(F32)).astype(BF16)
    rb = router_b.reshape(N_EXPERTS, 1)
    wg = exp_w_gate.astype(BF16)
    wu = exp_w_up.astype(BF16)
    wd = exp_w_down.astype(BF16)

    xl = x.reshape(n, d)
    xc = ctx.reshape(nc, d)

    w_in = conv_w_in[0].astype(BF16)
    b_in = conv_b_in[0].reshape(1, 2 * d)
    conv = (conv_w_dw[0], row(conv_b_dw[0]), row(conv_ln_g[0]), row(conv_ln_b[0]),
            conv_w_out[0].astype(BF16), row(conv_b_out[0]))
    tailp = (row(g_mix_post[0]), row(g_ffn_pre[0]), rwh, rwl, rb)
    (u,) = _proj(xl, mods, 0, row(g_mix_pre[0]), w_in, tm=tm_lat, rows_per_batch=s, ctx=False,
                 bias=b_in, name="conv_in")
    (uc,) = _proj(xc, mods, 0, row(g_mix_pre[0]), w_in, tm=TM_CTX, rows_per_batch=l, ctx=True,
                  bias=b_in, name="conv_in_ctx")
    x1, tok, route = _conv_tail(u, xl, mods, 0, conv, tailp, tm=tm_lat, rows_per_batch=s, ctx=False,
                                name="conv_tail")
    xc1, tokc, routec = _conv_tail(uc, xc, mods, 0, conv, tailp, tm=TM_CTX, rows_per_batch=l, ctx=True,
                                   name="conv_tail_ctx")
    f = _moe_layer(jnp.concatenate([tok, tokc], axis=0), jnp.concatenate([route, routec], axis=1),
                   wg[0], wu[0], wd[0])
    f0, f0c = f[:n], f[n:]

    w_qkv = attn_w_qkv[0].astype(BF16)
    resid = lambda ff: (ff, row(g_ffn_post[0]))
    x2, q, k, v = _proj(x1, mods, 1, row(g_mix_pre[1]), w_qkv, tm=tm_lat, rows_per_batch=s, ctx=False,
                        resid=resid(f0), n_out=3, name="qkv")
    _, kc, vc = _proj(xc1, mods, 1, row(g_mix_pre[1]), w_qkv[:, d:], tm=TM_CTX, rows_per_batch=l, ctx=True,
                      resid=resid(f0c), n_out=2, name="kv_ctx")
    bias = _bias_table(attn_rpb[0])
    o = _natten(q.reshape(b, s, d), k.reshape(b, s, d), v.reshape(b, s, d),
                kc.reshape(b, l, d), vc.reshape(b, l, d), bias)
    tailp1 = (row(g_mix_post[1]), row(g_ffn_pre[1]), rwh, rwl, rb)
    x3, tok1, route1 = _attn_tail(o.reshape(n, d), x2, mods, 1, attn_w_o[0].astype(BF16), tailp1,
                                  tm=tm_lat, rows_per_batch=s, name="attn_tail")
    f1 = _moe_layer(tok1, route1, wg[1], wu[1], wd[1])
    out = _final(x3, f1, mods, 1, row(g_ffn_post[1]), tm=tm_lat, rows_per_batch=s)
    return out.reshape(b, s, d)
```

```python
import functools

import jax
import jax.numpy as jnp
from jax import lax
from jax.experimental import pallas as pl
from jax.experimental.pallas import tpu as pltpu

F32 = jnp.float32
BF16 = jnp.bfloat16

EPS = 1e-6
GRID_W = 64
WIN_H = 8
WIN_W = 16
N_HEADS = 16
HEAD_DIM = 64
N_EXPERTS = 16
N_GROUPS = 4
GROUP_SIZE = N_EXPERTS // N_GROUPS
N_PAIRS = 6
N_BUCKETS = N_GROUPS * N_PAIRS
PAIR_LO = (0, 0, 0, 1, 1, 2)
PAIR_HI = (1, 2, 3, 2, 3, 3)
CONV_WIDTH = 31
CONV_PAD = (CONV_WIDTH - 1) // 2
HALO = 16
MASK_BIAS = -1e30
Q_ROWS = 4
SLAB_ROWS = Q_ROWS + WIN_H
MOD_ROWS = 16
CNT_ROWS = 32
META_LANES = 256

TM_LAT = 512
TM_CTX = 256
TM_MOE = 256
VMEM_LIMIT = 56 * 1024 * 1024

_NT = (((1,), (1,)), ((), ()))


def _cparams(n_axes):
    return pltpu.CompilerParams(dimension_semantics=("arbitrary",) * n_axes,
                                vmem_limit_bytes=VMEM_LIMIT)


def _sigmoid(v):
    return 1.0 / (1.0 + jnp.exp(-v))


def _rms(v, g):
    return v * lax.rsqrt(jnp.mean(v * v, axis=-1, keepdims=True) + EPS) * g


def _split_bf16(v):
    hi = v.astype(BF16)
    lo = (v - hi.astype(F32)).astype(BF16)
    return hi, lo


def _ada_kernel(c_ref, w_ref, b_ref, o_ref):
    a = c_ref[...]
    a = a * _sigmoid(a)
    a_hi, a_lo = _split_bf16(a)
    w_hi, w_lo = _split_bf16(w_ref[...])
    acc = jnp.dot(a_hi, w_hi, preferred_element_type=F32)
    acc += jnp.dot(a_lo, w_hi, preferred_element_type=F32)
    acc += jnp.dot(a_hi, w_lo, preferred_element_type=F32)
    o_ref[...] = acc + b_ref[...]


def _ada(cc, w_ada, b_ada):
    depth, d, n = w_ada.shape
    bn = 1024
    return pl.pallas_call(
        _ada_kernel,
        out_shape=jax.ShapeDtypeStruct((depth, MOD_ROWS, n), F32),
        grid=(depth, n // bn),
        in_specs=[pl.BlockSpec((MOD_ROWS, d), lambda l, j: (0, 0)),
                  pl.BlockSpec((None, d, bn), lambda l, j: (l, 0, j)),
                  pl.BlockSpec((None, 1, bn), lambda l, j: (l, 0, j))],
        out_specs=pl.BlockSpec((None, MOD_ROWS, bn), lambda l, j: (l, 0, j)),
        compiler_params=_cparams(2),
        name="ada",
    )(cc, w_ada, b_ada.reshape(depth, 1, n))


def _proj_kernel(*refs, fuse_resid, glu, d):
    refs = list(refs)
    x_ref = refs.pop(0)
    x = x_ref[...]
    if fuse_resid:
        f_ref, modp_ref, gpost_ref = refs.pop(0), refs.pop(0), refs.pop(0)
    mod_ref, gpre_ref, w_ref = refs.pop(0), refs.pop(0), refs.pop(0)
    if glu:
        b_ref = refs.pop(0)
    if fuse_resid:
        xn_ref = refs.pop(0)
        x = x + modp_ref[5:6, :] * _rms(f_ref[...], gpost_ref[...])
        xn_ref[...] = x
    h = _rms(x, gpre_ref[...]) * (1.0 + mod_ref[1:2, :]) + mod_ref[0:1, :]
    z = jnp.dot(h.astype(BF16), w_ref[...], preferred_element_type=F32)
    if glu:
        z = z + b_ref[...]
        refs[0][...] = (z[:, :d] * _sigmoid(z[:, d:])).astype(BF16)
    else:
        n_out = len(refs)
        for j, o_ref in enumerate(refs):
            part = z[:, j * d:(j + 1) * d]
            if n_out == 3 and j == 0:
                part = part * (HEAD_DIM ** -0.5)
            o_ref[...] = part.astype(BF16)


def _proj(x, mods, layer, gpre, w, *, tm, rows_per_batch, ctx, bias=None, resid=None, n_out=1, name):
    n, d = x.shape
    tiles_per_batch = rows_per_batch // tm
    if ctx:
        mod_idx = lambda i: (layer, 8, 0, 0)
    else:
        mod_idx = lambda i: (layer, i // tiles_per_batch, 0, 0)
    row = lambda i: (i, 0)
    const = lambda i: (0, 0)
    args, specs = [x], [pl.BlockSpec((tm, d), row)]
    if resid is not None:
        f, gpost, f_row0 = resid
        f_blk0 = f_row0 // tm
        if ctx:
            modp_idx = lambda i: (layer - 1, 8, 0, 0)
        else:
            modp_idx = lambda i: (layer - 1, i // tiles_per_batch, 0, 0)
        args += [f, mods, gpost]
        specs += [pl.BlockSpec((tm, d), lambda i: (i + f_blk0, 0)), pl.BlockSpec((None, None, 6, d), modp_idx),
                  pl.BlockSpec((1, d), const)]
    args += [mods, gpre, w]
    specs += [pl.BlockSpec((None, None, 6, d), mod_idx), pl.BlockSpec((1, d), const),
              pl.BlockSpec(w.shape, const)]
    glu = bias is not None
    if glu:
        args.append(bias)
        specs.append(pl.BlockSpec(bias.shape, const))
    out_shape, out_specs = [], []
    if resid is not None:
        out_shape.append(jax.ShapeDtypeStruct((n, d), F32))
        out_specs.append(pl.BlockSpec((tm, d), row))
    for _ in range(n_out):
        out_shape.append(jax.ShapeDtypeStruct((n, d), BF16))
        out_specs.append(pl.BlockSpec((tm, d), row))
    return pl.pallas_call(
        functools.partial(_proj_kernel, fuse_resid=resid is not None, glu=glu, d=d),
        out_shape=out_shape, grid=(n // tm,), in_specs=specs, out_specs=out_specs,
        compiler_params=_cparams(1), name=name,
    )(*args)


def _route(lg, rb):
    s = _sigmoid(lg)
    ssel = s + rb
    sel = [ssel[e:e + 1, :] for e in range(N_EXPERTS)]
    raw = [s[e:e + 1, :] for e in range(N_EXPERTS)]
    scores = []
    for g in range(N_GROUPS):
        v0, v1, v2, v3 = sel[4 * g:4 * g + 4]
        a, b = jnp.maximum(v0, v1), jnp.minimum(v0, v1)
        c, e = jnp.maximum(v2, v3), jnp.minimum(v2, v3)
        scores.append(jnp.maximum(a, c) + jnp.maximum(jnp.minimum(a, c), jnp.maximum(b, e)))
    best = scores[0]
    gi = jnp.zeros_like(best)
    for g in range(1, N_GROUPS):
        better = scores[g] > best
        gi = jnp.where(better, float(g), gi)
        best = jnp.where(better, scores[g], best)

    def pick(rows, j):
        return jnp.where(gi == 0.0, rows[j],
                         jnp.where(gi == 1.0, rows[4 + j], jnp.where(gi == 2.0, rows[8 + j], rows[12 + j])))

    v = [pick(sel, j) for j in range(GROUP_SIZE)]
    r = [pick(raw, j) for j in range(GROUP_SIZE)]
    m = []
    for j in range(GROUP_SIZE):
        ahead = jnp.zeros_like(best)
        for k in range(GROUP_SIZE):
            if k == j:
                continue
            beats = (v[k] >= v[j]) if k < j else (v[k] > v[j])
            ahead = ahead + jnp.where(beats, 1.0, 0.0)
        m.append(ahead < 2.0)
    pair = jnp.where(m[0], jnp.where(m[1], 0.0, jnp.where(m[2], 1.0, 2.0)),
                     jnp.where(m[1], jnp.where(m[2], 3.0, 4.0), 5.0))
    sa = jnp.where(m[0], r[0], jnp.where(m[1], r[1], r[2]))
    sb = jnp.where(m[3], r[3], jnp.where(m[2], r[2], r[1]))
    den = sa + sb
    bucket = gi * float(N_PAIRS) + pair
    zeros = jnp.zeros((5, lg.shape[1]), F32)
    return jnp.concatenate([bucket, sa / den, sb / den, zeros], axis=0)


def _tail(y, x, mod_ref, gpost_ref, gffn_ref, rwh_ref, rwl_ref, rb_ref, tri_ref, cin_ref,
          x1_ref, tok_ref, route_ref, cnt_ref):
    x1 = x + mod_ref[2:3, :] * _rms(y, gpost_ref[...])
    x1_ref[...] = x1
    t = _rms(x1, gffn_ref[...]) * (1.0 + mod_ref[4:5, :]) + mod_ref[3:4, :]
    tok_ref[...] = t
    t_hi, t_lo = _split_bf16(t)
    lg = lax.dot_general(rwh_ref[...], t_hi, _NT, preferred_element_type=F32)
    lg += lax.dot_general(rwl_ref[...], t_hi, _NT, preferred_element_type=F32)
    lg += lax.dot_general(rwh_ref[...], t_lo, _NT, preferred_element_type=F32)
    rt = _route(lg, rb_ref[...])
    tm = rt.shape[1]

    @pl.when(pl.program_id(0) == 0)
    def _():
        cnt_ref[...] = cin_ref[...]

    bid = lax.broadcasted_iota(jnp.int32, (CNT_ROWS, tm), 0).astype(F32)
    onehot = jnp.where(bid == rt[0:1, :], 1.0, 0.0)
    cum = jnp.dot(onehot.astype(BF16), tri_ref[...], preferred_element_type=F32)
    rank = jnp.sum(onehot * (cum + cnt_ref[:, 0:1]), axis=0, keepdims=True) - 1.0
    cnt_ref[...] = cnt_ref[...] + cum[:, tm - 1:tm]
    route_ref[...] = jnp.concatenate([rt[0:3, :], rank, jnp.zeros((4, tm), F32)], axis=0)


def _conv_tail_kernel(u_ref, up_ref, un_ref, x_ref, mod_ref, wdw_ref, bdw_ref, lng_ref, lnb_ref,
                      wout_ref, bout_ref, gpost_ref, gffn_ref, rwh_ref, rwl_ref, rb_ref, tri_ref, cin_ref,
                      x1_ref, tok_ref, route_ref, cnt_ref, ubuf, ush, cacc, *, tm, tiles_per_seq, chunk):
    t = pl.program_id(0) % tiles_per_seq
    prev_ok = jnp.where(t > 0, 1.0, 0.0)
    next_ok = jnp.where(t < tiles_per_seq - 1, 1.0, 0.0)
    ubuf[0:HALO, :] = up_ref[...].astype(F32) * prev_ok
    ubuf[HALO:HALO + tm, :] = u_ref[...].astype(F32)
    ubuf[HALO + tm:, :] = un_ref[...].astype(F32) * next_ok

    base = HALO - CONV_PAD
    first = True
    for s in range(8):
        taps = [k for k in range(CONV_WIDTH) if (k + base) % 8 == s]
        if not taps:
            continue
        span = tm + 8 * ((CONV_WIDTH - 1 + base) // 8)
        if s == 0:
            src = ubuf
        else:
            ush[0:span, :] = ubuf[s:s + span, :]
            src = ush

        def body(c, carry, taps=taps, src=src, first=first):
            r0 = pl.multiple_of(c * chunk, chunk)
            acc = None if first else cacc[pl.ds(r0, chunk), :]
            for k in taps:
                q = (k + base) // 8
                term = src[pl.ds(r0 + 8 * q, chunk), :] * wdw_ref[k:k + 1, :]
                acc = term if acc is None else acc + term
            cacc[pl.ds(r0, chunk), :] = acc
            return carry

        lax.fori_loop(0, tm // chunk, body, 0)
        first = False

    y = cacc[...] + bdw_ref[...]
    mu = jnp.mean(y, axis=-1, keepdims=True)
    yc = y - mu
    var = jnp.mean(yc * yc, axis=-1, keepdims=True)
    yn = yc * lax.rsqrt(var + EPS) * lng_ref[...] + lnb_ref[...]
    act = yn * _sigmoid(yn)
    z = jnp.dot(act.astype(BF16), wout_ref[...], preferred_element_type=F32) + bout_ref[...]
    _tail(z, x_ref[...], mod_ref, gpost_ref, gffn_ref, rwh_ref, rwl_ref, rb_ref, tri_ref, cin_ref,
          x1_ref, tok_ref, route_ref, cnt_ref)


def _tail_common_specs(d, tm):
    const = lambda i: (0, 0)
    return [pl.BlockSpec((1, d), const), pl.BlockSpec((1, d), const),
            pl.BlockSpec((N_EXPERTS, d), const), pl.BlockSpec((N_EXPERTS, d), const),
            pl.BlockSpec((N_EXPERTS, 1), const), pl.BlockSpec((tm, tm), const),
            pl.BlockSpec((CNT_ROWS, 128), const)]


def _tail_common_args(tailp, counts, tm):
    gpost, gffn, rwh, rwl, rb = tailp
    j = lax.broadcasted_iota(jnp.int32, (tm, tm), 0)
    t = lax.broadcasted_iota(jnp.int32, (tm, tm), 1)
    return [gpost, gffn, rwh, rwl, rb, (j <= t).astype(BF16), counts]


def _tail_outs(n, d, tm):
    out_shape = [jax.ShapeDtypeStruct((n, d), F32), jax.ShapeDtypeStruct((n, d), F32),
                 jax.ShapeDtypeStruct((8, n), F32), jax.ShapeDtypeStruct((CNT_ROWS, 128), F32)]
    out_specs = [pl.BlockSpec((tm, d), lambda i: (i, 0)), pl.BlockSpec((tm, d), lambda i: (i, 0)),
                 pl.BlockSpec((8, tm), lambda i: (0, i)), pl.BlockSpec((CNT_ROWS, 128), lambda i: (0, 0))]
    return out_shape, out_specs


def _conv_tail(u, x, mods, layer, conv, tailp, counts, *, tm, rows_per_batch, ctx, name):
    n, d = x.shape
    wdw, bdw, lng, lnb, wout, bout = conv
    tiles_per_seq = rows_per_batch // tm
    hb = tm // HALO
    n_hb = n // HALO
    if ctx:
        mod_idx = lambda i: (layer, 8, 0, 0)
    else:
        mod_idx = lambda i: (layer, i // tiles_per_seq, 0, 0)
    row = lambda i: (i, 0)
    const = lambda i: (0, 0)
    specs = [pl.BlockSpec((tm, d), row),
             pl.BlockSpec((HALO, d), lambda i: (jnp.maximum(i * hb - 1, 0), 0)),
             pl.BlockSpec((HALO, d), lambda i: (jnp.minimum((i + 1) * hb, n_hb - 1), 0)),
             pl.BlockSpec((tm, d), row),
             pl.BlockSpec((None, None, 6, d), mod_idx),
             pl.BlockSpec(wdw.shape, const), pl.BlockSpec((1, d), const),
             pl.BlockSpec((1, d), const), pl.BlockSpec((1, d), const),
             pl.BlockSpec(wout.shape, const), pl.BlockSpec((1, d), const)]
    specs += _tail_common_specs(d, tm)
    out_shape, out_specs = _tail_outs(n, d, tm)
    chunk = 16
    return pl.pallas_call(
        functools.partial(_conv_tail_kernel, tm=tm, tiles_per_seq=tiles_per_seq, chunk=chunk),
        out_shape=out_shape, grid=(n // tm,), in_specs=specs, out_specs=out_specs,
        scratch_shapes=[pltpu.VMEM((tm + 2 * HALO, d), F32), pltpu.VMEM((tm + 2 * HALO, d), F32),
                        pltpu.VMEM((tm, d), F32)],
        compiler_params=_cparams(1), name=name,
    )(u, u, u, x, mods, wdw, bdw, lng, lnb, wout, bout, *_tail_common_args(tailp, counts, tm))


def _attn_tail_kernel(o_ref, x_ref, mod_ref, wo_ref, gpost_ref, gffn_ref, rwh_ref, rwl_ref, rb_ref, tri_ref,
                      cin_ref, x1_ref, tok_ref, route_ref, cnt_ref):
    y = jnp.dot(o_ref[...], wo_ref[...], preferred_element_type=F32)
    _tail(y, x_ref[...], mod_ref, gpost_ref, gffn_ref, rwh_ref, rwl_ref, rb_ref, tri_ref, cin_ref,
          x1_ref, tok_ref, route_ref, cnt_ref)


def _attn_tail(o, x, mods, layer, wo, tailp, counts, *, tm, rows_per_batch, name):
    n, d = x.shape
    tiles_per_batch = rows_per_batch // tm
    mod_idx = lambda i: (layer, i // tiles_per_batch, 0, 0)
    row = lambda i: (i, 0)
    const = lambda i: (0, 0)
    specs = [pl.BlockSpec((tm, d), row), pl.BlockSpec((tm, d), row),
             pl.BlockSpec((None, None, 6, d), mod_idx), pl.BlockSpec(wo.shape, const)]
    specs += _tail_common_specs(d, tm)
    out_shape, out_specs = _tail_outs(n, d, tm)
    return pl.pallas_call(
        _attn_tail_kernel, out_shape=out_shape, grid=(n // tm,), in_specs=specs, out_specs=out_specs,
        compiler_params=_cparams(1), name=name,
    )(o, x, mods, wo, *_tail_common_args(tailp, counts, tm))


def _window_plan(rows):
    n_blocks = rows // Q_ROWS
    plan = []
    for blk in (0, 1, n_blocks - 1):
        slab = min(max(Q_ROWS * blk - WIN_H // 2, 0), rows - SLAB_ROWS)
        per_row = []
        for rl in range(Q_ROWS):
            r = Q_ROWS * blk + rl
            r0 = min(max(r - WIN_H // 2, 0), rows - WIN_H)
            pairs = []
            for kp in range(SLAB_ROWS // 2):
                kr = slab + 2 * kp
                inside = (r0 <= kr < r0 + WIN_H, r0 <= kr + 1 < r0 + WIN_H)
                pairs.append((min(max(kr - r + WIN_H, 0), 2 * WIN_H - 1), inside))
            per_row.append(pairs)
        plan.append(per_row)
    return plan


def _natten_kernel(q_ref, k_ref, v_ref, kc_ref, vc_ref, c2_ref, o_ref, bias, *, rows):
    hw = 2 * HEAD_DIM
    nq = Q_ROWS * GRID_W
    n_slab = SLAB_ROWS * GRID_W
    n_blocks = rows // Q_ROWS
    lane = lax.broadcasted_iota(jnp.int32, (1, hw), 1)
    low = lane < HEAD_DIM

    @pl.when(pl.program_id(1) == 0)
    def _():
        for var, per_row in enumerate(_window_plan(rows)):
            for h in range(2):
                for rl, pairs in enumerate(per_row):
                    for kp, (entry, inside) in enumerate(pairs):
                        if inside[0] or inside[1]:
                            blk = c2_ref[h, entry]
                            if not inside[0]:
                                blk = jnp.where(low, MASK_BIAS, blk)
                            if not inside[1]:
                                blk = jnp.where(low, blk, MASK_BIAS)
                        else:
                            blk = jnp.full((GRID_W, hw), MASK_BIAS, F32)
                        bias[var, h * nq + rl * GRID_W:h * nq + (rl + 1) * GRID_W, kp * hw:(kp + 1) * hw] = blk

    kc = kc_ref[...]
    vc = vc_ref[...]

    def body(blk, carry):
        slab = jnp.clip(Q_ROWS * blk - WIN_H // 2, 0, rows - SLAB_ROWS)
        var = jnp.where(blk == 0, 0, jnp.where(blk == n_blocks - 1, 2, 1))
        q0 = pl.multiple_of(blk * nq, nq)
        k0 = pl.multiple_of(slab * GRID_W, GRID_W)
        q2 = q_ref[pl.ds(q0, nq), :]
        zero = jnp.zeros_like(q2)
        qm = jnp.concatenate([jnp.where(low, q2, zero), jnp.where(low, zero, q2)], axis=0)
        kw = k_ref[pl.ds(k0, n_slab), :]
        vw = v_ref[pl.ds(k0, n_slab), :]
        s_loc = lax.dot_general(qm, kw, _NT, preferred_element_type=F32) + bias[var]
        s_ctx = lax.dot_general(qm, kc, _NT, preferred_element_type=F32)
        m = jnp.maximum(jnp.max(s_loc, axis=-1, keepdims=True), jnp.max(s_ctx, axis=-1, keepdims=True))
        p_loc = jnp.exp(s_loc - m)
        p_ctx = jnp.exp(s_ctx - m)
        den = jnp.sum(p_loc, axis=-1, keepdims=True) + jnp.sum(p_ctx, axis=-1, keepdims=True)
        acc = jnp.dot(p_loc.astype(BF16), vw, preferred_element_type=F32)
        acc += jnp.dot(p_ctx.astype(BF16), vc, preferred_element_type=F32)
        acc = acc / den
        o_ref[pl.ds(q0, nq), :] = jnp.where(low, acc[:nq], acc[nq:]).astype(BF16)
        return carry

    lax.fori_loop(0, n_blocks, body, 0, unroll=4)


def _natten(q, k, v, kc, vc, c2):
    b, s, d = q.shape
    l = kc.shape[1]
    rows = s // GRID_W
    hw = 2 * HEAD_DIM
    lat = pl.BlockSpec((None, s, hw), lambda hp, bi: (bi, 0, hp))
    cspec = pl.BlockSpec((None, l, hw), lambda hp, bi: (bi, 0, hp))
    return pl.pallas_call(
        functools.partial(_natten_kernel, rows=rows),
        out_shape=jax.ShapeDtypeStruct((b, s, d), BF16),
        grid=(d // hw, b),
        in_specs=[lat, lat, lat, cspec, cspec,
                  pl.BlockSpec((2, 2 * WIN_H, GRID_W, hw), lambda hp, bi: (hp, 0, 0, 0))],
        out_specs=lat,
        scratch_shapes=[pltpu.VMEM((3, 2 * Q_ROWS * GRID_W, SLAB_ROWS * GRID_W), F32)],
        compiler_params=_cparams(2), name="natten",
    )(q, k, v, kc, vc, c2)


def _bias_kernel(r_ref, oh_ref, mask_ref, o_ref):
    r = r_ref[...]
    r1 = r.astype(BF16)
    r2 = (r - r1.astype(F32)).astype(BF16)
    r3 = (r - r1.astype(F32) - r2.astype(F32)).astype(BF16)
    oh = oh_ref[...]
    acc = jnp.dot(r1, oh, preferred_element_type=F32)
    acc += jnp.dot(r2, oh, preferred_element_type=F32)
    acc += jnp.dot(r3, oh, preferred_element_type=F32)
    o_ref[...] = acc + mask_ref[...]


def _bias_pairs(rpb):
    n_h, n_dr, n_dc = rpb.shape
    col = jnp.arange(GRID_W)[:, None]
    kcol = jnp.arange(GRID_W)[None, :]
    cs = jnp.clip(col - WIN_W // 2, 0, GRID_W - WIN_W)
    valid = (kcol >= cs) & (kcol < cs + WIN_W)
    dc = jnp.arange(32)[:, None, None]
    onehot = ((kcol - col + WIN_W - 1)[None] == dc) & valid[None]
    onehot = onehot.reshape(32, GRID_W * GRID_W).astype(BF16)
    maskadd = jnp.where(valid, 0.0, MASK_BIAS).reshape(1, GRID_W * GRID_W).astype(F32)
    rp = jnp.zeros((n_h, 16, 32), F32).at[:, :n_dr, :n_dc].set(rpb)
    flat = pl.pallas_call(
        _bias_kernel,
        out_shape=jax.ShapeDtypeStruct((n_h, 16, GRID_W * GRID_W), F32),
        grid=(n_h,),
        in_specs=[pl.BlockSpec((None, 16, 32), lambda h: (h, 0, 0)),
                  pl.BlockSpec((32, GRID_W * GRID_W), lambda h: (0, 0)),
                  pl.BlockSpec((1, GRID_W * GRID_W), lambda h: (0, 0))],
        out_specs=pl.BlockSpec((None, 16, GRID_W * GRID_W), lambda h: (h, 0, 0)),
        compiler_params=_cparams(1), name="bias_expand",
    )(rp, onehot, maskadd)
    tab = flat[:, :n_dr].reshape(n_h, n_dr, GRID_W, GRID_W)
    masked = jnp.full((n_h, 1, GRID_W, GRID_W), MASK_BIAS, F32)
    ext = jnp.concatenate([masked, tab, masked], axis=1)
    return jnp.concatenate([ext[:, :-1], ext[:, 1:]], axis=-1)


def _moe_kernel(ea_ref, eb_ref, nt_ref, xs_ref, wga_ref, wua_ref, wda_ref, wgb_ref, wub_ref, wdb_ref,
                y_ref, cga, cua, cda, cgb, cub, cdb, *, d):
    i = pl.program_id(0)
    prev = jnp.maximum(i - 1, 0)
    fresh = (i == 0) | (ea_ref[i] != ea_ref[prev]) | (eb_ref[i] != eb_ref[prev])

    @pl.when(fresh)
    def _():
        for src, dst in ((wga_ref, cga), (wua_ref, cua), (wda_ref, cda),
                         (wgb_ref, cgb), (wub_ref, cub), (wdb_ref, cdb)):
            dst[...] = src[...].astype(BF16)

    @pl.when(i < nt_ref[0])
    def _():
        xt = xs_ref[:, 0:d].astype(BF16)

        def expert(wg_ref, wu_ref, wd_ref):
            g = jnp.dot(xt, wg_ref[...], preferred_element_type=F32)
            u = jnp.dot(xt, wu_ref[...], preferred_element_type=F32)
            h = (g * _sigmoid(g)) * u
            return jnp.dot(h.astype(BF16), wd_ref[...], preferred_element_type=F32)

        ya = expert(cga, cua, cda)
        yb = expert(cgb, cub, cdb)
        y_ref[...] = xs_ref[:, d + 1:d + 2] * ya + xs_ref[:, d + 2:d + 3] * yb

    @pl.when(i >= nt_ref[0])
    def _():
        y_ref[...] = jnp.zeros_like(y_ref)


def _moe(ea, eb, nt, xs, layer, wg, wu, wd):
    npad, wide = xs.shape
    d = wide - 128
    de = wg.shape[3]
    sel_a = lambda i, ea, eb, nt: (layer, ea[i], 0, 0)
    sel_b = lambda i, ea, eb, nt: (layer, eb[i], 0, 0)
    up = (None, None, d, de)
    down = (None, None, de, d)
    grid_spec = pltpu.PrefetchScalarGridSpec(
        num_scalar_prefetch=3, grid=(npad // TM_MOE,),
        in_specs=[pl.BlockSpec((TM_MOE, wide), lambda i, ea, eb, nt: (i, 0)),
                  pl.BlockSpec(up, sel_a), pl.BlockSpec(up, sel_a), pl.BlockSpec(down, sel_a),
                  pl.BlockSpec(up, sel_b), pl.BlockSpec(up, sel_b), pl.BlockSpec(down, sel_b)],
        out_specs=pl.BlockSpec((TM_MOE, d), lambda i, ea, eb, nt: (i, 0)),
        scratch_shapes=[pltpu.VMEM((d, de), BF16), pltpu.VMEM((d, de), BF16), pltpu.VMEM((de, d), BF16),
                        pltpu.VMEM((d, de), BF16), pltpu.VMEM((d, de), BF16), pltpu.VMEM((de, d), BF16)])
    return pl.pallas_call(
        functools.partial(_moe_kernel, d=d), out_shape=jax.ShapeDtypeStruct((npad, d), F32),
        grid_spec=grid_spec, compiler_params=_cparams(1), name="moe",
    )(ea, eb, nt, xs, wg, wu, wd, wg, wu, wd)


def _plan_kernel(route_ref, cnt_ref, eab_ref, pos_ref, meta_ref):
    cnt = cnt_ref[...]
    tiles = jnp.floor((cnt + float(TM_MOE - 1)) * (1.0 / TM_MOE))
    r = lax.broadcasted_iota(jnp.int32, (CNT_ROWS, CNT_ROWS), 0)
    c = lax.broadcasted_iota(jnp.int32, (CNT_ROWS, CNT_ROWS), 1)
    before = jnp.where(c < r, 1.0, 0.0).astype(BF16)
    first = jnp.dot(before, tiles.astype(BF16), preferred_element_type=F32)
    first_c, tiles_c = first[:, 0:1], tiles[:, 0:1]
    nt = jnp.sum(tiles_c, axis=0, keepdims=True)
    n = route_ref.shape[1]
    bid = lax.broadcasted_iota(jnp.int32, (CNT_ROWS, n), 0).astype(F32)
    base = jnp.sum(jnp.where(bid == route_ref[0:1, :], first_c * float(TM_MOE), 0.0), axis=0, keepdims=True)
    pos_ref[...] = (base + route_ref[3:4, :]).astype(jnp.int32)
    ti = jnp.minimum(lax.broadcasted_iota(jnp.int32, (CNT_ROWS, META_LANES), 1).astype(F32), nt - 1.0)
    mine = (ti >= first_c) & (ti < first_c + tiles_c)
    ea = jnp.sum(jnp.where(mine, eab_ref[:, 0:1], 0.0), axis=0, keepdims=True)
    eb = jnp.sum(jnp.where(mine, eab_ref[:, 1:2], 0.0), axis=0, keepdims=True)
    rows = [ea, eb, jnp.broadcast_to(nt, (1, META_LANES)), jnp.zeros((5, META_LANES), F32)]
    meta_ref[...] = jnp.concatenate(rows, axis=0).astype(jnp.int32)


def _plan(route, counts):
    n = route.shape[1]
    chunk = max(ck for ck in (2048, 1024, 512) if n % ck == 0)
    eab = jnp.zeros((CNT_ROWS, 128), F32)
    eab = eab.at[:N_BUCKETS, 0].set(jnp.asarray(
        [GROUP_SIZE * (bk // N_PAIRS) + PAIR_LO[bk % N_PAIRS] for bk in range(N_BUCKETS)], F32))
    eab = eab.at[:N_BUCKETS, 1].set(jnp.asarray(
        [GROUP_SIZE * (bk // N_PAIRS) + PAIR_HI[bk % N_PAIRS] for bk in range(N_BUCKETS)], F32))
    const = lambda i: (0, 0)
    return pl.pallas_call(
        _plan_kernel,
        out_shape=[jax.ShapeDtypeStruct((1, n), jnp.int32), jax.ShapeDtypeStruct((8, META_LANES), jnp.int32)],
        grid=(n // chunk,),
        in_specs=[pl.BlockSpec((8, chunk), lambda i: (0, i)), pl.BlockSpec((CNT_ROWS, 128), const),
                  pl.BlockSpec((CNT_ROWS, 128), const)],
        out_specs=[pl.BlockSpec((1, chunk), lambda i: (0, i)), pl.BlockSpec((8, META_LANES), const)],
        compiler_params=_cparams(1), name="plan",
    )(route, counts, eab)


def _sort_kernel(pos_ref, tok_ref, route_ref, xs_in_ref, xs_ref, buf, sem, *, tm, d, n_steps):
    del xs_in_ref
    i = pl.program_id(0)
    slot = i % 2

    def wait_rows(s):
        pltpu.make_async_copy(buf.at[s], xs_ref.at[pl.ds(0, tm)], sem.at[s]).wait()

    @pl.when(i >= 2)
    def _():
        wait_rows(slot)

    cols = jnp.concatenate([route_ref[...], jnp.zeros((128 - 8, tm), F32)], axis=0).T
    buf[slot, :, 0:d] = tok_ref[...]
    buf[slot, :, d:] = cols

    def issue(t, carry):
        p = pos_ref[i * tm + t]
        pltpu.make_async_copy(buf.at[slot, pl.ds(t, 1)], xs_ref.at[pl.ds(p, 1)], sem.at[slot]).start()
        return carry

    lax.fori_loop(0, tm, issue, 0, unroll=8)

    @pl.when(i == n_steps - 1)
    def _():
        wait_rows(slot)
        if n_steps >= 2:
            wait_rows(1 - slot)


def _sort_rows(pos, tok, route, xs):
    n, d = tok.shape
    tm = TM_LAT
    n_steps = n // tm
    grid_spec = pltpu.PrefetchScalarGridSpec(
        num_scalar_prefetch=1, grid=(n_steps,),
        in_specs=[pl.BlockSpec((tm, d), lambda i, pos: (i, 0)), pl.BlockSpec((8, tm), lambda i, pos: (0, i)),
                  pl.BlockSpec(memory_space=pl.ANY)],
        out_specs=pl.BlockSpec(memory_space=pl.ANY),
        scratch_shapes=[pltpu.VMEM((2, tm, d + 128), F32), pltpu.SemaphoreType.DMA((2,))])
    return pl.pallas_call(
        functools.partial(_sort_kernel, tm=tm, d=d, n_steps=n_steps),
        out_shape=jax.ShapeDtypeStruct(xs.shape, F32), grid_spec=grid_spec,
        input_output_aliases={3: 0}, compiler_params=_cparams(1), name="sort_rows",
    )(pos, tok, route, xs)


def _unsort_kernel(pos_ref, ys_ref, o_ref, sem, *, tm):
    i = pl.program_id(0)

    def issue(t, carry):
        p = pos_ref[i * tm + t]
        pltpu.make_async_copy(ys_ref.at[pl.ds(p, 1)], o_ref.at[pl.ds(t, 1)], sem).start()
        return carry

    lax.fori_loop(0, tm, issue, 0, unroll=8)
    pltpu.make_async_copy(ys_ref.at[pl.ds(0, tm)], o_ref, sem).wait()


def _unsort_rows(pos, ys):
    n = pos.shape[0]
    d = ys.shape[1]
    tm = TM_LAT
    grid_spec = pltpu.PrefetchScalarGridSpec(
        num_scalar_prefetch=1, grid=(n // tm,),
        in_specs=[pl.BlockSpec(memory_space=pl.ANY)],
        out_specs=pl.BlockSpec((tm, d), lambda i, pos: (i, 0)),
        scratch_shapes=[pltpu.SemaphoreType.DMA])
    return pl.pallas_call(
        functools.partial(_unsort_kernel, tm=tm), out_shape=jax.ShapeDtypeStruct((n, d), F32),
        grid_spec=grid_spec, compiler_params=_cparams(1), name="unsort_rows",
    )(pos, ys)


def _moe_layer(toks, routes, counts, layer, wg, wu, wd):
    d = toks[0].shape[1]
    n = sum(t.shape[0] for t in toks)
    n_tiles_max = n // TM_MOE + N_BUCKETS
    assert n_tiles_max <= META_LANES
    route = routes[0] if len(routes) == 1 else jnp.concatenate(routes, axis=1)
    pos, meta = _plan(route, counts)
    pos = pos.reshape(n)
    xs = jnp.zeros((n_tiles_max * TM_MOE, d + 128), F32)
    start = 0
    for tok, rt in zip(toks, routes):
        xs = _sort_rows(pos[start:start + tok.shape[0]], tok, rt, xs)
        start += tok.shape[0]
    ys = _moe(meta[0, :n_tiles_max], meta[1, :n_tiles_max], meta[2, :1], xs, layer, wg, wu, wd)
    return _unsort_rows(pos, ys)


def _final_kernel(x_ref, f_ref, mod_ref, gpost_ref, o_ref):
    o_ref[...] = x_ref[...] + mod_ref[5:6, :] * _rms(f_ref[...], gpost_ref[...])


def _final(x, f, mods, layer, gpost, *, tm, rows_per_batch):
    n, d = x.shape
    tiles_per_batch = rows_per_batch // tm
    row = lambda i: (i, 0)
    return pl.pallas_call(
        _final_kernel, out_shape=jax.ShapeDtypeStruct((n, d), F32), grid=(n // tm,),
        in_specs=[pl.BlockSpec((tm, d), row), pl.BlockSpec((tm, d), row),
                  pl.BlockSpec((None, None, 6, d), lambda i: (layer, i // tiles_per_batch, 0, 0)),
                  pl.BlockSpec((1, d), lambda i: (0, 0))],
        out_specs=pl.BlockSpec((tm, d), row),
        compiler_params=_cparams(1), name="final",
    )(x, f, mods, gpost)


def kernel(x, c, ctx, c_ctx, w_ada, b_ada, g_mix_pre, g_mix_post, g_ffn_pre, g_ffn_post, conv_w_in, conv_b_in, conv_w_dw, conv_b_dw, conv_ln_g, conv_ln_b, conv_w_out, conv_b_out, attn_w_qkv, attn_w_o, attn_rpb, router_w, router_b, exp_w_gate, exp_w_up, exp_w_down):
    b, s, d = x.shape
    l = ctx.shape[1]
    n, nc = b * s, b * l
    depth = w_ada.shape[0]
    rows = s // GRID_W
    assert depth == 2 and b <= 8 and s % TM_LAT == 0 and l == TM_CTX
    assert rows % Q_ROWS == 0 and rows >= SLAB_ROWS + Q_ROWS
    tm_lat = TM_LAT

    cc = jnp.zeros((MOD_ROWS, d), F32).at[:b].set(c).at[8].set(c_ctx)
    mods = _ada(cc, w_ada, b_ada).reshape(depth, MOD_ROWS, 6, d)

    row = lambda v: v.reshape(1, d)
    rwt = router_w.T
    rwh = rwt.astype(BF16)
    rwl = (rwt - rwh.astype(F32)).astype(BF16)
    rb = router_b.reshape(N_EXPERTS, 1)
    no_counts = jnp.zeros((CNT_ROWS, 128), F32)

    xl = x.reshape(n, d)
    xc = ctx.reshape(nc, d)

    w_in = conv_w_in[0].astype(BF16)
    b_in = conv_b_in[0].reshape(1, 2 * d)
    conv = (conv_w_dw[0], row(conv_b_dw[0]), row(conv_ln_g[0]), row(conv_ln_b[0]),
            conv_w_out[0].astype(BF16), row(conv_b_out[0]))
    tailp = (row(g_mix_post[0]), row(g_ffn_pre[0]), rwh, rwl, rb)
    (u,) = _proj(xl, mods, 0, row(g_mix_pre[0]), w_in, tm=tm_lat, rows_per_batch=s, ctx=False,
                 bias=b_in, name="conv_in")
    (uc,) = _proj(xc, mods, 0, row(g_mix_pre[0]), w_in, tm=TM_CTX, rows_per_batch=l, ctx=True,
                  bias=b_in, name="conv_in_ctx")
    x1, tok, route, counts = _conv_tail(u, xl, mods, 0, conv, tailp, no_counts, tm=tm_lat, rows_per_batch=s,
                                        ctx=False, name="conv_tail")
    xc1, tokc, routec, counts = _conv_tail(uc, xc, mods, 0, conv, tailp, counts, tm=TM_CTX, rows_per_batch=l,
                                           ctx=True, name="conv_tail_ctx")
    f0 = _moe_layer([tok, tokc], [route, routec], counts, 0, exp_w_gate, exp_w_up, exp_w_down)

    w_qkv = attn_w_qkv[0].astype(BF16)
    x2, q, k, v = _proj(x1, mods, 1, row(g_mix_pre[1]), w_qkv, tm=tm_lat, rows_per_batch=s, ctx=False,
                        resid=(f0, row(g_ffn_post[0]), 0), n_out=3, name="qkv")
    _, kc, vc = _proj(xc1, mods, 1, row(g_mix_pre[1]), w_qkv[:, d:], tm=TM_CTX, rows_per_batch=l, ctx=True,
                      resid=(f0, row(g_ffn_post[0]), n), n_out=2, name="kv_ctx")
    c2 = _bias_pairs(attn_rpb[0])
    o = _natten(q.reshape(b, s, d), k.reshape(b, s, d), v.reshape(b, s, d),
                kc.reshape(b, l, d), vc.reshape(b, l, d), c2)
    tailp1 = (row(g_mix_post[1]), row(g_ffn_pre[1]), rwh, rwl, rb)
    x3, tok1, route1, counts1 = _attn_tail(o.reshape(n, d), x2, mods, 1, attn_w_o[0].astype(BF16), tailp1,
                                           no_counts, tm=tm_lat, rows_per_batch=s, name="attn_tail")
    f1 = _moe_layer([tok1], [route1], counts1, 1, exp_w_gate, exp_w_up, exp_w_down)
    out = _final(x3, f1, mods, 1, row(g_ffn_post[1]), tm=tm_lat, rows_per_batch=s)
    return out.reshape(b, s, d)
```

```python
import functools

import jax
import jax.numpy as jnp
from jax import lax
from jax.experimental import pallas as pl
from jax.experimental.pallas import tpu as pltpu

F32 = jnp.float32
BF16 = jnp.bfloat16

EPS = 1e-6
GRID_W = 64
WIN_H = 8
WIN_W = 16
N_HEADS = 16
HEAD_DIM = 64
N_EXPERTS = 16
N_GROUPS = 4
GROUP_SIZE = N_EXPERTS // N_GROUPS
N_PAIRS = 6
N_BUCKETS = N_GROUPS * N_PAIRS
PAIR_LO = (0, 0, 0, 1, 1, 2)
PAIR_HI = (1, 2, 3, 2, 3, 3)
CONV_WIDTH = 31
CONV_PAD = (CONV_WIDTH - 1) // 2
HALO = 16
MASK_BIAS = -1e30
Q_ROWS = 4
SLAB_ROWS = Q_ROWS + WIN_H
MOD_ROWS = 16
CNT_ROWS = 32
META_LANES = 256

TM_LAT = 512
TM_CTX = 256
TM_MOE = 256
VMEM_LIMIT = 56 * 1024 * 1024

_NT = (((1,), (1,)), ((), ()))


def _cparams(n_axes):
    return pltpu.CompilerParams(dimension_semantics=("arbitrary",) * n_axes,
                                vmem_limit_bytes=VMEM_LIMIT)


def _sigmoid(v):
    return 1.0 / (1.0 + jnp.exp(-v))


def _rms(v, g):
    return v * lax.rsqrt(jnp.mean(v * v, axis=-1, keepdims=True) + EPS) * g


def _split_bf16(v):
    hi = v.astype(BF16)
    lo = (v - hi.astype(F32)).astype(BF16)
    return hi, lo


def _ada_kernel(c_ref, w_ref, b_ref, o_ref):
    a = c_ref[...]
    a = a * _sigmoid(a)
    a_hi, a_lo = _split_bf16(a)
    w_hi, w_lo = _split_bf16(w_ref[...])
    acc = jnp.dot(a_hi, w_hi, preferred_element_type=F32)
    acc += jnp.dot(a_lo, w_hi, preferred_element_type=F32)
    acc += jnp.dot(a_hi, w_lo, preferred_element_type=F32)
    o_ref[...] = acc + b_ref[...]


def _ada(cc, w_ada, b_ada):
    depth, d, n = w_ada.shape
    bn = 1024
    return pl.pallas_call(
        _ada_kernel,
        out_shape=jax.ShapeDtypeStruct((depth, MOD_ROWS, n), F32),
        grid=(depth, n // bn),
        in_specs=[pl.BlockSpec((MOD_ROWS, d), lambda l, j: (0, 0)),
                  pl.BlockSpec((None, d, bn), lambda l, j: (l, 0, j)),
                  pl.BlockSpec((None, 1, bn), lambda l, j: (l, 0, j))],
        out_specs=pl.BlockSpec((None, MOD_ROWS, bn), lambda l, j: (l, 0, j)),
        compiler_params=_cparams(2),
        name="ada",
    )(cc, w_ada, b_ada.reshape(depth, 1, n))


def _start_row_gather(pos_ref, src_ref, buf, sem, first_row, slot, tm):
    for t in range(tm):
        p = pos_ref[first_row + t]
        pltpu.make_async_copy(src_ref.at[pl.ds(p, 1)], buf.at[slot, pl.ds(t, 1)], sem.at[slot]).start()


def _wait_row_gather(src_ref, buf, sem, slot, tm):
    pltpu.make_async_copy(src_ref.at[pl.ds(0, tm)], buf.at[slot], sem.at[slot]).wait()


def _gathered_tile(pos_ref, src_ref, buf, sem, row0, tm):
    i = pl.program_id(0)
    slot = i % 2

    @pl.when(i == 0)
    def _():
        _start_row_gather(pos_ref, src_ref, buf, sem, row0, 0, tm)

    _wait_row_gather(src_ref, buf, sem, slot, tm)
    _start_row_gather(pos_ref, src_ref, buf, sem, row0 + (i + 1) * tm, 1 - slot, tm)

    @pl.when(i == pl.num_programs(0) - 1)
    def _():
        _wait_row_gather(src_ref, buf, sem, 1 - slot, tm)

    return buf[slot]


def _proj_kernel(*refs, fuse_resid, glu, d, tm, f_row0):
    refs = list(refs)
    if fuse_resid:
        pos_ref = refs.pop(0)
    x_ref = refs.pop(0)
    x = x_ref[...]
    if fuse_resid:
        ys_ref, modp_ref, gpost_ref = refs.pop(0), refs.pop(0), refs.pop(0)
    mod_ref, gpre_ref, w_ref = refs.pop(0), refs.pop(0), refs.pop(0)
    if glu:
        b_ref = refs.pop(0)
    if fuse_resid:
        xn_ref = refs.pop(0)
        fbuf, sem = refs.pop(-2), refs.pop(-1)
        f = _gathered_tile(pos_ref, ys_ref, fbuf, sem, f_row0, tm)
        x = x + modp_ref[5:6, :] * _rms(f, gpost_ref[...])
        xn_ref[...] = x
    h = _rms(x, gpre_ref[...]) * (1.0 + mod_ref[1:2, :]) + mod_ref[0:1, :]
    z = jnp.dot(h.astype(BF16), w_ref[...], preferred_element_type=F32)
    if glu:
        z = z + b_ref[...]
        refs[0][...] = (z[:, :d] * _sigmoid(z[:, d:])).astype(BF16)
    else:
        n_out = len(refs)
        for j, o_ref in enumerate(refs):
            part = z[:, j * d:(j + 1) * d]
            if n_out == 3 and j == 0:
                part = part * (HEAD_DIM ** -0.5)
            o_ref[...] = part.astype(BF16)


def _proj(x, mods, layer, gpre, w, *, tm, rows_per_batch, ctx, bias=None, resid=None, n_out=1, name):
    n, d = x.shape
    tiles_per_batch = rows_per_batch // tm
    if ctx:
        mod_idx = lambda i, *_: (layer, 8, 0, 0)
    else:
        mod_idx = lambda i, *_: (layer, i // tiles_per_batch, 0, 0)
    row = lambda i, *_: (i, 0)
    const = lambda i, *_: (0, 0)
    args, specs, scratch, f_row0 = [x], [pl.BlockSpec((tm, d), row)], [], 0
    if resid is not None:
        pos, ys, gpost, f_row0 = resid
        if ctx:
            modp_idx = lambda i, *_: (layer - 1, 8, 0, 0)
        else:
            modp_idx = lambda i, *_: (layer - 1, i // tiles_per_batch, 0, 0)
        args = [pos] + args + [ys, mods, gpost]
        specs += [pl.BlockSpec(memory_space=pl.ANY), pl.BlockSpec((None, None, 6, d), modp_idx),
                  pl.BlockSpec((1, d), const)]
        scratch = [pltpu.VMEM((2, tm, d), F32), pltpu.SemaphoreType.DMA((2,))]
    args += [mods, gpre, w]
    specs += [pl.BlockSpec((None, None, 6, d), mod_idx), pl.BlockSpec((1, d), const),
              pl.BlockSpec(w.shape, const)]
    glu = bias is not None
    if glu:
        args.append(bias)
        specs.append(pl.BlockSpec(bias.shape, const))
    out_shape, out_specs = [], []
    if resid is not None:
        out_shape.append(jax.ShapeDtypeStruct((n, d), F32))
        out_specs.append(pl.BlockSpec((tm, d), row))
    for _ in range(n_out):
        out_shape.append(jax.ShapeDtypeStruct((n, d), BF16))
        out_specs.append(pl.BlockSpec((tm, d), row))
    grid_spec = pltpu.PrefetchScalarGridSpec(
        num_scalar_prefetch=0 if resid is None else 1, grid=(n // tm,),
        in_specs=specs, out_specs=out_specs, scratch_shapes=scratch)
    return pl.pallas_call(
        functools.partial(_proj_kernel, fuse_resid=resid is not None, glu=glu, d=d, tm=tm, f_row0=f_row0),
        out_shape=out_shape, grid_spec=grid_spec, compiler_params=_cparams(1), name=name,
    )(*args)


def _route(lg, rb):
    s = _sigmoid(lg)
    ssel = s + rb
    sel = [ssel[e:e + 1, :] for e in range(N_EXPERTS)]
    raw = [s[e:e + 1, :] for e in range(N_EXPERTS)]
    scores = []
    for g in range(N_GROUPS):
        v0, v1, v2, v3 = sel[4 * g:4 * g + 4]
        a, b = jnp.maximum(v0, v1), jnp.minimum(v0, v1)
        c, e = jnp.maximum(v2, v3), jnp.minimum(v2, v3)
        scores.append(jnp.maximum(a, c) + jnp.maximum(jnp.minimum(a, c), jnp.maximum(b, e)))
    best = scores[0]
    gi = jnp.zeros_like(best)
    for g in range(1, N_GROUPS):
        better = scores[g] > best
        gi = jnp.where(better, float(g), gi)
        best = jnp.where(better, scores[g], best)

    def pick(rows, j):
        return jnp.where(gi == 0.0, rows[j],
                         jnp.where(gi == 1.0, rows[4 + j], jnp.where(gi == 2.0, rows[8 + j], rows[12 + j])))

    v = [pick(sel, j) for j in range(GROUP_SIZE)]
    r = [pick(raw, j) for j in range(GROUP_SIZE)]
    m = []
    for j in range(GROUP_SIZE):
        ahead = jnp.zeros_like(best)
        for k in range(GROUP_SIZE):
            if k == j:
                continue
            beats = (v[k] >= v[j]) if k < j else (v[k] > v[j])
            ahead = ahead + jnp.where(beats, 1.0, 0.0)
        m.append(ahead < 2.0)
    pair = jnp.where(m[0], jnp.where(m[1], 0.0, jnp.where(m[2], 1.0, 2.0)),
                     jnp.where(m[1], jnp.where(m[2], 3.0, 4.0), 5.0))
    sa = jnp.where(m[0], r[0], jnp.where(m[1], r[1], r[2]))
    sb = jnp.where(m[3], r[3], jnp.where(m[2], r[2], r[1]))
    den = sa + sb
    bucket = gi * float(N_PAIRS) + pair
    zeros = jnp.zeros((5, lg.shape[1]), F32)
    return jnp.concatenate([bucket, sa / den, sb / den, zeros], axis=0)


def _tail(y, x, mod_ref, gpost_ref, gffn_ref, rwh_ref, rwl_ref, rb_ref, tri_ref, cin_ref,
          x1_ref, tok_ref, route_ref, cnt_ref):
    x1 = x + mod_ref[2:3, :] * _rms(y, gpost_ref[...])
    x1_ref[...] = x1
    t = _rms(x1, gffn_ref[...]) * (1.0 + mod_ref[4:5, :]) + mod_ref[3:4, :]
    tok_ref[...] = t
    t_hi, t_lo = _split_bf16(t)
    lg = lax.dot_general(rwh_ref[...], t_hi, _NT, preferred_element_type=F32)
    lg += lax.dot_general(rwl_ref[...], t_hi, _NT, preferred_element_type=F32)
    lg += lax.dot_general(rwh_ref[...], t_lo, _NT, preferred_element_type=F32)
    rt = _route(lg, rb_ref[...])
    tm = rt.shape[1]

    @pl.when(pl.program_id(0) == 0)
    def _():
        cnt_ref[...] = cin_ref[...]

    bid = lax.broadcasted_iota(jnp.int32, (CNT_ROWS, tm), 0).astype(F32)
    onehot = jnp.where(bid == rt[0:1, :], 1.0, 0.0)
    cum = jnp.dot(onehot.astype(BF16), tri_ref[...], preferred_element_type=F32)
    rank = jnp.sum(onehot * (cum + cnt_ref[:, 0:1]), axis=0, keepdims=True) - 1.0
    cnt_ref[...] = cnt_ref[...] + cum[:, tm - 1:tm]
    route_ref[...] = jnp.concatenate([rt[0:3, :], rank, jnp.zeros((4, tm), F32)], axis=0)


def _conv_tail_kernel(u_ref, up_ref, un_ref, x_ref, mod_ref, wdw_ref, bdw_ref, lng_ref, lnb_ref,
                      wout_ref, bout_ref, gpost_ref, gffn_ref, rwh_ref, rwl_ref, rb_ref, tri_ref, cin_ref,
                      x1_ref, tok_ref, route_ref, cnt_ref, ubuf, ush, cacc, wb, *, tm, tiles_per_seq, chunk):
    t = pl.program_id(0) % tiles_per_seq
    prev_ok = jnp.where(t > 0, 1.0, 0.0)
    next_ok = jnp.where(t < tiles_per_seq - 1, 1.0, 0.0)
    ubuf[0:HALO, :] = up_ref[...].astype(F32) * prev_ok
    ubuf[HALO:HALO + tm, :] = u_ref[...].astype(F32)
    ubuf[HALO + tm:, :] = un_ref[...].astype(F32) * next_ok

    base = HALO - CONV_PAD
    span = tm + 8 * ((CONV_WIDTH - 1 + base) // 8)
    for s in range(1, 8):
        ush[s - 1, 0:span, :] = ubuf[s:s + span, :]

    @pl.when(pl.program_id(0) == 0)
    def _():
        for k in range(CONV_WIDTH):
            wb[k] = jnp.broadcast_to(wdw_ref[k:k + 1, :], (8, wb.shape[2]))

    n_lane_blocks = wb.shape[2] // 128

    def body(c, carry):
        r0 = pl.multiple_of(c * chunk, chunk)
        for lb in range(n_lane_blocks):
            lanes = slice(lb * 128, (lb + 1) * 128)
            acc = None
            for s in range(8):
                if s == 0:
                    win = ubuf[pl.ds(r0, chunk + span - tm), lanes]
                else:
                    win = ush[s - 1, pl.ds(r0, chunk + span - tm), lanes]
                win = win.reshape((chunk + span - tm) // 8, 8, 128)
                for q in range((span - tm) // 8 + 1):
                    k = 8 * q + s - base
                    if 0 <= k < CONV_WIDTH:
                        term = win[q:q + chunk // 8] * wb[k, :, lanes][None]
                        acc = term if acc is None else acc + term
            cacc[pl.ds(r0, chunk), lanes] = acc.reshape(chunk, 128)
        return carry

    lax.fori_loop(0, tm // chunk, body, 0)

    y = cacc[...] + bdw_ref[...]
    mu = jnp.mean(y, axis=-1, keepdims=True)
    yc = y - mu
    var = jnp.mean(yc * yc, axis=-1, keepdims=True)
    yn = yc * lax.rsqrt(var + EPS) * lng_ref[...] + lnb_ref[...]
    act = yn * _sigmoid(yn)
    z = jnp.dot(act.astype(BF16), wout_ref[...], preferred_element_type=F32) + bout_ref[...]
    _tail(z, x_ref[...], mod_ref, gpost_ref, gffn_ref, rwh_ref, rwl_ref, rb_ref, tri_ref, cin_ref,
          x1_ref, tok_ref, route_ref, cnt_ref)


def _tail_common_specs(d, tm):
    const = lambda i: (0, 0)
    return [pl.BlockSpec((1, d), const), pl.BlockSpec((1, d), const),
            pl.BlockSpec((N_EXPERTS, d), const), pl.BlockSpec((N_EXPERTS, d), const),
            pl.BlockSpec((N_EXPERTS, 1), const), pl.BlockSpec((tm, tm), const),
            pl.BlockSpec((CNT_ROWS, 128), const)]


def _tail_common_args(tailp, counts, tm):
    gpost, gffn, rwh, rwl, rb = tailp
    j = lax.broadcasted_iota(jnp.int32, (tm, tm), 0)
    t = lax.broadcasted_iota(jnp.int32, (tm, tm), 1)
    return [gpost, gffn, rwh, rwl, rb, (j <= t).astype(BF16), counts]


def _tail_outs(n, d, tm):
    out_shape = [jax.ShapeDtypeStruct((n, d), F32), jax.ShapeDtypeStruct((n, d), F32),
                 jax.ShapeDtypeStruct((8, n), F32), jax.ShapeDtypeStruct((CNT_ROWS, 128), F32)]
    out_specs = [pl.BlockSpec((tm, d), lambda i: (i, 0)), pl.BlockSpec((tm, d), lambda i: (i, 0)),
                 pl.BlockSpec((8, tm), lambda i: (0, i)), pl.BlockSpec((CNT_ROWS, 128), lambda i: (0, 0))]
    return out_shape, out_specs


def _conv_tail(u, x, mods, layer, conv, tailp, counts, *, tm, rows_per_batch, ctx, name):
    n, d = x.shape
    wdw, bdw, lng, lnb, wout, bout = conv
    tiles_per_seq = rows_per_batch // tm
    hb = tm // HALO
    n_hb = n // HALO
    if ctx:
        mod_idx = lambda i: (layer, 8, 0, 0)
    else:
        mod_idx = lambda i: (layer, i // tiles_per_seq, 0, 0)
    row = lambda i: (i, 0)
    const = lambda i: (0, 0)
    specs = [pl.BlockSpec((tm, d), row),
             pl.BlockSpec((HALO, d), lambda i: (jnp.maximum(i * hb - 1, 0), 0)),
             pl.BlockSpec((HALO, d), lambda i: (jnp.minimum((i + 1) * hb, n_hb - 1), 0)),
             pl.BlockSpec((tm, d), row),
             pl.BlockSpec((None, None, 6, d), mod_idx),
             pl.BlockSpec(wdw.shape, const), pl.BlockSpec((1, d), const),
             pl.BlockSpec((1, d), const), pl.BlockSpec((1, d), const),
             pl.BlockSpec(wout.shape, const), pl.BlockSpec((1, d), const)]
    specs += _tail_common_specs(d, tm)
    out_shape, out_specs = _tail_outs(n, d, tm)
    chunk = 64
    return pl.pallas_call(
        functools.partial(_conv_tail_kernel, tm=tm, tiles_per_seq=tiles_per_seq, chunk=chunk),
        out_shape=out_shape, grid=(n // tm,), in_specs=specs, out_specs=out_specs,
        scratch_shapes=[pltpu.VMEM((tm + 2 * HALO, d), F32), pltpu.VMEM((7, tm + 2 * HALO, d), F32),
                        pltpu.VMEM((tm, d), F32), pltpu.VMEM((CONV_WIDTH, 8, d), F32)],
        compiler_params=_cparams(1), name=name,
    )(u, u, u, x, mods, wdw, bdw, lng, lnb, wout, bout, *_tail_common_args(tailp, counts, tm))


def _attn_tail_kernel(o_ref, x_ref, mod_ref, wo_ref, gpost_ref, gffn_ref, rwh_ref, rwl_ref, rb_ref, tri_ref,
                      cin_ref, x1_ref, tok_ref, route_ref, cnt_ref):
    y = jnp.dot(o_ref[...], wo_ref[...], preferred_element_type=F32)
    _tail(y, x_ref[...], mod_ref, gpost_ref, gffn_ref, rwh_ref, rwl_ref, rb_ref, tri_ref, cin_ref,
          x1_ref, tok_ref, route_ref, cnt_ref)


def _attn_tail(o, x, mods, layer, wo, tailp, counts, *, tm, rows_per_batch, name):
    n, d = x.shape
    tiles_per_batch = rows_per_batch // tm
    mod_idx = lambda i: (layer, i // tiles_per_batch, 0, 0)
    row = lambda i: (i, 0)
    const = lambda i: (0, 0)
    specs = [pl.BlockSpec((tm, d), row), pl.BlockSpec((tm, d), row),
             pl.BlockSpec((None, None, 6, d), mod_idx), pl.BlockSpec(wo.shape, const)]
    specs += _tail_common_specs(d, tm)
    out_shape, out_specs = _tail_outs(n, d, tm)
    return pl.pallas_call(
        _attn_tail_kernel, out_shape=out_shape, grid=(n // tm,), in_specs=specs, out_specs=out_specs,
        compiler_params=_cparams(1), name=name,
    )(o, x, mods, wo, *_tail_common_args(tailp, counts, tm))


def _window_plan(rows):
    n_blocks = rows // Q_ROWS
    plan = []
    for blk in (0, 1, n_blocks - 1):
        slab = min(max(Q_ROWS * blk - WIN_H // 2, 0), rows - SLAB_ROWS)
        per_row = []
        for rl in range(Q_ROWS):
            r = Q_ROWS * blk + rl
            r0 = min(max(r - WIN_H // 2, 0), rows - WIN_H)
            pairs = []
            for kp in range(SLAB_ROWS // 2):
                kr = slab + 2 * kp
                inside = (r0 <= kr < r0 + WIN_H, r0 <= kr + 1 < r0 + WIN_H)
                pairs.append((min(max(kr - r + WIN_H, 0), 2 * WIN_H - 1), inside))
            per_row.append(pairs)
        plan.append(per_row)
    return plan


def _natten_kernel(q_ref, k_ref, v_ref, kc_ref, vc_ref, c2_ref, o_ref, bias, *, rows):
    hw = 2 * HEAD_DIM
    nq = Q_ROWS * GRID_W
    n_slab = SLAB_ROWS * GRID_W
    n_blocks = rows // Q_ROWS
    lane = lax.broadcasted_iota(jnp.int32, (1, hw), 1)
    low = lane < HEAD_DIM

    @pl.when(pl.program_id(1) == 0)
    def _():
        for var, per_row in enumerate(_window_plan(rows)):
            for h in range(2):
                for rl, pairs in enumerate(per_row):
                    for kp, (entry, inside) in enumerate(pairs):
                        if inside[0] or inside[1]:
                            blk = c2_ref[h, entry]
                            if not inside[0]:
                                blk = jnp.where(low, MASK_BIAS, blk)
                            if not inside[1]:
                                blk = jnp.where(low, blk, MASK_BIAS)
                        else:
                            blk = jnp.full((GRID_W, hw), MASK_BIAS, F32)
                        bias[var, h * nq + rl * GRID_W:h * nq + (rl + 1) * GRID_W, kp * hw:(kp + 1) * hw] = blk

    kc = kc_ref[...]
    vc = vc_ref[...]

    def body(blk, carry):
        slab = jnp.clip(Q_ROWS * blk - WIN_H // 2, 0, rows - SLAB_ROWS)
        var = jnp.where(blk == 0, 0, jnp.where(blk == n_blocks - 1, 2, 1))
        q0 = pl.multiple_of(blk * nq, nq)
        k0 = pl.multiple_of(slab * GRID_W, GRID_W)
        q2 = q_ref[pl.ds(q0, nq), :]
        zero = jnp.zeros_like(q2)
        qm = jnp.concatenate([jnp.where(low, q2, zero), jnp.where(low, zero, q2)], axis=0)
        kw = k_ref[pl.ds(k0, n_slab), :]
        vw = v_ref[pl.ds(k0, n_slab), :]
        s_loc = lax.dot_general(qm, kw, _NT, preferred_element_type=F32) + bias[var]
        s_ctx = lax.dot_general(qm, kc, _NT, preferred_element_type=F32)
        m = jnp.maximum(jnp.max(s_loc, axis=-1, keepdims=True), jnp.max(s_ctx, axis=-1, keepdims=True))
        p_loc = jnp.exp(s_loc - m)
        p_ctx = jnp.exp(s_ctx - m)
        den = jnp.sum(p_loc, axis=-1, keepdims=True) + jnp.sum(p_ctx, axis=-1, keepdims=True)
        acc = jnp.dot(p_loc.astype(BF16), vw, preferred_element_type=F32)
        acc += jnp.dot(p_ctx.astype(BF16), vc, preferred_element_type=F32)
        acc = acc / den
        o_ref[pl.ds(q0, nq), :] = jnp.where(low, acc[:nq], acc[nq:]).astype(BF16)
        return carry

    lax.fori_loop(0, n_blocks, body, 0, unroll=8)


def _natten(q, k, v, kc, vc, c2):
    b, s, d = q.shape
    l = kc.shape[1]
    rows = s // GRID_W
    hw = 2 * HEAD_DIM
    lat = pl.BlockSpec((None, s, hw), lambda hp, bi: (bi, 0, hp))
    cspec = pl.BlockSpec((None, l, hw), lambda hp, bi: (bi, 0, hp))
    return pl.pallas_call(
        functools.partial(_natten_kernel, rows=rows),
        out_shape=jax.ShapeDtypeStruct((b, s, d), BF16),
        grid=(d // hw, b),
        in_specs=[lat, lat, lat, cspec, cspec,
                  pl.BlockSpec((2, 2 * WIN_H, GRID_W, hw), lambda hp, bi: (hp, 0, 0, 0))],
        out_specs=lat,
        scratch_shapes=[pltpu.VMEM((3, 2 * Q_ROWS * GRID_W, SLAB_ROWS * GRID_W), F32)],
        compiler_params=_cparams(2), name="natten",
    )(q, k, v, kc, vc, c2)


def _bias_kernel(r_ref, oh_ref, mask_ref, o_ref):
    r = r_ref[...]
    r1 = r.astype(BF16)
    r2 = (r - r1.astype(F32)).astype(BF16)
    r3 = (r - r1.astype(F32) - r2.astype(F32)).astype(BF16)
    oh = oh_ref[...]
    acc = jnp.dot(r1, oh, preferred_element_type=F32)
    acc += jnp.dot(r2, oh, preferred_element_type=F32)
    acc += jnp.dot(r3, oh, preferred_element_type=F32)
    o_ref[...] = acc + mask_ref[...]


def _bias_pairs(rpb):
    n_h, n_dr, n_dc = rpb.shape
    col = jnp.arange(GRID_W)[:, None]
    kcol = jnp.arange(GRID_W)[None, :]
    cs = jnp.clip(col - WIN_W // 2, 0, GRID_W - WIN_W)
    valid = (kcol >= cs) & (kcol < cs + WIN_W)
    dc = jnp.arange(32)[:, None, None]
    onehot = ((kcol - col + WIN_W - 1)[None] == dc) & valid[None]
    onehot = onehot.reshape(32, GRID_W * GRID_W).astype(BF16)
    maskadd = jnp.where(valid, 0.0, MASK_BIAS).reshape(1, GRID_W * GRID_W).astype(F32)
    rp = jnp.zeros((n_h, 16, 32), F32).at[:, :n_dr, :n_dc].set(rpb)
    flat = pl.pallas_call(
        _bias_kernel,
        out_shape=jax.ShapeDtypeStruct((n_h, 16, GRID_W * GRID_W), F32),
        grid=(n_h,),
        in_specs=[pl.BlockSpec((None, 16, 32), lambda h: (h, 0, 0)),
                  pl.BlockSpec((32, GRID_W * GRID_W), lambda h: (0, 0)),
                  pl.BlockSpec((1, GRID_W * GRID_W), lambda h: (0, 0))],
        out_specs=pl.BlockSpec((None, 16, GRID_W * GRID_W), lambda h: (h, 0, 0)),
        compiler_params=_cparams(1), name="bias_expand",
    )(rp, onehot, maskadd)
    tab = flat[:, :n_dr].reshape(n_h, n_dr, GRID_W, GRID_W)
    masked = jnp.full((n_h, 1, GRID_W, GRID_W), MASK_BIAS, F32)
    ext = jnp.concatenate([masked, tab, masked], axis=1)
    return jnp.concatenate([ext[:, :-1], ext[:, 1:]], axis=-1)


def _moe_kernel(ea_ref, eb_ref, nt_ref, xs_ref, wga_ref, wua_ref, wda_ref, wgb_ref, wub_ref, wdb_ref,
                y_ref, cga, cua, cda, cgb, cub, cdb, *, d):
    i = pl.program_id(0)
    prev = jnp.maximum(i - 1, 0)
    fresh = (i == 0) | (ea_ref[i] != ea_ref[prev]) | (eb_ref[i] != eb_ref[prev])

    @pl.when(fresh)
    def _():
        for src, dst in ((wga_ref, cga), (wua_ref, cua), (wda_ref, cda),
                         (wgb_ref, cgb), (wub_ref, cub), (wdb_ref, cdb)):
            dst[...] = src[...].astype(BF16)

    @pl.when(i < nt_ref[0])
    def _():
        xt = xs_ref[:, 0:d].astype(BF16)

        def expert(wg_ref, wu_ref, wd_ref):
            g = jnp.dot(xt, wg_ref[...], preferred_element_type=F32)
            u = jnp.dot(xt, wu_ref[...], preferred_element_type=F32)
            h = (g * _sigmoid(g)) * u
            return jnp.dot(h.astype(BF16), wd_ref[...], preferred_element_type=F32)

        ya = expert(cga, cua, cda)
        yb = expert(cgb, cub, cdb)
        y_ref[...] = xs_ref[:, d + 1:d + 2] * ya + xs_ref[:, d + 2:d + 3] * yb

    @pl.when(i >= nt_ref[0])
    def _():
        y_ref[...] = jnp.zeros_like(y_ref)


def _moe(ea, eb, nt, xs, layer, wg, wu, wd):
    npad, wide = xs.shape
    d = wide - 128
    de = wg.shape[3]
    sel_a = lambda i, ea, eb, nt: (layer, ea[i], 0, 0)
    sel_b = lambda i, ea, eb, nt: (layer, eb[i], 0, 0)
    up = (None, None, d, de)
    down = (None, None, de, d)
    grid_spec = pltpu.PrefetchScalarGridSpec(
        num_scalar_prefetch=3, grid=(npad // TM_MOE,),
        in_specs=[pl.BlockSpec((TM_MOE, wide), lambda i, ea, eb, nt: (i, 0)),
                  pl.BlockSpec(up, sel_a), pl.BlockSpec(up, sel_a), pl.BlockSpec(down, sel_a),
                  pl.BlockSpec(up, sel_b), pl.BlockSpec(up, sel_b), pl.BlockSpec(down, sel_b)],
        out_specs=pl.BlockSpec((TM_MOE, d), lambda i, ea, eb, nt: (i, 0)),
        scratch_shapes=[pltpu.VMEM((d, de), BF16), pltpu.VMEM((d, de), BF16), pltpu.VMEM((de, d), BF16),
                        pltpu.VMEM((d, de), BF16), pltpu.VMEM((d, de), BF16), pltpu.VMEM((de, d), BF16)])
    return pl.pallas_call(
        functools.partial(_moe_kernel, d=d), out_shape=jax.ShapeDtypeStruct((npad, d), F32),
        grid_spec=grid_spec, compiler_params=_cparams(1), name="moe",
    )(ea, eb, nt, xs, wg, wu, wd, wg, wu, wd)


def _plan_kernel(route_ref, cnt_ref, eab_ref, pos_ref, meta_ref):
    cnt = cnt_ref[...]
    tiles = jnp.floor((cnt + float(TM_MOE - 1)) * (1.0 / TM_MOE))
    r = lax.broadcasted_iota(jnp.int32, (CNT_ROWS, CNT_ROWS), 0)
    c = lax.broadcasted_iota(jnp.int32, (CNT_ROWS, CNT_ROWS), 1)
    before = jnp.where(c < r, 1.0, 0.0).astype(BF16)
    first = jnp.dot(before, tiles.astype(BF16), preferred_element_type=F32)
    first_c, tiles_c = first[:, 0:1], tiles[:, 0:1]
    nt = jnp.sum(tiles_c, axis=0, keepdims=True)
    n = route_ref.shape[1]
    bid = lax.broadcasted_iota(jnp.int32, (CNT_ROWS, n), 0).astype(F32)
    base = jnp.sum(jnp.where(bid == route_ref[0:1, :], first_c * float(TM_MOE), 0.0), axis=0, keepdims=True)
    pos_ref[...] = (base + route_ref[3:4, :]).astype(jnp.int32)
    ti = jnp.minimum(lax.broadcasted_iota(jnp.int32, (CNT_ROWS, META_LANES), 1).astype(F32), nt - 1.0)
    mine = (ti >= first_c) & (ti < first_c + tiles_c)
    ea = jnp.sum(jnp.where(mine, eab_ref[:, 0:1], 0.0), axis=0, keepdims=True)
    eb = jnp.sum(jnp.where(mine, eab_ref[:, 1:2], 0.0), axis=0, keepdims=True)
    rows = [ea, eb, jnp.broadcast_to(nt, (1, META_LANES)), jnp.zeros((5, META_LANES), F32)]
    meta_ref[...] = jnp.concatenate(rows, axis=0).astype(jnp.int32)


def _plan(route, counts):
    n = route.shape[1]
    chunk = max(ck for ck in (2048, 1024, 512) if n % ck == 0)
    eab = jnp.zeros((CNT_ROWS, 128), F32)
    eab = eab.at[:N_BUCKETS, 0].set(jnp.asarray(
        [GROUP_SIZE * (bk // N_PAIRS) + PAIR_LO[bk % N_PAIRS] for bk in range(N_BUCKETS)], F32))
    eab = eab.at[:N_BUCKETS, 1].set(jnp.asarray(
        [GROUP_SIZE * (bk // N_PAIRS) + PAIR_HI[bk % N_PAIRS] for bk in range(N_BUCKETS)], F32))
    const = lambda i: (0, 0)
    return pl.pallas_call(
        _plan_kernel,
        out_shape=[jax.ShapeDtypeStruct((1, n), jnp.int32), jax.ShapeDtypeStruct((8, META_LANES), jnp.int32)],
        grid=(n // chunk,),
        in_specs=[pl.BlockSpec((8, chunk), lambda i: (0, i)), pl.BlockSpec((CNT_ROWS, 128), const),
                  pl.BlockSpec((CNT_ROWS, 128), const)],
        out_specs=[pl.BlockSpec((1, chunk), lambda i: (0, i)), pl.BlockSpec((8, META_LANES), const)],
        compiler_params=_cparams(1), name="plan",
    )(route, counts, eab)


def _sort_kernel(pos_ref, tok_ref, route_ref, xs_in_ref, xs_ref, buf, sem, *, tm, d, n_steps):
    del xs_in_ref
    i = pl.program_id(0)
    slot = i % 2

    def wait_rows(s):
        pltpu.make_async_copy(buf.at[s], xs_ref.at[pl.ds(0, tm)], sem.at[s]).wait()

    @pl.when(i >= 2)
    def _():
        wait_rows(slot)

    cols = jnp.concatenate([route_ref[...], jnp.zeros((128 - 8, tm), F32)], axis=0).T
    buf[slot, :, 0:d] = tok_ref[...]
    buf[slot, :, d:] = cols

    for t in range(tm):
        p = pos_ref[i * tm + t]
        pltpu.make_async_copy(buf.at[slot, pl.ds(t, 1)], xs_ref.at[pl.ds(p, 1)], sem.at[slot]).start()

    @pl.when(i == n_steps - 1)
    def _():
        wait_rows(slot)
        if n_steps >= 2:
            wait_rows(1 - slot)


def _sort_rows(pos, tok, route, xs):
    n, d = tok.shape
    tm = TM_LAT
    n_steps = n // tm
    grid_spec = pltpu.PrefetchScalarGridSpec(
        num_scalar_prefetch=1, grid=(n_steps,),
        in_specs=[pl.BlockSpec((tm, d), lambda i, pos: (i, 0)), pl.BlockSpec((8, tm), lambda i, pos: (0, i)),
                  pl.BlockSpec(memory_space=pl.ANY)],
        out_specs=pl.BlockSpec(memory_space=pl.ANY),
        scratch_shapes=[pltpu.VMEM((2, tm, d + 128), F32), pltpu.SemaphoreType.DMA((2,))])
    return pl.pallas_call(
        functools.partial(_sort_kernel, tm=tm, d=d, n_steps=n_steps),
        out_shape=jax.ShapeDtypeStruct(xs.shape, F32), grid_spec=grid_spec,
        input_output_aliases={3: 0}, compiler_params=_cparams(1), name="sort_rows",
    )(pos, tok, route, xs)


def _moe_layer(toks, routes, counts, layer, wg, wu, wd):
    d = toks[0].shape[1]
    n = sum(t.shape[0] for t in toks)
    n_tiles_max = n // TM_MOE + N_BUCKETS
    assert n_tiles_max <= META_LANES
    route = routes[0] if len(routes) == 1 else jnp.concatenate(routes, axis=1)
    pos, meta = _plan(route, counts)
    pos = pos.reshape(n)
    xs = jnp.zeros((n_tiles_max * TM_MOE, d + 128), F32)
    start = 0
    for tok, rt in zip(toks, routes):
        xs = _sort_rows(pos[start:start + tok.shape[0]], tok, rt, xs)
        start += tok.shape[0]
    ys = _moe(meta[0, :n_tiles_max], meta[1, :n_tiles_max], meta[2, :1], xs, layer, wg, wu, wd)
    return jnp.concatenate([pos, jnp.zeros((TM_LAT,), jnp.int32)]), ys


def _final_kernel(pos_ref, x_ref, ys_ref, mod_ref, gpost_ref, o_ref, fbuf, sem, *, tm):
    f = _gathered_tile(pos_ref, ys_ref, fbuf, sem, 0, tm)
    o_ref[...] = x_ref[...] + mod_ref[5:6, :] * _rms(f, gpost_ref[...])


def _final(x, pos, ys, mods, layer, gpost, *, tm, rows_per_batch):
    n, d = x.shape
    tiles_per_batch = rows_per_batch // tm
    row = lambda i, *_: (i, 0)
    grid_spec = pltpu.PrefetchScalarGridSpec(
        num_scalar_prefetch=1, grid=(n // tm,),
        in_specs=[pl.BlockSpec((tm, d), row), pl.BlockSpec(memory_space=pl.ANY),
                  pl.BlockSpec((None, None, 6, d), lambda i, *_: (layer, i // tiles_per_batch, 0, 0)),
                  pl.BlockSpec((1, d), lambda i, *_: (0, 0))],
        out_specs=pl.BlockSpec((tm, d), row),
        scratch_shapes=[pltpu.VMEM((2, tm, d), F32), pltpu.SemaphoreType.DMA((2,))])
    return pl.pallas_call(
        functools.partial(_final_kernel, tm=tm), out_shape=jax.ShapeDtypeStruct((n, d), F32),
        grid_spec=grid_spec, compiler_params=_cparams(1), name="final",
    )(pos, x, ys, mods, gpost)


def kernel(x, c, ctx, c_ctx, w_ada, b_ada, g_mix_pre, g_mix_post, g_ffn_pre, g_ffn_post, conv_w_in, conv_b_in, conv_w_dw, conv_b_dw, conv_ln_g, conv_ln_b, conv_w_out, conv_b_out, attn_w_qkv, attn_w_o, attn_rpb, router_w, router_b, exp_w_gate, exp_w_up, exp_w_down):
    b, s, d = x.shape
    l = ctx.shape[1]
    n, nc = b * s, b * l
    depth = w_ada.shape[0]
    rows = s // GRID_W
    assert depth == 2 and b <= 8 and s % TM_LAT == 0 and l == TM_CTX
    assert rows % Q_ROWS == 0 and rows >= SLAB_ROWS + Q_ROWS
    tm_lat = TM_LAT

    cc = jnp.zeros((MOD_ROWS, d), F32).at[:b].set(c).at[8].set(c_ctx)
    mods = _ada(cc, w_ada, b_ada).reshape(depth, MOD_ROWS, 6, d)

    row = lambda v: v.reshape(1, d)
    rwt = router_w.T
    rwh = rwt.astype(BF16)
    rwl = (rwt - rwh.astype(F32)).astype(BF16)
    rb = router_b.reshape(N_EXPERTS, 1)
    no_counts = jnp.zeros((CNT_ROWS, 128), F32)

    xl = x.reshape(n, d)
    xc = ctx.reshape(nc, d)

    w_in = conv_w_in[0].astype(BF16)
    b_in = conv_b_in[0].reshape(1, 2 * d)
    conv = (conv_w_dw[0], row(conv_b_dw[0]), row(conv_ln_g[0]), row(conv_ln_b[0]),
            conv_w_out[0].astype(BF16), row(conv_b_out[0]))
    tailp = (row(g_mix_post[0]), row(g_ffn_pre[0]), rwh, rwl, rb)
    (u,) = _proj(xl, mods, 0, row(g_mix_pre[0]), w_in, tm=tm_lat, rows_per_batch=s, ctx=False,
                 bias=b_in, name="conv_in")
    (uc,) = _proj(xc, mods, 0, row(g_mix_pre[0]), w_in, tm=TM_CTX, rows_per_batch=l, ctx=True,
                  bias=b_in, name="conv_in_ctx")
    x1, tok, route, counts = _conv_tail(u, xl, mods, 0, conv, tailp, no_counts, tm=tm_lat, rows_per_batch=s,
                                        ctx=False, name="conv_tail")
    xc1, tokc, routec, counts = _conv_tail(uc, xc, mods, 0, conv, tailp, counts, tm=TM_CTX, rows_per_batch=l,
                                           ctx=True, name="conv_tail_ctx")
    pos0, ys0 = _moe_layer([tok, tokc], [route, routec], counts, 0, exp_w_gate, exp_w_up, exp_w_down)

    w_qkv = attn_w_qkv[0].astype(BF16)
    x2, q, k, v = _proj(x1, mods, 1, row(g_mix_pre[1]), w_qkv, tm=tm_lat, rows_per_batch=s, ctx=False,
                        resid=(pos0, ys0, row(g_ffn_post[0]), 0), n_out=3, name="qkv")
    _, kc, vc = _proj(xc1, mods, 1, row(g_mix_pre[1]), w_qkv[:, d:], tm=TM_CTX, rows_per_batch=l, ctx=True,
                      resid=(pos0, ys0, row(g_ffn_post[0]), n), n_out=2, name="kv_ctx")
    c2 = _bias_pairs(attn_rpb[0])
    o = _natten(q.reshape(b, s, d), k.reshape(b, s, d), v.reshape(b, s, d),
                kc.reshape(b, l, d), vc.reshape(b, l, d), c2)
    tailp1 = (row(g_mix_post[1]), row(g_ffn_pre[1]), rwh, rwl, rb)
    x3, tok1, route1, counts1 = _attn_tail(o.reshape(n, d), x2, mods, 1, attn_w_o[0].astype(BF16), tailp1,
                                           no_counts, tm=tm_lat, rows_per_batch=s, name="attn_tail")
    pos1, ys1 = _moe_layer([tok1], [route1], counts1, 1, exp_w_gate, exp_w_up, exp_w_down)
    out = _final(x3, pos1, ys1, mods, 1, row(g_ffn_post[1]), tm=tm_lat, rows_per_batch=s)
    return out.reshape(b, s, d)
```

```python
import functools

import jax
import jax.numpy as jnp
from jax import lax
from jax.experimental import pallas as pl
from jax.experimental.pallas import tpu as pltpu

F32 = jnp.float32
BF16 = jnp.bfloat16

EPS = 1e-6
GRID_W = 64
WIN_H = 8
WIN_W = 16
N_HEADS = 16
HEAD_DIM = 64
N_EXPERTS = 16
N_GROUPS = 4
GROUP_SIZE = N_EXPERTS // N_GROUPS
N_PAIRS = 6
N_BUCKETS = N_GROUPS * N_PAIRS
PAIR_LO = (0, 0, 0, 1, 1, 2)
PAIR_HI = (1, 2, 3, 2, 3, 3)
CONV_WIDTH = 31
CONV_PAD = (CONV_WIDTH - 1) // 2
HALO = 16
MASK_BIAS = -1e30
Q_ROWS = 4
SLAB_ROWS = Q_ROWS + WIN_H
MOD_ROWS = 16
CNT_ROWS = 32
META_LANES = 256
N_DMA_QUEUES = 2

TM_LAT = 512
TM_CTX = 256
TM_MOE = 256
VMEM_LIMIT = 56 * 1024 * 1024

_NT = (((1,), (1,)), ((), ()))


def _cparams(n_axes):
    return pltpu.CompilerParams(dimension_semantics=("arbitrary",) * n_axes,
                                vmem_limit_bytes=VMEM_LIMIT)


def _sigmoid(v):
    return 1.0 / (1.0 + jnp.exp(-v))


def _rms(v, g):
    return v * lax.rsqrt(jnp.mean(v * v, axis=-1, keepdims=True) + EPS) * g


def _split_bf16(v):
    hi = v.astype(BF16)
    lo = (v - hi.astype(F32)).astype(BF16)
    return hi, lo


def _ada_kernel(c_ref, w_ref, b_ref, o_ref):
    a = c_ref[...]
    a = a * _sigmoid(a)
    a_hi, a_lo = _split_bf16(a)
    w_hi, w_lo = _split_bf16(w_ref[...])
    acc = jnp.dot(a_hi, w_hi, preferred_element_type=F32)
    acc += jnp.dot(a_lo, w_hi, preferred_element_type=F32)
    acc += jnp.dot(a_hi, w_lo, preferred_element_type=F32)
    o_ref[...] = acc + b_ref[...]


def _ada(cc, w_ada, b_ada):
    depth, d, n = w_ada.shape
    bn = 1024
    return pl.pallas_call(
        _ada_kernel,
        out_shape=jax.ShapeDtypeStruct((depth, MOD_ROWS, n), F32),
        grid=(depth, n // bn),
        in_specs=[pl.BlockSpec((MOD_ROWS, d), lambda l, j: (0, 0)),
                  pl.BlockSpec((None, d, bn), lambda l, j: (l, 0, j)),
                  pl.BlockSpec((None, 1, bn), lambda l, j: (l, 0, j))],
        out_specs=pl.BlockSpec((None, MOD_ROWS, bn), lambda l, j: (l, 0, j)),
        compiler_params=_cparams(2),
        name="ada",
    )(cc, w_ada, b_ada.reshape(depth, 1, n))


def _start_row_gather(pos_ref, src_ref, buf, sem, first_row, slot, tm):
    for t in range(tm):
        p = pos_ref[first_row + t]
        pltpu.async_copy(src_ref.at[pl.ds(p, 1)], buf.at[slot, pl.ds(t, 1)], sem.at[slot],
                         priority=t % N_DMA_QUEUES)


def _wait_row_gather(src_ref, buf, sem, slot, tm):
    pltpu.make_async_copy(src_ref.at[pl.ds(0, tm)], buf.at[slot], sem.at[slot]).wait()


def _gathered_tile(pos_ref, src_ref, buf, sem, row0, tm):
    i = pl.program_id(0)
    slot = i % 2

    @pl.when(i == 0)
    def _():
        _start_row_gather(pos_ref, src_ref, buf, sem, row0, 0, tm)

    _wait_row_gather(src_ref, buf, sem, slot, tm)
    _start_row_gather(pos_ref, src_ref, buf, sem, row0 + (i + 1) * tm, 1 - slot, tm)

    @pl.when(i == pl.num_programs(0) - 1)
    def _():
        _wait_row_gather(src_ref, buf, sem, 1 - slot, tm)

    return buf[slot]


def _proj_kernel(*refs, fuse_resid, glu, d, tm, f_row0):
    refs = list(refs)
    if fuse_resid:
        pos_ref = refs.pop(0)
    x_ref = refs.pop(0)
    x = x_ref[...]
    if fuse_resid:
        ys_ref, modp_ref, gpost_ref = refs.pop(0), refs.pop(0), refs.pop(0)
    mod_ref, gpre_ref, w_ref = refs.pop(0), refs.pop(0), refs.pop(0)
    if glu:
        b_ref = refs.pop(0)
    if fuse_resid:
        xn_ref = refs.pop(0)
        fbuf, sem = refs.pop(-2), refs.pop(-1)
        f = _gathered_tile(pos_ref, ys_ref, fbuf, sem, f_row0, tm)
        x = x + modp_ref[5:6, :] * _rms(f, gpost_ref[...])
        xn_ref[...] = x
    h = _rms(x, gpre_ref[...]) * (1.0 + mod_ref[1:2, :]) + mod_ref[0:1, :]
    z = jnp.dot(h.astype(BF16), w_ref[...], preferred_element_type=F32)
    if glu:
        z = z + b_ref[...]
        refs[0][...] = (z[:, :d] * _sigmoid(z[:, d:])).astype(BF16)
    else:
        n_out = len(refs)
        for j, o_ref in enumerate(refs):
            part = z[:, j * d:(j + 1) * d]
            if n_out == 3 and j == 0:
                part = part * (HEAD_DIM ** -0.5)
            o_ref[...] = part.astype(BF16)


def _proj(x, mods, layer, gpre, w, *, tm, rows_per_batch, ctx, bias=None, resid=None, n_out=1, name):
    n, d = x.shape
    tiles_per_batch = rows_per_batch // tm
    if ctx:
        mod_idx = lambda i, *_: (layer, 8, 0, 0)
    else:
        mod_idx = lambda i, *_: (layer, i // tiles_per_batch, 0, 0)
    row = lambda i, *_: (i, 0)
    const = lambda i, *_: (0, 0)
    args, specs, scratch, f_row0 = [x], [pl.BlockSpec((tm, d), row)], [], 0
    if resid is not None:
        pos, ys, gpost, f_row0 = resid
        if ctx:
            modp_idx = lambda i, *_: (layer - 1, 8, 0, 0)
        else:
            modp_idx = lambda i, *_: (layer - 1, i // tiles_per_batch, 0, 0)
        args = [pos] + args + [ys, mods, gpost]
        specs += [pl.BlockSpec(memory_space=pl.ANY), pl.BlockSpec((None, None, 6, d), modp_idx),
                  pl.BlockSpec((1, d), const)]
        scratch = [pltpu.VMEM((2, tm, d), F32), pltpu.SemaphoreType.DMA((2,))]
    args += [mods, gpre, w]
    specs += [pl.BlockSpec((None, None, 6, d), mod_idx), pl.BlockSpec((1, d), const),
              pl.BlockSpec(w.shape, const)]
    glu = bias is not None
    if glu:
        args.append(bias)
        specs.append(pl.BlockSpec(bias.shape, const))
    out_shape, out_specs = [], []
    if resid is not None:
        out_shape.append(jax.ShapeDtypeStruct((n, d), F32))
        out_specs.append(pl.BlockSpec((tm, d), row))
    for _ in range(n_out):
        out_shape.append(jax.ShapeDtypeStruct((n, d), BF16))
        out_specs.append(pl.BlockSpec((tm, d), row))
    grid_spec = pltpu.PrefetchScalarGridSpec(
        num_scalar_prefetch=0 if resid is None else 1, grid=(n // tm,),
        in_specs=specs, out_specs=out_specs, scratch_shapes=scratch)
    return pl.pallas_call(
        functools.partial(_proj_kernel, fuse_resid=resid is not None, glu=glu, d=d, tm=tm, f_row0=f_row0),
        out_shape=out_shape, grid_spec=grid_spec, compiler_params=_cparams(1), name=name,
    )(*args)


def _route(lg, rb):
    s = _sigmoid(lg)
    ssel = s + rb
    sel = [ssel[e:e + 1, :] for e in range(N_EXPERTS)]
    raw = [s[e:e + 1, :] for e in range(N_EXPERTS)]
    scores = []
    for g in range(N_GROUPS):
        v0, v1, v2, v3 = sel[4 * g:4 * g + 4]
        a, b = jnp.maximum(v0, v1), jnp.minimum(v0, v1)
        c, e = jnp.maximum(v2, v3), jnp.minimum(v2, v3)
        scores.append(jnp.maximum(a, c) + jnp.maximum(jnp.minimum(a, c), jnp.maximum(b, e)))
    best = scores[0]
    gi = jnp.zeros_like(best)
    for g in range(1, N_GROUPS):
        better = scores[g] > best
        gi = jnp.where(better, float(g), gi)
        best = jnp.where(better, scores[g], best)

    def pick(rows, j):
        return jnp.where(gi == 0.0, rows[j],
                         jnp.where(gi == 1.0, rows[4 + j], jnp.where(gi == 2.0, rows[8 + j], rows[12 + j])))

    v = [pick(sel, j) for j in range(GROUP_SIZE)]
    r = [pick(raw, j) for j in range(GROUP_SIZE)]
    m = []
    for j in range(GROUP_SIZE):
        ahead = jnp.zeros_like(best)
        for k in range(GROUP_SIZE):
            if k == j:
                continue
            beats = (v[k] >= v[j]) if k < j else (v[k] > v[j])
            ahead = ahead + jnp.where(beats, 1.0, 0.0)
        m.append(ahead < 2.0)
    pair = jnp.where(m[0], jnp.where(m[1], 0.0, jnp.where(m[2], 1.0, 2.0)),
                     jnp.where(m[1], jnp.where(m[2], 3.0, 4.0), 5.0))
    sa = jnp.where(m[0], r[0], jnp.where(m[1], r[1], r[2]))
    sb = jnp.where(m[3], r[3], jnp.where(m[2], r[2], r[1]))
    den = sa + sb
    bucket = gi * float(N_PAIRS) + pair
    zeros = jnp.zeros((5, lg.shape[1]), F32)
    return jnp.concatenate([bucket, sa / den, sb / den, zeros], axis=0)


def _tail(y, x, mod_ref, gpost_ref, gffn_ref, rwh_ref, rwl_ref, rb_ref, tri_ref, cin_ref,
          x1_ref, tok_ref, route_ref, cnt_ref):
    x1 = x + mod_ref[2:3, :] * _rms(y, gpost_ref[...])
    x1_ref[...] = x1
    t = _rms(x1, gffn_ref[...]) * (1.0 + mod_ref[4:5, :]) + mod_ref[3:4, :]
    tok_ref[...] = t
    t_hi, t_lo = _split_bf16(t)
    lg = lax.dot_general(rwh_ref[...], t_hi, _NT, preferred_element_type=F32)
    lg += lax.dot_general(rwl_ref[...], t_hi, _NT, preferred_element_type=F32)
    lg += lax.dot_general(rwh_ref[...], t_lo, _NT, preferred_element_type=F32)
    rt = _route(lg, rb_ref[...])
    tm = rt.shape[1]

    @pl.when(pl.program_id(0) == 0)
    def _():
        cnt_ref[...] = cin_ref[...]

    bid = lax.broadcasted_iota(jnp.int32, (CNT_ROWS, tm), 0).astype(F32)
    onehot = jnp.where(bid == rt[0:1, :], 1.0, 0.0)
    cum = jnp.dot(onehot.astype(BF16), tri_ref[...], preferred_element_type=F32)
    rank = jnp.sum(onehot * (cum + cnt_ref[:, 0:1]), axis=0, keepdims=True) - 1.0
    cnt_ref[...] = cnt_ref[...] + cum[:, tm - 1:tm]
    route_ref[...] = jnp.concatenate([rt[0:3, :], rank, jnp.zeros((4, tm), F32)], axis=0)


def _conv_tail_kernel(u_ref, up_ref, un_ref, x_ref, mod_ref, wdw_ref, bdw_ref, lng_ref, lnb_ref,
                      wout_ref, bout_ref, gpost_ref, gffn_ref, rwh_ref, rwl_ref, rb_ref, tri_ref, cin_ref,
                      x1_ref, tok_ref, route_ref, cnt_ref, ubuf, ush, cacc, wb, *, tm, tiles_per_seq, chunk):
    t = pl.program_id(0) % tiles_per_seq
    prev_ok = jnp.where(t > 0, 1.0, 0.0)
    next_ok = jnp.where(t < tiles_per_seq - 1, 1.0, 0.0)
    ubuf[0:HALO, :] = up_ref[...].astype(F32) * prev_ok
    ubuf[HALO:HALO + tm, :] = u_ref[...].astype(F32)
    ubuf[HALO + tm:, :] = un_ref[...].astype(F32) * next_ok

    base = HALO - CONV_PAD
    span = tm + 8 * ((CONV_WIDTH - 1 + base) // 8)
    for s in range(1, 8):
        ush[s - 1, 0:span, :] = ubuf[s:s + span, :]

    @pl.when(pl.program_id(0) == 0)
    def _():
        for k in range(CONV_WIDTH):
            wb[k] = jnp.broadcast_to(wdw_ref[k:k + 1, :], (8, wb.shape[2]))

    n_lane_blocks = wb.shape[2] // 128

    def body(c, carry):
        r0 = pl.multiple_of(c * chunk, chunk)
        for lb in range(n_lane_blocks):
            lanes = slice(lb * 128, (lb + 1) * 128)
            acc = None
            for s in range(8):
                if s == 0:
                    win = ubuf[pl.ds(r0, chunk + span - tm), lanes]
                else:
                    win = ush[s - 1, pl.ds(r0, chunk + span - tm), lanes]
                win = win.reshape((chunk + span - tm) // 8, 8, 128)
                for q in range((span - tm) // 8 + 1):
                    k = 8 * q + s - base
                    if 0 <= k < CONV_WIDTH:
                        term = win[q:q + chunk // 8] * wb[k, :, lanes][None]
                        acc = term if acc is None else acc + term
            cacc[pl.ds(r0, chunk), lanes] = acc.reshape(chunk, 128)
        return carry

    lax.fori_loop(0, tm // chunk, body, 0)

    y = cacc[...] + bdw_ref[...]
    mu = jnp.mean(y, axis=-1, keepdims=True)
    yc = y - mu
    var = jnp.mean(yc * yc, axis=-1, keepdims=True)
    yn = yc * lax.rsqrt(var + EPS) * lng_ref[...] + lnb_ref[...]
    act = yn * _sigmoid(yn)
    z = jnp.dot(act.astype(BF16), wout_ref[...], preferred_element_type=F32) + bout_ref[...]
    _tail(z, x_ref[...], mod_ref, gpost_ref, gffn_ref, rwh_ref, rwl_ref, rb_ref, tri_ref, cin_ref,
          x1_ref, tok_ref, route_ref, cnt_ref)


def _tail_common_specs(d, tm):
    const = lambda i: (0, 0)
    return [pl.BlockSpec((1, d), const), pl.BlockSpec((1, d), const),
            pl.BlockSpec((N_EXPERTS, d), const), pl.BlockSpec((N_EXPERTS, d), const),
            pl.BlockSpec((N_EXPERTS, 1), const), pl.BlockSpec((tm, tm), const),
            pl.BlockSpec((CNT_ROWS, 128), const)]


def _tail_common_args(tailp, counts, tm):
    gpost, gffn, rwh, rwl, rb = tailp
    j = lax.broadcasted_iota(jnp.int32, (tm, tm), 0)
    t = lax.broadcasted_iota(jnp.int32, (tm, tm), 1)
    return [gpost, gffn, rwh, rwl, rb, (j <= t).astype(BF16), counts]


def _tail_outs(n, d, tm):
    out_shape = [jax.ShapeDtypeStruct((n, d), F32), jax.ShapeDtypeStruct((n, d), F32),
                 jax.ShapeDtypeStruct((8, n), F32), jax.ShapeDtypeStruct((CNT_ROWS, 128), F32)]
    out_specs = [pl.BlockSpec((tm, d), lambda i: (i, 0)), pl.BlockSpec((tm, d), lambda i: (i, 0)),
                 pl.BlockSpec((8, tm), lambda i: (0, i)), pl.BlockSpec((CNT_ROWS, 128), lambda i: (0, 0))]
    return out_shape, out_specs


def _conv_tail(u, x, mods, layer, conv, tailp, counts, *, tm, rows_per_batch, ctx, name):
    n, d = x.shape
    wdw, bdw, lng, lnb, wout, bout = conv
    tiles_per_seq = rows_per_batch // tm
    hb = tm // HALO
    n_hb = n // HALO
    if ctx:
        mod_idx = lambda i: (layer, 8, 0, 0)
    else:
        mod_idx = lambda i: (layer, i // tiles_per_seq, 0, 0)
    row = lambda i: (i, 0)
    const = lambda i: (0, 0)
    specs = [pl.BlockSpec((tm, d), row),
             pl.BlockSpec((HALO, d), lambda i: (jnp.maximum(i * hb - 1, 0), 0)),
             pl.BlockSpec((HALO, d), lambda i: (jnp.minimum((i + 1) * hb, n_hb - 1), 0)),
             pl.BlockSpec((tm, d), row),
             pl.BlockSpec((None, None, 6, d), mod_idx),
             pl.BlockSpec(wdw.shape, const), pl.BlockSpec((1, d), const),
             pl.BlockSpec((1, d), const), pl.BlockSpec((1, d), const),
             pl.BlockSpec(wout.shape, const), pl.BlockSpec((1, d), const)]
    specs += _tail_common_specs(d, tm)
    out_shape, out_specs = _tail_outs(n, d, tm)
    chunk = 64
    return pl.pallas_call(
        functools.partial(_conv_tail_kernel, tm=tm, tiles_per_seq=tiles_per_seq, chunk=chunk),
        out_shape=out_shape, grid=(n // tm,), in_specs=specs, out_specs=out_specs,
        scratch_shapes=[pltpu.VMEM((tm + 2 * HALO, d), F32), pltpu.VMEM((7, tm + 2 * HALO, d), F32),
                        pltpu.VMEM((tm, d), F32), pltpu.VMEM((CONV_WIDTH, 8, d), F32)],
        compiler_params=_cparams(1), name=name,
    )(u, u, u, x, mods, wdw, bdw, lng, lnb, wout, bout, *_tail_common_args(tailp, counts, tm))


def _attn_tail_kernel(o_ref, x_ref, mod_ref, wo_ref, gpost_ref, gffn_ref, rwh_ref, rwl_ref, rb_ref, tri_ref,
                      cin_ref, x1_ref, tok_ref, route_ref, cnt_ref):
    y = jnp.dot(o_ref[...], wo_ref[...], preferred_element_type=F32)
    _tail(y, x_ref[...], mod_ref, gpost_ref, gffn_ref, rwh_ref, rwl_ref, rb_ref, tri_ref, cin_ref,
          x1_ref, tok_ref, route_ref, cnt_ref)


def _attn_tail(o, x, mods, layer, wo, tailp, counts, *, tm, rows_per_batch, name):
    n, d = x.shape
    tiles_per_batch = rows_per_batch // tm
    mod_idx = lambda i: (layer, i // tiles_per_batch, 0, 0)
    row = lambda i: (i, 0)
    const = lambda i: (0, 0)
    specs = [pl.BlockSpec((tm, d), row), pl.BlockSpec((tm, d), row),
             pl.BlockSpec((None, None, 6, d), mod_idx), pl.BlockSpec(wo.shape, const)]
    specs += _tail_common_specs(d, tm)
    out_shape, out_specs = _tail_outs(n, d, tm)
    return pl.pallas_call(
        _attn_tail_kernel, out_shape=out_shape, grid=(n // tm,), in_specs=specs, out_specs=out_specs,
        compiler_params=_cparams(1), name=name,
    )(o, x, mods, wo, *_tail_common_args(tailp, counts, tm))


def _window_plan(rows):
    n_blocks = rows // Q_ROWS
    plan = []
    for blk in (0, 1, n_blocks - 1):
        slab = min(max(Q_ROWS * blk - WIN_H // 2, 0), rows - SLAB_ROWS)
        per_row = []
        for rl in range(Q_ROWS):
            r = Q_ROWS * blk + rl
            r0 = min(max(r - WIN_H // 2, 0), rows - WIN_H)
            pairs = []
            for kp in range(SLAB_ROWS // 2):
                kr = slab + 2 * kp
                inside = (r0 <= kr < r0 + WIN_H, r0 <= kr + 1 < r0 + WIN_H)
                pairs.append((min(max(kr - r + WIN_H, 0), 2 * WIN_H - 1), inside))
            per_row.append(pairs)
        plan.append(per_row)
    return plan


def _natten_kernel(q_ref, k_ref, v_ref, kc_ref, vc_ref, c2_ref, o_ref, bias, *, rows):
    hw = 2 * HEAD_DIM
    nq = Q_ROWS * GRID_W
    n_slab = SLAB_ROWS * GRID_W
    n_blocks = rows // Q_ROWS
    lane = lax.broadcasted_iota(jnp.int32, (1, hw), 1)
    low = lane < HEAD_DIM

    @pl.when(pl.program_id(1) == 0)
    def _():
        for var, per_row in enumerate(_window_plan(rows)):
            for h in range(2):
                for rl, pairs in enumerate(per_row):
                    for kp, (entry, inside) in enumerate(pairs):
                        if inside[0] or inside[1]:
                            blk = c2_ref[h, entry]
                            if not inside[0]:
                                blk = jnp.where(low, MASK_BIAS, blk)
                            if not inside[1]:
                                blk = jnp.where(low, blk, MASK_BIAS)
                        else:
                            blk = jnp.full((GRID_W, hw), MASK_BIAS, F32)
                        bias[var, h * nq + rl * GRID_W:h * nq + (rl + 1) * GRID_W, kp * hw:(kp + 1) * hw] = blk

    kc = kc_ref[...]
    vc = vc_ref[...]

    def body(blk, carry):
        slab = jnp.clip(Q_ROWS * blk - WIN_H // 2, 0, rows - SLAB_ROWS)
        var = jnp.where(blk == 0, 0, jnp.where(blk == n_blocks - 1, 2, 1))
        q0 = pl.multiple_of(blk * nq, nq)
        k0 = pl.multiple_of(slab * GRID_W, GRID_W)
        q2 = q_ref[pl.ds(q0, nq), :]
        zero = jnp.zeros_like(q2)
        qm = jnp.concatenate([jnp.where(low, q2, zero), jnp.where(low, zero, q2)], axis=0)
        kw = k_ref[pl.ds(k0, n_slab), :]
        vw = v_ref[pl.ds(k0, n_slab), :]
        s_loc = lax.dot_general(qm, kw, _NT, preferred_element_type=F32) + bias[var]
        s_ctx = lax.dot_general(qm, kc, _NT, preferred_element_type=F32)
        m = jnp.maximum(jnp.max(s_loc, axis=-1, keepdims=True), jnp.max(s_ctx, axis=-1, keepdims=True))
        p_loc = jnp.exp(s_loc - m)
        p_ctx = jnp.exp(s_ctx - m)
        den = jnp.sum(p_loc, axis=-1, keepdims=True) + jnp.sum(p_ctx, axis=-1, keepdims=True)
        acc = jnp.dot(p_loc.astype(BF16), vw, preferred_element_type=F32)
        acc += jnp.dot(p_ctx.astype(BF16), vc, preferred_element_type=F32)
        acc = acc / den
        o_ref[pl.ds(q0, nq), :] = jnp.where(low, acc[:nq], acc[nq:]).astype(BF16)
        return carry

    lax.fori_loop(0, n_blocks, body, 0, unroll=8)


def _natten(q, k, v, kc, vc, c2):
    b, s, d = q.shape
    l = kc.shape[1]
    rows = s // GRID_W
    hw = 2 * HEAD_DIM
    lat = pl.BlockSpec((None, s, hw), lambda hp, bi: (bi, 0, hp))
    cspec = pl.BlockSpec((None, l, hw), lambda hp, bi: (bi, 0, hp))
    return pl.pallas_call(
        functools.partial(_natten_kernel, rows=rows),
        out_shape=jax.ShapeDtypeStruct((b, s, d), BF16),
        grid=(d // hw, b),
        in_specs=[lat, lat, lat, cspec, cspec,
                  pl.BlockSpec((2, 2 * WIN_H, GRID_W, hw), lambda hp, bi: (hp, 0, 0, 0))],
        out_specs=lat,
        scratch_shapes=[pltpu.VMEM((3, 2 * Q_ROWS * GRID_W, SLAB_ROWS * GRID_W), F32)],
        compiler_params=_cparams(2), name="natten",
    )(q, k, v, kc, vc, c2)


def _bias_kernel(r_ref, oh_ref, mask_ref, o_ref):
    r = r_ref[...]
    r1 = r.astype(BF16)
    r2 = (r - r1.astype(F32)).astype(BF16)
    r3 = (r - r1.astype(F32) - r2.astype(F32)).astype(BF16)
    oh = oh_ref[...]
    acc = jnp.dot(r1, oh, preferred_element_type=F32)
    acc += jnp.dot(r2, oh, preferred_element_type=F32)
    acc += jnp.dot(r3, oh, preferred_element_type=F32)
    o_ref[...] = acc + mask_ref[...]


def _bias_pairs(rpb):
    n_h, n_dr, n_dc = rpb.shape
    col = jnp.arange(GRID_W)[:, None]
    kcol = jnp.arange(GRID_W)[None, :]
    cs = jnp.clip(col - WIN_W // 2, 0, GRID_W - WIN_W)
    valid = (kcol >= cs) & (kcol < cs + WIN_W)
    dc = jnp.arange(32)[:, None, None]
    onehot = ((kcol - col + WIN_W - 1)[None] == dc) & valid[None]
    onehot = onehot.reshape(32, GRID_W * GRID_W).astype(BF16)
    maskadd = jnp.where(valid, 0.0, MASK_BIAS).reshape(1, GRID_W * GRID_W).astype(F32)
    rp = jnp.zeros((n_h, 16, 32), F32).at[:, :n_dr, :n_dc].set(rpb)
    flat = pl.pallas_call(
        _bias_kernel,
        out_shape=jax.ShapeDtypeStruct((n_h, 16, GRID_W * GRID_W), F32),
        grid=(n_h,),
        in_specs=[pl.BlockSpec((None, 16, 32), lambda h: (h, 0, 0)),
                  pl.BlockSpec((32, GRID_W * GRID_W), lambda h: (0, 0)),
                  pl.BlockSpec((1, GRID_W * GRID_W), lambda h: (0, 0))],
        out_specs=pl.BlockSpec((None, 16, GRID_W * GRID_W), lambda h: (h, 0, 0)),
        compiler_params=_cparams(1), name="bias_expand",
    )(rp, onehot, maskadd)
    tab = flat[:, :n_dr].reshape(n_h, n_dr, GRID_W, GRID_W)
    masked = jnp.full((n_h, 1, GRID_W, GRID_W), MASK_BIAS, F32)
    ext = jnp.concatenate([masked, tab, masked], axis=1)
    return jnp.concatenate([ext[:, :-1], ext[:, 1:]], axis=-1)


def _moe_kernel(ea_ref, eb_ref, nt_ref, xs_ref, wga_ref, wua_ref, wda_ref, wgb_ref, wub_ref, wdb_ref,
                y_ref, cga, cua, cda, cgb, cub, cdb, *, d):
    i = pl.program_id(0)
    prev = jnp.maximum(i - 1, 0)
    fresh = (i == 0) | (ea_ref[i] != ea_ref[prev]) | (eb_ref[i] != eb_ref[prev])

    @pl.when(fresh)
    def _():
        for src, dst in ((wga_ref, cga), (wua_ref, cua), (wda_ref, cda),
                         (wgb_ref, cgb), (wub_ref, cub), (wdb_ref, cdb)):
            dst[...] = src[...].astype(BF16)

    @pl.when(i < nt_ref[0])
    def _():
        xt = xs_ref[:, 0:d].astype(BF16)

        def expert(wg_ref, wu_ref, wd_ref):
            g = jnp.dot(xt, wg_ref[...], preferred_element_type=F32)
            u = jnp.dot(xt, wu_ref[...], preferred_element_type=F32)
            h = (g * _sigmoid(g)) * u
            return jnp.dot(h.astype(BF16), wd_ref[...], preferred_element_type=F32)

        ya = expert(cga, cua, cda)
        yb = expert(cgb, cub, cdb)
        y_ref[...] = xs_ref[:, d + 1:d + 2] * ya + xs_ref[:, d + 2:d + 3] * yb

    @pl.when(i >= nt_ref[0])
    def _():
        y_ref[...] = jnp.zeros_like(y_ref)


def _moe(ea, eb, nt, xs, layer, wg, wu, wd):
    npad, wide = xs.shape
    d = wide - 128
    de = wg.shape[3]
    sel_a = lambda i, ea, eb, nt: (layer, ea[i], 0, 0)
    sel_b = lambda i, ea, eb, nt: (layer, eb[i], 0, 0)
    up = (None, None, d, de)
    down = (None, None, de, d)
    grid_spec = pltpu.PrefetchScalarGridSpec(
        num_scalar_prefetch=3, grid=(npad // TM_MOE,),
        in_specs=[pl.BlockSpec((TM_MOE, wide), lambda i, ea, eb, nt: (jnp.minimum(i, nt[0] - 1), 0)),
                  pl.BlockSpec(up, sel_a), pl.BlockSpec(up, sel_a), pl.BlockSpec(down, sel_a),
                  pl.BlockSpec(up, sel_b), pl.BlockSpec(up, sel_b), pl.BlockSpec(down, sel_b)],
        out_specs=pl.BlockSpec((TM_MOE, d), lambda i, ea, eb, nt: (i, 0)),
        scratch_shapes=[pltpu.VMEM((d, de), BF16), pltpu.VMEM((d, de), BF16), pltpu.VMEM((de, d), BF16),
                        pltpu.VMEM((d, de), BF16), pltpu.VMEM((d, de), BF16), pltpu.VMEM((de, d), BF16)])
    return pl.pallas_call(
        functools.partial(_moe_kernel, d=d), out_shape=jax.ShapeDtypeStruct((npad, d), F32),
        grid_spec=grid_spec, compiler_params=_cparams(1), name="moe",
    )(ea, eb, nt, xs, wg, wu, wd, wg, wu, wd)


def _plan_kernel(route_ref, cnt_ref, eab_ref, pos_ref, meta_ref):
    cnt = cnt_ref[...]
    tiles = jnp.floor((cnt + float(TM_MOE - 1)) * (1.0 / TM_MOE))
    r = lax.broadcasted_iota(jnp.int32, (CNT_ROWS, CNT_ROWS), 0)
    c = lax.broadcasted_iota(jnp.int32, (CNT_ROWS, CNT_ROWS), 1)
    before = jnp.where(c < r, 1.0, 0.0).astype(BF16)
    first = jnp.dot(before, tiles.astype(BF16), preferred_element_type=F32)
    first_c, tiles_c = first[:, 0:1], tiles[:, 0:1]
    nt = jnp.sum(tiles_c, axis=0, keepdims=True)
    n = route_ref.shape[1]
    bid = lax.broadcasted_iota(jnp.int32, (CNT_ROWS, n), 0).astype(F32)
    base = jnp.sum(jnp.where(bid == route_ref[0:1, :], first_c * float(TM_MOE), 0.0), axis=0, keepdims=True)
    pos_ref[...] = (base + route_ref[3:4, :]).astype(jnp.int32)
    ti = jnp.minimum(lax.broadcasted_iota(jnp.int32, (CNT_ROWS, META_LANES), 1).astype(F32), nt - 1.0)
    mine = (ti >= first_c) & (ti < first_c + tiles_c)
    ea = jnp.sum(jnp.where(mine, eab_ref[:, 0:1], 0.0), axis=0, keepdims=True)
    eb = jnp.sum(jnp.where(mine, eab_ref[:, 1:2], 0.0), axis=0, keepdims=True)
    diag = (lax.broadcasted_iota(jnp.int32, (CNT_ROWS, META_LANES), 0)
            == lax.broadcasted_iota(jnp.int32, (CNT_ROWS, META_LANES), 1))
    first_l = jnp.sum(jnp.where(diag, first_c, 0.0), axis=0, keepdims=True)
    cnt_l = jnp.sum(jnp.where(diag, cnt[:, 0:1], 0.0), axis=0, keepdims=True)
    rows = [ea, eb, jnp.broadcast_to(nt, (1, META_LANES)), first_l, cnt_l, jnp.zeros((3, META_LANES), F32)]
    meta_ref[...] = jnp.concatenate(rows, axis=0).astype(jnp.int32)


def _plan(route, counts):
    n = route.shape[1]
    chunk = max(ck for ck in (2048, 1024, 512) if n % ck == 0)
    eab = jnp.zeros((CNT_ROWS, 128), F32)
    eab = eab.at[:N_BUCKETS, 0].set(jnp.asarray(
        [GROUP_SIZE * (bk // N_PAIRS) + PAIR_LO[bk % N_PAIRS] for bk in range(N_BUCKETS)], F32))
    eab = eab.at[:N_BUCKETS, 1].set(jnp.asarray(
        [GROUP_SIZE * (bk // N_PAIRS) + PAIR_HI[bk % N_PAIRS] for bk in range(N_BUCKETS)], F32))
    const = lambda i: (0, 0)
    return pl.pallas_call(
        _plan_kernel,
        out_shape=[jax.ShapeDtypeStruct((1, n), jnp.int32), jax.ShapeDtypeStruct((8, META_LANES), jnp.int32)],
        grid=(n // chunk,),
        in_specs=[pl.BlockSpec((8, chunk), lambda i: (0, i)), pl.BlockSpec((CNT_ROWS, 128), const),
                  pl.BlockSpec((CNT_ROWS, 128), const)],
        out_specs=[pl.BlockSpec((1, chunk), lambda i: (0, i)), pl.BlockSpec((8, META_LANES), const)],
        compiler_params=_cparams(1), name="plan",
    )(route, counts, eab)


def _fill_padding(first_ref, cnt_ref, nt_ref, xs_ref, zbuf, sem, n_tiles_max):
    zbuf[...] = jnp.zeros_like(zbuf)

    def fill(first_row, n_rows):
        cp = pltpu.make_async_copy(zbuf.at[pl.ds(0, n_rows)], xs_ref.at[pl.ds(first_row, n_rows)], sem)
        cp.start()
        cp.wait()

    def fill_tile(j, carry):
        fill(pl.multiple_of(j * TM_MOE, TM_MOE), TM_MOE)
        return carry

    lax.fori_loop(nt_ref[0], n_tiles_max, fill_tile, 0)

    for bk in range(N_BUCKETS):
        cnt = cnt_ref[bk]
        start = first_ref[bk] * TM_MOE + cnt
        pad = (-cnt) & (TM_MOE - 1)
        head = jnp.minimum((-cnt) & 7, pad)
        for j in range(7):
            @pl.when(j < head)
            def _(j=j, start=start):
                fill(start + j, 1)

        body = pad - head
        piece = TM_MOE // 2
        while piece >= 8:
            done = body & ~(2 * piece - 1)

            @pl.when((body & piece) != 0)
            def _(piece=piece, done=done, start=start, head=head):
                fill(pl.multiple_of(start + head + done, 8), piece)

            piece //= 2


def _sort_kernel(*refs, tm, d, step_ranges, n_tiles_max):
    n_src = len(step_ranges)
    pos_ref, first_ref, cnt_ref, nt_ref = refs[:4]
    srcs = refs[4:4 + 2 * n_src]
    xs_ref, buf, zbuf, sem, zsem = refs[4 + 2 * n_src:]
    n_steps = step_ranges[-1][1]
    i = pl.program_id(0)
    slot = i % 2

    def wait_rows(s):
        pltpu.make_async_copy(buf.at[s], xs_ref.at[pl.ds(0, tm)], sem.at[s]).wait()

    @pl.when(i == 0)
    def _():
        _fill_padding(first_ref, cnt_ref, nt_ref, xs_ref, zbuf, zsem, n_tiles_max)

    @pl.when(i >= 2)
    def _():
        wait_rows(slot)

    for k, (s0, s1) in enumerate(step_ranges):
        @pl.when((i >= s0) & (i < s1))
        def _(k=k):
            tok_ref, route_ref = srcs[2 * k], srcs[2 * k + 1]
            cols = jnp.concatenate([route_ref[...], jnp.zeros((128 - 8, tm), F32)], axis=0).T
            buf[slot, :, 0:d] = tok_ref[...]
            buf[slot, :, d:] = cols

    for t in range(tm):
        p = pos_ref[i * tm + t]
        pltpu.async_copy(buf.at[slot, pl.ds(t, 1)], xs_ref.at[pl.ds(p, 1)], sem.at[slot],
                         priority=t % N_DMA_QUEUES)

    @pl.when(i == n_steps - 1)
    def _():
        wait_rows(slot)
        if n_steps >= 2:
            wait_rows(1 - slot)


def _sort_rows(pos, first, cnt, nt, toks, routes, n_tiles_max):
    d = toks[0].shape[1]
    tm = TM_LAT
    step_ranges, s0 = [], 0
    for tok in toks:
        step_ranges.append((s0, s0 + tok.shape[0] // tm))
        s0 = step_ranges[-1][1]
    in_specs, args = [], []
    for (b0, b1), tok, rt in zip(step_ranges, toks, routes):
        blk = lambda i, *_, b0=b0, b1=b1: jnp.clip(i - b0, 0, b1 - b0 - 1)
        in_specs += [pl.BlockSpec((tm, d), lambda i, *_, blk=blk: (blk(i), 0)),
                     pl.BlockSpec((8, tm), lambda i, *_, blk=blk: (0, blk(i)))]
        args += [tok, rt]
    grid_spec = pltpu.PrefetchScalarGridSpec(
        num_scalar_prefetch=4, grid=(s0,), in_specs=in_specs,
        out_specs=pl.BlockSpec(memory_space=pl.ANY),
        scratch_shapes=[pltpu.VMEM((2, tm, d + 128), F32), pltpu.VMEM((TM_MOE, d + 128), F32),
                        pltpu.SemaphoreType.DMA((2,)), pltpu.SemaphoreType.DMA])
    return pl.pallas_call(
        functools.partial(_sort_kernel, tm=tm, d=d, step_ranges=tuple(step_ranges), n_tiles_max=n_tiles_max),
        out_shape=jax.ShapeDtypeStruct((n_tiles_max * TM_MOE, d + 128), F32), grid_spec=grid_spec,
        compiler_params=_cparams(1), name="sort_rows",
    )(pos, first, cnt, nt, *args)


def _moe_layer(toks, routes, counts, layer, wg, wu, wd):
    d = toks[0].shape[1]
    n = sum(t.shape[0] for t in toks)
    n_tiles_max = n // TM_MOE + N_BUCKETS
    assert n_tiles_max <= META_LANES
    route = routes[0] if len(routes) == 1 else jnp.concatenate(routes, axis=1)
    pos, meta = _plan(route, counts)
    pos = pos.reshape(n)
    xs = _sort_rows(pos, meta[3, :N_BUCKETS], meta[4, :N_BUCKETS], meta[2, :1], toks, routes, n_tiles_max)
    ys = _moe(meta[0, :n_tiles_max], meta[1, :n_tiles_max], meta[2, :1], xs, layer, wg, wu, wd)
    return jnp.concatenate([pos, jnp.zeros((TM_LAT,), jnp.int32)]), ys


def _final_kernel(pos_ref, x_ref, ys_ref, mod_ref, gpost_ref, o_ref, fbuf, sem, *, tm):
    f = _gathered_tile(pos_ref, ys_ref, fbuf, sem, 0, tm)
    o_ref[...] = x_ref[...] + mod_ref[5:6, :] * _rms(f, gpost_ref[...])


def _final(x, pos, ys, mods, layer, gpost, *, tm, rows_per_batch):
    n, d = x.shape
    tiles_per_batch = rows_per_batch // tm
    row = lambda i, *_: (i, 0)
    grid_spec = pltpu.PrefetchScalarGridSpec(
        num_scalar_prefetch=1, grid=(n // tm,),
        in_specs=[pl.BlockSpec((tm, d), row), pl.BlockSpec(memory_space=pl.ANY),
                  pl.BlockSpec((None, None, 6, d), lambda i, *_: (layer, i // tiles_per_batch, 0, 0)),
                  pl.BlockSpec((1, d), lambda i, *_: (0, 0))],
        out_specs=pl.BlockSpec((tm, d), row),
        scratch_shapes=[pltpu.VMEM((2, tm, d), F32), pltpu.SemaphoreType.DMA((2,))])
    return pl.pallas_call(
        functools.partial(_final_kernel, tm=tm), out_shape=jax.ShapeDtypeStruct((n, d), F32),
        grid_spec=grid_spec, compiler_params=_cparams(1), name="final",
    )(pos, x, ys, mods, gpost)


def kernel(x, c, ctx, c_ctx, w_ada, b_ada, g_mix_pre, g_mix_post, g_ffn_pre, g_ffn_post, conv_w_in, conv_b_in, conv_w_dw, conv_b_dw, conv_ln_g, conv_ln_b, conv_w_out, conv_b_out, attn_w_qkv, attn_w_o, attn_rpb, router_w, router_b, exp_w_gate, exp_w_up, exp_w_down):
    b, s, d = x.shape
    l = ctx.shape[1]
    n, nc = b * s, b * l
    depth = w_ada.shape[0]
    rows = s // GRID_W
    assert depth == 2 and b <= 8 and s % TM_LAT == 0 and l == TM_CTX
    assert rows % Q_ROWS == 0 and rows >= SLAB_ROWS + Q_ROWS
    tm_lat = TM_LAT

    cc = jnp.zeros((MOD_ROWS, d), F32).at[:b].set(c).at[8].set(c_ctx)
    mods = _ada(cc, w_ada, b_ada).reshape(depth, MOD_ROWS, 6, d)

    row = lambda v: v.reshape(1, d)
    rwt = router_w.T
    rwh = rwt.astype(BF16)
    rwl = (rwt - rwh.astype(F32)).astype(BF16)
    rb = router_b.reshape(N_EXPERTS, 1)
    no_counts = jnp.zeros((CNT_ROWS, 128), F32)

    xl = x.reshape(n, d)
    xc = ctx.reshape(nc, d)

    w_in = conv_w_in[0].astype(BF16)
    b_in = conv_b_in[0].reshape(1, 2 * d)
    conv = (conv_w_dw[0], row(conv_b_dw[0]), row(conv_ln_g[0]), row(conv_ln_b[0]),
            conv_w_out[0].astype(BF16), row(conv_b_out[0]))
    tailp = (row(g_mix_post[0]), row(g_ffn_pre[0]), rwh, rwl, rb)
    (u,) = _proj(xl, mods, 0, row(g_mix_pre[0]), w_in, tm=tm_lat, rows_per_batch=s, ctx=False,
                 bias=b_in, name="conv_in")
    (uc,) = _proj(xc, mods, 0, row(g_mix_pre[0]), w_in, tm=TM_CTX, rows_per_batch=l, ctx=True,
                  bias=b_in, name="conv_in_ctx")
    x1, tok, route, counts = _conv_tail(u, xl, mods, 0, conv, tailp, no_counts, tm=tm_lat, rows_per_batch=s,
                                        ctx=False, name="conv_tail")
    xc1, tokc, routec, counts = _conv_tail(uc, xc, mods, 0, conv, tailp, counts, tm=TM_CTX, rows_per_batch=l,
                                           ctx=True, name="conv_tail_ctx")
    pos0, ys0 = _moe_layer([tok, tokc], [route, routec], counts, 0, exp_w_gate, exp_w_up, exp_w_down)

    w_qkv = attn_w_qkv[0].astype(BF16)
    x2, q, k, v = _proj(x1, mods, 1, row(g_mix_pre[1]), w_qkv, tm=tm_lat, rows_per_batch=s, ctx=False,
                        resid=(pos0, ys0, row(g_ffn_post[0]), 0), n_out=3, name="qkv")
    _, kc, vc = _proj(xc1, mods, 1, row(g_mix_pre[1]), w_qkv[:, d:], tm=TM_CTX, rows_per_batch=l, ctx=True,
                      resid=(pos0, ys0, row(g_ffn_post[0]), n), n_out=2, name="kv_ctx")
    c2 = _bias_pairs(attn_rpb[0])
    o = _natten(q.reshape(b, s, d), k.reshape(b, s, d), v.reshape(b, s, d),
                kc.reshape(b, l, d), vc.reshape(b, l, d), c2)
    tailp1 = (row(g_mix_post[1]), row(g_ffn_pre[1]), rwh, rwl, rb)
    x3, tok1, route1, counts1 = _attn_tail(o.reshape(n, d), x2, mods, 1, attn_w_o[0].astype(BF16), tailp1,
                                           no_counts, tm=tm_lat, rows_per_batch=s, name="attn_tail")
    pos1, ys1 = _moe_layer([tok1], [route1], counts1, 1, exp_w_gate, exp_w_up, exp_w_down)
    out = _final(x3, pos1, ys1, mods, 1, row(g_ffn_post[1]), tm=tm_lat, rows_per_batch=s)
    return out.reshape(b, s, d)
```

```python
import functools

import jax
import jax.numpy as jnp
from jax import lax
from jax.experimental import pallas as pl
from jax.experimental.pallas import tpu as pltpu

F32 = jnp.float32
BF16 = jnp.bfloat16

EPS = 1e-6
GRID_W = 64
WIN_H = 8
WIN_W = 16
N_HEADS = 16
HEAD_DIM = 64
N_EXPERTS = 16
N_GROUPS = 4
GROUP_SIZE = N_EXPERTS // N_GROUPS
N_PAIRS = 6
N_BUCKETS = N_GROUPS * N_PAIRS
PAIR_LO = (0, 0, 0, 1, 1, 2)
PAIR_HI = (1, 2, 3, 2, 3, 3)
CONV_WIDTH = 31
CONV_PAD = (CONV_WIDTH - 1) // 2
HALO = 16
MASK_BIAS = -1e30
Q_ROWS = 4
SLAB_ROWS = Q_ROWS + WIN_H
MOD_ROWS = 16
CNT_ROWS = 32
META_LANES = 256
N_DMA_QUEUES = 1

TM_LAT = 512
TM_CTX = 256
TM_MOE = 256
VMEM_LIMIT = 56 * 1024 * 1024

_NT = (((1,), (1,)), ((), ()))


def _cparams(n_axes):
    return pltpu.CompilerParams(dimension_semantics=("arbitrary",) * n_axes,
                                vmem_limit_bytes=VMEM_LIMIT)


def _sigmoid(v):
    return 1.0 / (1.0 + jnp.exp(-v))


def _rms(v, g):
    return v * lax.rsqrt(jnp.mean(v * v, axis=-1, keepdims=True) + EPS) * g


def _split_bf16(v):
    hi = v.astype(BF16)
    lo = (v - hi.astype(F32)).astype(BF16)
    return hi, lo


def _ada_kernel(c_ref, w_ref, b_ref, o_ref):
    a = c_ref[...]
    a = a * _sigmoid(a)
    a_hi, a_lo = _split_bf16(a)
    w_hi, w_lo = _split_bf16(w_ref[...])
    acc = jnp.dot(a_hi, w_hi, preferred_element_type=F32)
    acc += jnp.dot(a_lo, w_hi, preferred_element_type=F32)
    acc += jnp.dot(a_hi, w_lo, preferred_element_type=F32)
    o_ref[...] = acc + b_ref[...]


def _ada(cc, w_ada, b_ada):
    depth, d, n = w_ada.shape
    bn = 1024
    return pl.pallas_call(
        _ada_kernel,
        out_shape=jax.ShapeDtypeStruct((depth, MOD_ROWS, n), F32),
        grid=(depth, n // bn),
        in_specs=[pl.BlockSpec((MOD_ROWS, d), lambda l, j: (0, 0)),
                  pl.BlockSpec((None, d, bn), lambda l, j: (l, 0, j)),
                  pl.BlockSpec((None, 1, bn), lambda l, j: (l, 0, j))],
        out_specs=pl.BlockSpec((None, MOD_ROWS, bn), lambda l, j: (l, 0, j)),
        compiler_params=_cparams(2),
        name="ada",
    )(cc, w_ada, b_ada.reshape(depth, 1, n))


def _start_row_gather(pos_ref, src_ref, buf, sem, first_row, slot, tm):
    for t in range(tm):
        p = pos_ref[first_row + t]
        pltpu.async_copy(src_ref.at[pl.ds(p, 1)], buf.at[slot, pl.ds(t, 1)], sem.at[slot],
                         priority=t % N_DMA_QUEUES)


def _wait_row_gather(src_ref, buf, sem, slot, tm):
    pltpu.make_async_copy(src_ref.at[pl.ds(0, tm)], buf.at[slot], sem.at[slot]).wait()


def _with_gathered_tile(pos_ref, src_ref, buf, sem, row0, tm, consume):
    i = pl.program_id(0)

    @pl.when(i == 0)
    def _():
        _start_row_gather(pos_ref, src_ref, buf, sem, row0, 0, tm)

    for slot in range(2):
        @pl.when(i % 2 == slot)
        def _(slot=slot):
            _wait_row_gather(src_ref, buf, sem, slot, tm)
            _start_row_gather(pos_ref, src_ref, buf, sem, row0 + (i + 1) * tm, 1 - slot, tm)
            consume(buf[slot])

            @pl.when(i == pl.num_programs(0) - 1)
            def _():
                _wait_row_gather(src_ref, buf, sem, 1 - slot, tm)


def _proj_kernel(*refs, fuse_resid, glu, d, tm, f_row0):
    refs = list(refs)
    if fuse_resid:
        pos_ref = refs.pop(0)
    x_ref = refs.pop(0)
    if fuse_resid:
        ys_ref, modp_ref, gpost_ref = refs.pop(0), refs.pop(0), refs.pop(0)
    mod_ref, gpre_ref, w_ref = refs.pop(0), refs.pop(0), refs.pop(0)
    if glu:
        b_ref = refs.pop(0)
    if fuse_resid:
        xn_ref = refs.pop(0)
        fbuf, sem = refs.pop(-2), refs.pop(-1)

    def project(f):
        x = x_ref[...]
        if fuse_resid:
            x = x + modp_ref[5:6, :] * _rms(f, gpost_ref[...])
            xn_ref[...] = x
        h = _rms(x, gpre_ref[...]) * (1.0 + mod_ref[1:2, :]) + mod_ref[0:1, :]
        z = jnp.dot(h.astype(BF16), w_ref[...], preferred_element_type=F32)
        if glu:
            z = z + b_ref[...]
            refs[0][...] = (z[:, :d] * _sigmoid(z[:, d:])).astype(BF16)
        else:
            n_out = len(refs)
            for j, o_ref in enumerate(refs):
                part = z[:, j * d:(j + 1) * d]
                if n_out == 3 and j == 0:
                    part = part * (HEAD_DIM ** -0.5)
                o_ref[...] = part.astype(BF16)

    if fuse_resid:
        _with_gathered_tile(pos_ref, ys_ref, fbuf, sem, f_row0, tm, project)
    else:
        project(None)


def _proj(x, mods, layer, gpre, w, *, tm, rows_per_batch, ctx, bias=None, resid=None, n_out=1, name):
    n, d = x.shape
    tiles_per_batch = rows_per_batch // tm
    if ctx:
        mod_idx = lambda i, *_: (layer, 8, 0, 0)
    else:
        mod_idx = lambda i, *_: (layer, i // tiles_per_batch, 0, 0)
    row = lambda i, *_: (i, 0)
    const = lambda i, *_: (0, 0)
    args, specs, scratch, f_row0 = [x], [pl.BlockSpec((tm, d), row)], [], 0
    if resid is not None:
        pos, ys, gpost, f_row0 = resid
        if ctx:
            modp_idx = lambda i, *_: (layer - 1, 8, 0, 0)
        else:
            modp_idx = lambda i, *_: (layer - 1, i // tiles_per_batch, 0, 0)
        args = [pos] + args + [ys, mods, gpost]
        specs += [pl.BlockSpec(memory_space=pl.ANY), pl.BlockSpec((None, None, 6, d), modp_idx),
                  pl.BlockSpec((1, d), const)]
        scratch = [pltpu.VMEM((2, tm, d), F32), pltpu.SemaphoreType.DMA((2,))]
    args += [mods, gpre, w]
    specs += [pl.BlockSpec((None, None, 6, d), mod_idx), pl.BlockSpec((1, d), const),
              pl.BlockSpec(w.shape, const)]
    glu = bias is not None
    if glu:
        args.append(bias)
        specs.append(pl.BlockSpec(bias.shape, const))
    out_shape, out_specs = [], []
    if resid is not None:
        out_shape.append(jax.ShapeDtypeStruct((n, d), F32))
        out_specs.append(pl.BlockSpec((tm, d), row))
    for _ in range(n_out):
        out_shape.append(jax.ShapeDtypeStruct((n, d), BF16))
        out_specs.append(pl.BlockSpec((tm, d), row))
    grid_spec = pltpu.PrefetchScalarGridSpec(
        num_scalar_prefetch=0 if resid is None else 1, grid=(n // tm,),
        in_specs=specs, out_specs=out_specs, scratch_shapes=scratch)
    return pl.pallas_call(
        functools.partial(_proj_kernel, fuse_resid=resid is not None, glu=glu, d=d, tm=tm, f_row0=f_row0),
        out_shape=out_shape, grid_spec=grid_spec, compiler_params=_cparams(1), name=name,
    )(*args)


def _route(lg, rb):
    s = _sigmoid(lg)
    ssel = s + rb
    sel = [ssel[e:e + 1, :] for e in range(N_EXPERTS)]
    raw = [s[e:e + 1, :] for e in range(N_EXPERTS)]
    scores = []
    for g in range(N_GROUPS):
        v0, v1, v2, v3 = sel[4 * g:4 * g + 4]
        a, b = jnp.maximum(v0, v1), jnp.minimum(v0, v1)
        c, e = jnp.maximum(v2, v3), jnp.minimum(v2, v3)
        scores.append(jnp.maximum(a, c) + jnp.maximum(jnp.minimum(a, c), jnp.maximum(b, e)))
    best = scores[0]
    gi = jnp.zeros_like(best)
    for g in range(1, N_GROUPS):
        better = scores[g] > best
        gi = jnp.where(better, float(g), gi)
        best = jnp.where(better, scores[g], best)

    def pick(rows, j):
        return jnp.where(gi == 0.0, rows[j],
                         jnp.where(gi == 1.0, rows[4 + j], jnp.where(gi == 2.0, rows[8 + j], rows[12 + j])))

    v = [pick(sel, j) for j in range(GROUP_SIZE)]
    r = [pick(raw, j) for j in range(GROUP_SIZE)]
    m = []
    for j in range(GROUP_SIZE):
        ahead = jnp.zeros_like(best)
        for k in range(GROUP_SIZE):
            if k == j:
                continue
            beats = (v[k] >= v[j]) if k < j else (v[k] > v[j])
            ahead = ahead + jnp.where(beats, 1.0, 0.0)
        m.append(ahead < 2.0)
    pair = jnp.where(m[0], jnp.where(m[1], 0.0, jnp.where(m[2], 1.0, 2.0)),
                     jnp.where(m[1], jnp.where(m[2], 3.0, 4.0), 5.0))
    sa = jnp.where(m[0], r[0], jnp.where(m[1], r[1], r[2]))
    sb = jnp.where(m[3], r[3], jnp.where(m[2], r[2], r[1]))
    den = sa + sb
    bucket = gi * float(N_PAIRS) + pair
    zeros = jnp.zeros((5, lg.shape[1]), F32)
    return jnp.concatenate([bucket, sa / den, sb / den, zeros], axis=0)


def _tail(y, x, mod_ref, gpost_ref, gffn_ref, rwh_ref, rwl_ref, rb_ref, tri_ref, cin_ref,
          x1_ref, tok_ref, route_ref, cnt_ref):
    x1 = x + mod_ref[2:3, :] * _rms(y, gpost_ref[...])
    x1_ref[...] = x1
    t = _rms(x1, gffn_ref[...]) * (1.0 + mod_ref[4:5, :]) + mod_ref[3:4, :]
    tok_ref[...] = t
    t_hi, t_lo = _split_bf16(t)
    lg = lax.dot_general(rwh_ref[...], t_hi, _NT, preferred_element_type=F32)
    lg += lax.dot_general(rwl_ref[...], t_hi, _NT, preferred_element_type=F32)
    lg += lax.dot_general(rwh_ref[...], t_lo, _NT, preferred_element_type=F32)
    rt = _route(lg, rb_ref[...])
    tm = rt.shape[1]

    @pl.when(pl.program_id(0) == 0)
    def _():
        cnt_ref[...] = cin_ref[...]

    bid = lax.broadcasted_iota(jnp.int32, (CNT_ROWS, tm), 0).astype(F32)
    onehot = jnp.where(bid == rt[0:1, :], 1.0, 0.0)
    cum = jnp.dot(onehot.astype(BF16), tri_ref[...], preferred_element_type=F32)
    rank = jnp.sum(onehot * (cum + cnt_ref[:, 0:1]), axis=0, keepdims=True) - 1.0
    cnt_ref[...] = cnt_ref[...] + cum[:, tm - 1:tm]
    route_ref[...] = jnp.concatenate([rt[0:3, :], rank, jnp.zeros((4, tm), F32)], axis=0)


def _conv_tail_kernel(u_ref, up_ref, un_ref, x_ref, mod_ref, wdw_ref, bdw_ref, lng_ref, lnb_ref,
                      wout_ref, bout_ref, gpost_ref, gffn_ref, rwh_ref, rwl_ref, rb_ref, tri_ref, cin_ref,
                      x1_ref, tok_ref, route_ref, cnt_ref, ubuf, ush, cacc, wb, *, tm, tiles_per_seq, chunk):
    t = pl.program_id(0) % tiles_per_seq
    prev_ok = jnp.where(t > 0, 1.0, 0.0)
    next_ok = jnp.where(t < tiles_per_seq - 1, 1.0, 0.0)
    ubuf[0:HALO, :] = up_ref[...].astype(F32) * prev_ok
    ubuf[HALO:HALO + tm, :] = u_ref[...].astype(F32)
    ubuf[HALO + tm:, :] = un_ref[...].astype(F32) * next_ok

    base = HALO - CONV_PAD
    span = tm + 8 * ((CONV_WIDTH - 1 + base) // 8)
    for s in range(1, 8):
        ush[s - 1, 0:span, :] = ubuf[s:s + span, :]

    @pl.when(pl.program_id(0) == 0)
    def _():
        for k in range(CONV_WIDTH):
            wb[k] = jnp.broadcast_to(wdw_ref[k:k + 1, :], (8, wb.shape[2]))

    n_lane_blocks = wb.shape[2] // 128

    def body(c, carry):
        r0 = pl.multiple_of(c * chunk, chunk)
        for lb in range(n_lane_blocks):
            lanes = slice(lb * 128, (lb + 1) * 128)
            acc = None
            for s in range(8):
                if s == 0:
                    win = ubuf[pl.ds(r0, chunk + span - tm), lanes]
                else:
                    win = ush[s - 1, pl.ds(r0, chunk + span - tm), lanes]
                win = win.reshape((chunk + span - tm) // 8, 8, 128)
                for q in range((span - tm) // 8 + 1):
                    k = 8 * q + s - base
                    if 0 <= k < CONV_WIDTH:
                        term = win[q:q + chunk // 8] * wb[k, :, lanes][None]
                        acc = term if acc is None else acc + term
            cacc[pl.ds(r0, chunk), lanes] = acc.reshape(chunk, 128)
        return carry

    lax.fori_loop(0, tm // chunk, body, 0)

    y = cacc[...] + bdw_ref[...]
    mu = jnp.mean(y, axis=-1, keepdims=True)
    yc = y - mu
    var = jnp.mean(yc * yc, axis=-1, keepdims=True)
    yn = yc * lax.rsqrt(var + EPS) * lng_ref[...] + lnb_ref[...]
    act = yn * _sigmoid(yn)
    z = jnp.dot(act.astype(BF16), wout_ref[...], preferred_element_type=F32) + bout_ref[...]
    _tail(z, x_ref[...], mod_ref, gpost_ref, gffn_ref, rwh_ref, rwl_ref, rb_ref, tri_ref, cin_ref,
          x1_ref, tok_ref, route_ref, cnt_ref)


def _tail_common_specs(d, tm):
    const = lambda i: (0, 0)
    return [pl.BlockSpec((1, d), const), pl.BlockSpec((1, d), const),
            pl.BlockSpec((N_EXPERTS, d), const), pl.BlockSpec((N_EXPERTS, d), const),
            pl.BlockSpec((N_EXPERTS, 1), const), pl.BlockSpec((tm, tm), const),
            pl.BlockSpec((CNT_ROWS, 128), const)]


def _tail_common_args(tailp, counts, tm):
    gpost, gffn, rwh, rwl, rb = tailp
    j = lax.broadcasted_iota(jnp.int32, (tm, tm), 0)
    t = lax.broadcasted_iota(jnp.int32, (tm, tm), 1)
    return [gpost, gffn, rwh, rwl, rb, (j <= t).astype(BF16), counts]


def _tail_outs(n, d, tm):
    out_shape = [jax.ShapeDtypeStruct((n, d), F32), jax.ShapeDtypeStruct((n, d), F32),
                 jax.ShapeDtypeStruct((8, n), F32), jax.ShapeDtypeStruct((CNT_ROWS, 128), F32)]
    out_specs = [pl.BlockSpec((tm, d), lambda i: (i, 0)), pl.BlockSpec((tm, d), lambda i: (i, 0)),
                 pl.BlockSpec((8, tm), lambda i: (0, i)), pl.BlockSpec((CNT_ROWS, 128), lambda i: (0, 0))]
    return out_shape, out_specs


def _conv_tail(u, x, mods, layer, conv, tailp, counts, *, tm, rows_per_batch, ctx, name):
    n, d = x.shape
    wdw, bdw, lng, lnb, wout, bout = conv
    tiles_per_seq = rows_per_batch // tm
    hb = tm // HALO
    n_hb = n // HALO
    if ctx:
        mod_idx = lambda i: (layer, 8, 0, 0)
    else:
        mod_idx = lambda i: (layer, i // tiles_per_seq, 0, 0)
    row = lambda i: (i, 0)
    const = lambda i: (0, 0)
    specs = [pl.BlockSpec((tm, d), row),
             pl.BlockSpec((HALO, d), lambda i: (jnp.maximum(i * hb - 1, 0), 0)),
             pl.BlockSpec((HALO, d), lambda i: (jnp.minimum((i + 1) * hb, n_hb - 1), 0)),
             pl.BlockSpec((tm, d), row),
             pl.BlockSpec((None, None, 6, d), mod_idx),
             pl.BlockSpec(wdw.shape, const), pl.BlockSpec((1, d), const),
             pl.BlockSpec((1, d), const), pl.BlockSpec((1, d), const),
             pl.BlockSpec(wout.shape, const), pl.BlockSpec((1, d), const)]
    specs += _tail_common_specs(d, tm)
    out_shape, out_specs = _tail_outs(n, d, tm)
    chunk = 64
    return pl.pallas_call(
        functools.partial(_conv_tail_kernel, tm=tm, tiles_per_seq=tiles_per_seq, chunk=chunk),
        out_shape=out_shape, grid=(n // tm,), in_specs=specs, out_specs=out_specs,
        scratch_shapes=[pltpu.VMEM((tm + 2 * HALO, d), F32), pltpu.VMEM((7, tm + 2 * HALO, d), F32),
                        pltpu.VMEM((tm, d), F32), pltpu.VMEM((CONV_WIDTH, 8, d), F32)],
        compiler_params=_cparams(1), name=name,
    )(u, u, u, x, mods, wdw, bdw, lng, lnb, wout, bout, *_tail_common_args(tailp, counts, tm))


def _attn_tail_kernel(o_ref, x_ref, mod_ref, wo_ref, gpost_ref, gffn_ref, rwh_ref, rwl_ref, rb_ref, tri_ref,
                      cin_ref, x1_ref, tok_ref, route_ref, cnt_ref):
    y = jnp.dot(o_ref[...], wo_ref[...], preferred_element_type=F32)
    _tail(y, x_ref[...], mod_ref, gpost_ref, gffn_ref, rwh_ref, rwl_ref, rb_ref, tri_ref, cin_ref,
          x1_ref, tok_ref, route_ref, cnt_ref)


def _attn_tail(o, x, mods, layer, wo, tailp, counts, *, tm, rows_per_batch, name):
    n, d = x.shape
    tiles_per_batch = rows_per_batch // tm
    mod_idx = lambda i: (layer, i // tiles_per_batch, 0, 0)
    row = lambda i: (i, 0)
    const = lambda i: (0, 0)
    specs = [pl.BlockSpec((tm, d), row), pl.BlockSpec((tm, d), row),
             pl.BlockSpec((None, None, 6, d), mod_idx), pl.BlockSpec(wo.shape, const)]
    specs += _tail_common_specs(d, tm)
    out_shape, out_specs = _tail_outs(n, d, tm)
    return pl.pallas_call(
        _attn_tail_kernel, out_shape=out_shape, grid=(n // tm,), in_specs=specs, out_specs=out_specs,
        compiler_params=_cparams(1), name=name,
    )(o, x, mods, wo, *_tail_common_args(tailp, counts, tm))


def _window_plan(rows):
    n_blocks = rows // Q_ROWS
    plan = []
    for blk in (0, 1, n_blocks - 1):
        slab = min(max(Q_ROWS * blk - WIN_H // 2, 0), rows - SLAB_ROWS)
        per_row = []
        for rl in range(Q_ROWS):
            r = Q_ROWS * blk + rl
            r0 = min(max(r - WIN_H // 2, 0), rows - WIN_H)
            pairs = []
            for kp in range(SLAB_ROWS // 2):
                kr = slab + 2 * kp
                inside = (r0 <= kr < r0 + WIN_H, r0 <= kr + 1 < r0 + WIN_H)
                pairs.append((min(max(kr - r + WIN_H, 0), 2 * WIN_H - 1), inside))
            per_row.append(pairs)
        plan.append(per_row)
    return plan


def _natten_kernel(q_ref, k_ref, v_ref, kc_ref, vc_ref, c2_ref, o_ref, bias, *, rows):
    hw = 2 * HEAD_DIM
    nq = Q_ROWS * GRID_W
    n_slab = SLAB_ROWS * GRID_W
    n_blocks = rows // Q_ROWS
    lane = lax.broadcasted_iota(jnp.int32, (1, hw), 1)
    low = lane < HEAD_DIM

    @pl.when(pl.program_id(1) == 0)
    def _():
        for var, per_row in enumerate(_window_plan(rows)):
            for h in range(2):
                for rl, pairs in enumerate(per_row):
                    for kp, (entry, inside) in enumerate(pairs):
                        if inside[0] or inside[1]:
                            blk = c2_ref[h, entry]
                            if not inside[0]:
                                blk = jnp.where(low, MASK_BIAS, blk)
                            if not inside[1]:
                                blk = jnp.where(low, blk, MASK_BIAS)
                        else:
                            blk = jnp.full((GRID_W, hw), MASK_BIAS, F32)
                        bias[var, h * nq + rl * GRID_W:h * nq + (rl + 1) * GRID_W, kp * hw:(kp + 1) * hw] = blk

    kc = kc_ref[...]
    vc = vc_ref[...]

    def body(blk, carry):
        slab = jnp.clip(Q_ROWS * blk - WIN_H // 2, 0, rows - SLAB_ROWS)
        var = jnp.where(blk == 0, 0, jnp.where(blk == n_blocks - 1, 2, 1))
        q0 = pl.multiple_of(blk * nq, nq)
        k0 = pl.multiple_of(slab * GRID_W, GRID_W)
        q2 = q_ref[pl.ds(q0, nq), :]
        zero = jnp.zeros_like(q2)
        qm = jnp.concatenate([jnp.where(low, q2, zero), jnp.where(low, zero, q2)], axis=0)
        kw = k_ref[pl.ds(k0, n_slab), :]
        vw = v_ref[pl.ds(k0, n_slab), :]
        s_loc = lax.dot_general(qm, kw, _NT, preferred_element_type=F32) + bias[var]
        s_ctx = lax.dot_general(qm, kc, _NT, preferred_element_type=F32)
        m = jnp.maximum(jnp.max(s_loc, axis=-1, keepdims=True), jnp.max(s_ctx, axis=-1, keepdims=True))
        p_loc = jnp.exp(s_loc - m)
        p_ctx = jnp.exp(s_ctx - m)
        den = jnp.sum(p_loc, axis=-1, keepdims=True) + jnp.sum(p_ctx, axis=-1, keepdims=True)
        acc = jnp.dot(p_loc.astype(BF16), vw, preferred_element_type=F32)
        acc += jnp.dot(p_ctx.astype(BF16), vc, preferred_element_type=F32)
        acc = acc / den
        o_ref[pl.ds(q0, nq), :] = jnp.where(low, acc[:nq], acc[nq:]).astype(BF16)
        return carry

    lax.fori_loop(0, n_blocks, body, 0, unroll=8)


def _natten(q, k, v, kc, vc, c2):
    b, s, d = q.shape
    l = kc.shape[1]
    rows = s // GRID_W
    hw = 2 * HEAD_DIM
    lat = pl.BlockSpec((None, s, hw), lambda hp, bi: (bi, 0, hp))
    cspec = pl.BlockSpec((None, l, hw), lambda hp, bi: (bi, 0, hp))
    return pl.pallas_call(
        functools.partial(_natten_kernel, rows=rows),
        out_shape=jax.ShapeDtypeStruct((b, s, d), BF16),
        grid=(d // hw, b),
        in_specs=[lat, lat, lat, cspec, cspec,
                  pl.BlockSpec((2, 2 * WIN_H, GRID_W, hw), lambda hp, bi: (hp, 0, 0, 0))],
        out_specs=lat,
        scratch_shapes=[pltpu.VMEM((3, 2 * Q_ROWS * GRID_W, SLAB_ROWS * GRID_W), F32)],
        compiler_params=_cparams(2), name="natten",
    )(q, k, v, kc, vc, c2)


def _bias_kernel(r_ref, oh_ref, mask_ref, o_ref):
    r = r_ref[...]
    r1 = r.astype(BF16)
    r2 = (r - r1.astype(F32)).astype(BF16)
    r3 = (r - r1.astype(F32) - r2.astype(F32)).astype(BF16)
    oh = oh_ref[...]
    acc = jnp.dot(r1, oh, preferred_element_type=F32)
    acc += jnp.dot(r2, oh, preferred_element_type=F32)
    acc += jnp.dot(r3, oh, preferred_element_type=F32)
    o_ref[...] = acc + mask_ref[...]


def _bias_pairs(rpb):
    n_h, n_dr, n_dc = rpb.shape
    col = jnp.arange(GRID_W)[:, None]
    kcol = jnp.arange(GRID_W)[None, :]
    cs = jnp.clip(col - WIN_W // 2, 0, GRID_W - WIN_W)
    valid = (kcol >= cs) & (kcol < cs + WIN_W)
    dc = jnp.arange(32)[:, None, None]
    onehot = ((kcol - col + WIN_W - 1)[None] == dc) & valid[None]
    onehot = onehot.reshape(32, GRID_W * GRID_W).astype(BF16)
    maskadd = jnp.where(valid, 0.0, MASK_BIAS).reshape(1, GRID_W * GRID_W).astype(F32)
    rp = jnp.zeros((n_h, 16, 32), F32).at[:, :n_dr, :n_dc].set(rpb)
    flat = pl.pallas_call(
        _bias_kernel,
        out_shape=jax.ShapeDtypeStruct((n_h, 16, GRID_W * GRID_W), F32),
        grid=(n_h,),
        in_specs=[pl.BlockSpec((None, 16, 32), lambda h: (h, 0, 0)),
                  pl.BlockSpec((32, GRID_W * GRID_W), lambda h: (0, 0)),
                  pl.BlockSpec((1, GRID_W * GRID_W), lambda h: (0, 0))],
        out_specs=pl.BlockSpec((None, 16, GRID_W * GRID_W), lambda h: (h, 0, 0)),
        compiler_params=_cparams(1), name="bias_expand",
    )(rp, onehot, maskadd)
    tab = flat[:, :n_dr].reshape(n_h, n_dr, GRID_W, GRID_W)
    masked = jnp.full((n_h, 1, GRID_W, GRID_W), MASK_BIAS, F32)
    ext = jnp.concatenate([masked, tab, masked], axis=1)
    return jnp.concatenate([ext[:, :-1], ext[:, 1:]], axis=-1)


def _moe_kernel(ea_ref, eb_ref, nt_ref, xs_ref, wga_ref, wua_ref, wda_ref, wgb_ref, wub_ref, wdb_ref,
                y_ref, cga, cua, cda, cgb, cub, cdb, *, d):
    i = pl.program_id(0)
    prev = jnp.maximum(i - 1, 0)
    fresh = (i == 0) | (ea_ref[i] != ea_ref[prev]) | (eb_ref[i] != eb_ref[prev])

    @pl.when(fresh)
    def _():
        for src, dst in ((wga_ref, cga), (wua_ref, cua), (wda_ref, cda),
                         (wgb_ref, cgb), (wub_ref, cub), (wdb_ref, cdb)):
            dst[...] = src[...].astype(BF16)

    @pl.when(i < nt_ref[0])
    def _():
        xt = xs_ref[:, 0:d].astype(BF16)

        def expert(wg_ref, wu_ref, wd_ref):
            g = jnp.dot(xt, wg_ref[...], preferred_element_type=F32)
            u = jnp.dot(xt, wu_ref[...], preferred_element_type=F32)
            h = (g * _sigmoid(g)) * u
            return jnp.dot(h.astype(BF16), wd_ref[...], preferred_element_type=F32)

        ya = expert(cga, cua, cda)
        yb = expert(cgb, cub, cdb)
        y_ref[...] = xs_ref[:, d + 1:d + 2] * ya + xs_ref[:, d + 2:d + 3] * yb

    @pl.when(i >= nt_ref[0])
    def _():
        y_ref[...] = jnp.zeros_like(y_ref)


def _moe(ea, eb, nt, xs, layer, wg, wu, wd):
    npad, wide = xs.shape
    d = wide - 128
    de = wg.shape[3]
    sel_a = lambda i, ea, eb, nt: (layer, ea[i], 0, 0)
    sel_b = lambda i, ea, eb, nt: (layer, eb[i], 0, 0)
    up = (None, None, d, de)
    down = (None, None, de, d)
    grid_spec = pltpu.PrefetchScalarGridSpec(
        num_scalar_prefetch=3, grid=(npad // TM_MOE,),
        in_specs=[pl.BlockSpec((TM_MOE, wide), lambda i, ea, eb, nt: (jnp.minimum(i, nt[0] - 1), 0)),
                  pl.BlockSpec(up, sel_a), pl.BlockSpec(up, sel_a), pl.BlockSpec(down, sel_a),
                  pl.BlockSpec(up, sel_b), pl.BlockSpec(up, sel_b), pl.BlockSpec(down, sel_b)],
        out_specs=pl.BlockSpec((TM_MOE, d), lambda i, ea, eb, nt: (i, 0)),
        scratch_shapes=[pltpu.VMEM((d, de), BF16), pltpu.VMEM((d, de), BF16), pltpu.VMEM((de, d), BF16),
                        pltpu.VMEM((d, de), BF16), pltpu.VMEM((d, de), BF16), pltpu.VMEM((de, d), BF16)])
    return pl.pallas_call(
        functools.partial(_moe_kernel, d=d), out_shape=jax.ShapeDtypeStruct((npad, d), F32),
        grid_spec=grid_spec, compiler_params=_cparams(1), name="moe",
    )(ea, eb, nt, xs, wg, wu, wd, wg, wu, wd)


def _plan_kernel(route_ref, cnt_ref, eab_ref, pos_ref, meta_ref):
    cnt = cnt_ref[...]
    tiles = jnp.floor((cnt + float(TM_MOE - 1)) * (1.0 / TM_MOE))
    r = lax.broadcasted_iota(jnp.int32, (CNT_ROWS, CNT_ROWS), 0)
    c = lax.broadcasted_iota(jnp.int32, (CNT_ROWS, CNT_ROWS), 1)
    before = jnp.where(c < r, 1.0, 0.0).astype(BF16)
    first = jnp.dot(before, tiles.astype(BF16), preferred_element_type=F32)
    first_c, tiles_c = first[:, 0:1], tiles[:, 0:1]
    nt = jnp.sum(tiles_c, axis=0, keepdims=True)
    n = route_ref.shape[1]
    bid = lax.broadcasted_iota(jnp.int32, (CNT_ROWS, n), 0).astype(F32)
    base = jnp.sum(jnp.where(bid == route_ref[0:1, :], first_c * float(TM_MOE), 0.0), axis=0, keepdims=True)
    pos_ref[...] = (base + route_ref[3:4, :]).astype(jnp.int32)
    ti = jnp.minimum(lax.broadcasted_iota(jnp.int32, (CNT_ROWS, META_LANES), 1).astype(F32), nt - 1.0)
    mine = (ti >= first_c) & (ti < first_c + tiles_c)
    ea = jnp.sum(jnp.where(mine, eab_ref[:, 0:1], 0.0), axis=0, keepdims=True)
    eb = jnp.sum(jnp.where(mine, eab_ref[:, 1:2], 0.0), axis=0, keepdims=True)
    diag = (lax.broadcasted_iota(jnp.int32, (CNT_ROWS, META_LANES), 0)
            == lax.broadcasted_iota(jnp.int32, (CNT_ROWS, META_LANES), 1))
    first_l = jnp.sum(jnp.where(diag, first_c, 0.0), axis=0, keepdims=True)
    cnt_l = jnp.sum(jnp.where(diag, cnt[:, 0:1], 0.0), axis=0, keepdims=True)
    rows = [ea, eb, jnp.broadcast_to(nt, (1, META_LANES)), first_l, cnt_l, jnp.zeros((3, META_LANES), F32)]
    meta_ref[...] = jnp.concatenate(rows, axis=0).astype(jnp.int32)


def _plan(route, counts):
    n = route.shape[1]
    chunk = max(ck for ck in (2048, 1024, 512) if n % ck == 0)
    eab = jnp.zeros((CNT_ROWS, 128), F32)
    eab = eab.at[:N_BUCKETS, 0].set(jnp.asarray(
        [GROUP_SIZE * (bk // N_PAIRS) + PAIR_LO[bk % N_PAIRS] for bk in range(N_BUCKETS)], F32))
    eab = eab.at[:N_BUCKETS, 1].set(jnp.asarray(
        [GROUP_SIZE * (bk // N_PAIRS) + PAIR_HI[bk % N_PAIRS] for bk in range(N_BUCKETS)], F32))
    const = lambda i: (0, 0)
    return pl.pallas_call(
        _plan_kernel,
        out_shape=[jax.ShapeDtypeStruct((1, n), jnp.int32), jax.ShapeDtypeStruct((8, META_LANES), jnp.int32)],
        grid=(n // chunk,),
        in_specs=[pl.BlockSpec((8, chunk), lambda i: (0, i)), pl.BlockSpec((CNT_ROWS, 128), const),
                  pl.BlockSpec((CNT_ROWS, 128), const)],
        out_specs=[pl.BlockSpec((1, chunk), lambda i: (0, i)), pl.BlockSpec((8, META_LANES), const)],
        compiler_params=_cparams(1), name="plan",
    )(route, counts, eab)


def _fill_padding(first_ref, cnt_ref, nt_ref, xs_ref, zbuf, sem, n_tiles_max):
    zbuf[...] = jnp.zeros_like(zbuf)

    def fill(first_row, n_rows):
        cp = pltpu.make_async_copy(zbuf.at[pl.ds(0, n_rows)], xs_ref.at[pl.ds(first_row, n_rows)], sem)
        cp.start()
        cp.wait()

    def fill_tile(j, carry):
        fill(pl.multiple_of(j * TM_MOE, TM_MOE), TM_MOE)
        return carry

    lax.fori_loop(nt_ref[0], n_tiles_max, fill_tile, 0)

    for bk in range(N_BUCKETS):
        cnt = cnt_ref[bk]
        start = first_ref[bk] * TM_MOE + cnt
        pad = (-cnt) & (TM_MOE - 1)
        head = jnp.minimum((-cnt) & 7, pad)
        for j in range(7):
            @pl.when(j < head)
            def _(j=j, start=start):
                fill(start + j, 1)

        body = pad - head
        piece = TM_MOE // 2
        while piece >= 8:
            done = body & ~(2 * piece - 1)

            @pl.when((body & piece) != 0)
            def _(piece=piece, done=done, start=start, head=head):
                fill(pl.multiple_of(start + head + done, 8), piece)

            piece //= 2


def _sort_kernel(*refs, tm, d, step_ranges, n_tiles_max):
    n_src = len(step_ranges)
    pos_ref, first_ref, cnt_ref, nt_ref = refs[:4]
    srcs = refs[4:4 + 2 * n_src]
    xs_ref, buf, zbuf, sem, zsem = refs[4 + 2 * n_src:]
    n_steps = step_ranges[-1][1]
    i = pl.program_id(0)
    slot = i % 2

    def wait_rows(s):
        pltpu.make_async_copy(buf.at[s], xs_ref.at[pl.ds(0, tm)], sem.at[s]).wait()

    @pl.when(i == 0)
    def _():
        _fill_padding(first_ref, cnt_ref, nt_ref, xs_ref, zbuf, zsem, n_tiles_max)

    @pl.when(i >= 2)
    def _():
        wait_rows(slot)

    for k, (s0, s1) in enumerate(step_ranges):
        @pl.when((i >= s0) & (i < s1))
        def _(k=k):
            tok_ref, route_ref = srcs[2 * k], srcs[2 * k + 1]
            cols = jnp.concatenate([route_ref[...], jnp.zeros((128 - 8, tm), F32)], axis=0).T
            buf[slot, :, 0:d] = tok_ref[...]
            buf[slot, :, d:] = cols

    for t in range(tm):
        p = pos_ref[i * tm + t]
        pltpu.async_copy(buf.at[slot, pl.ds(t, 1)], xs_ref.at[pl.ds(p, 1)], sem.at[slot],
                         priority=t % N_DMA_QUEUES)

    @pl.when(i == n_steps - 1)
    def _():
        wait_rows(slot)
        if n_steps >= 2:
            wait_rows(1 - slot)


def _sort_rows(pos, first, cnt, nt, toks, routes, n_tiles_max):
    d = toks[0].shape[1]
    tm = TM_LAT
    step_ranges, s0 = [], 0
    for tok in toks:
        step_ranges.append((s0, s0 + tok.shape[0] // tm))
        s0 = step_ranges[-1][1]
    in_specs, args = [], []
    for (b0, b1), tok, rt in zip(step_ranges, toks, routes):
        blk = lambda i, *_, b0=b0, b1=b1: jnp.clip(i - b0, 0, b1 - b0 - 1)
        in_specs += [pl.BlockSpec((tm, d), lambda i, *_, blk=blk: (blk(i), 0)),
                     pl.BlockSpec((8, tm), lambda i, *_, blk=blk: (0, blk(i)))]
        args += [tok, rt]
    grid_spec = pltpu.PrefetchScalarGridSpec(
        num_scalar_prefetch=4, grid=(s0,), in_specs=in_specs,
        out_specs=pl.BlockSpec(memory_space=pl.ANY),
        scratch_shapes=[pltpu.VMEM((2, tm, d + 128), F32), pltpu.VMEM((TM_MOE, d + 128), F32),
                        pltpu.SemaphoreType.DMA((2,)), pltpu.SemaphoreType.DMA])
    return pl.pallas_call(
        functools.partial(_sort_kernel, tm=tm, d=d, step_ranges=tuple(step_ranges), n_tiles_max=n_tiles_max),
        out_shape=jax.ShapeDtypeStruct((n_tiles_max * TM_MOE, d + 128), F32), grid_spec=grid_spec,
        compiler_params=_cparams(1), name="sort_rows",
    )(pos, first, cnt, nt, *args)


def _moe_layer(toks, routes, counts, layer, wg, wu, wd):
    d = toks[0].shape[1]
    n = sum(t.shape[0] for t in toks)
    n_tiles_max = n // TM_MOE + N_BUCKETS
    assert n_tiles_max <= META_LANES
    route = routes[0] if len(routes) == 1 else jnp.concatenate(routes, axis=1)
    pos, meta = _plan(route, counts)
    pos = pos.reshape(n)
    xs = _sort_rows(pos, meta[3, :N_BUCKETS], meta[4, :N_BUCKETS], meta[2, :1], toks, routes, n_tiles_max)
    ys = _moe(meta[0, :n_tiles_max], meta[1, :n_tiles_max], meta[2, :1], xs, layer, wg, wu, wd)
    return jnp.concatenate([pos, jnp.zeros((TM_LAT,), jnp.int32)]), ys


def _final_kernel(pos_ref, x_ref, ys_ref, mod_ref, gpost_ref, o_ref, fbuf, sem, *, tm):
    def add_residual(f):
        o_ref[...] = x_ref[...] + mod_ref[5:6, :] * _rms(f, gpost_ref[...])

    _with_gathered_tile(pos_ref, ys_ref, fbuf, sem, 0, tm, add_residual)


def _final(x, pos, ys, mods, layer, gpost, *, tm, rows_per_batch):
    n, d = x.shape
    tiles_per_batch = rows_per_batch // tm
    row = lambda i, *_: (i, 0)
    grid_spec = pltpu.PrefetchScalarGridSpec(
        num_scalar_prefetch=1, grid=(n // tm,),
        in_specs=[pl.BlockSpec((tm, d), row), pl.BlockSpec(memory_space=pl.ANY),
                  pl.BlockSpec((None, None, 6, d), lambda i, *_: (layer, i // tiles_per_batch, 0, 0)),
                  pl.BlockSpec((1, d), lambda i, *_: (0, 0))],
        out_specs=pl.BlockSpec((tm, d), row),
        scratch_shapes=[pltpu.VMEM((2, tm, d), F32), pltpu.SemaphoreType.DMA((2,))])
    return pl.pallas_call(
        functools.partial(_final_kernel, tm=tm), out_shape=jax.ShapeDtypeStruct((n, d), F32),
        grid_spec=grid_spec, compiler_params=_cparams(1), name="final",
    )(pos, x, ys, mods, gpost)


def kernel(x, c, ctx, c_ctx, w_ada, b_ada, g_mix_pre, g_mix_post, g_ffn_pre, g_ffn_post, conv_w_in, conv_b_in, conv_w_dw, conv_b_dw, conv_ln_g, conv_ln_b, conv_w_out, conv_b_out, attn_w_qkv, attn_w_o, attn_rpb, router_w, router_b, exp_w_gate, exp_w_up, exp_w_down):
    b, s, d = x.shape
    l = ctx.shape[1]
    n, nc = b * s, b * l
    depth = w_ada.shape[0]
    rows = s // GRID_W
    assert depth == 2 and b <= 8 and s % TM_LAT == 0 and l == TM_CTX
    assert rows % Q_ROWS == 0 and rows >= SLAB_ROWS + Q_ROWS
    tm_lat = TM_LAT

    cc = jnp.zeros((MOD_ROWS, d), F32).at[:b].set(c).at[8].set(c_ctx)
    mods = _ada(cc, w_ada, b_ada).reshape(depth, MOD_ROWS, 6, d)

    row = lambda v: v.reshape(1, d)
    rwt = router_w.T
    rwh = rwt.astype(BF16)
    rwl = (rwt - rwh.astype(F32)).astype(BF16)
    rb = router_b.reshape(N_EXPERTS, 1)
    no_counts = jnp.zeros((CNT_ROWS, 128), F32)

    xl = x.reshape(n, d)
    xc = ctx.reshape(nc, d)

    w_in = conv_w_in[0].astype(BF16)
    b_in = conv_b_in[0].reshape(1, 2 * d)
    conv = (conv_w_dw[0], row(conv_b_dw[0]), row(conv_ln_g[0]), row(conv_ln_b[0]),
            conv_w_out[0].astype(BF16), row(conv_b_out[0]))
    tailp = (row(g_mix_post[0]), row(g_ffn_pre[0]), rwh, rwl, rb)
    (u,) = _proj(xl, mods, 0, row(g_mix_pre[0]), w_in, tm=tm_lat, rows_per_batch=s, ctx=False,
                 bias=b_in, name="conv_in")
    (uc,) = _proj(xc, mods, 0, row(g_mix_pre[0]), w_in, tm=TM_CTX, rows_per_batch=l, ctx=True,
                  bias=b_in, name="conv_in_ctx")
    x1, tok, route, counts = _conv_tail(u, xl, mods, 0, conv, tailp, no_counts, tm=tm_lat, rows_per_batch=s,
                                        ctx=False, name="conv_tail")
    xc1, tokc, routec, counts = _conv_tail(uc, xc, mods, 0, conv, tailp, counts, tm=TM_CTX, rows_per_batch=l,
                                           ctx=True, name="conv_tail_ctx")
    pos0, ys0 = _moe_layer([tok, tokc], [route, routec], counts, 0, exp_w_gate, exp_w_up, exp_w_down)

    w_qkv = attn_w_qkv[0].astype(BF16)
    x2, q, k, v = _proj(x1, mods, 1, row(g_mix_pre[1]), w_qkv, tm=tm_lat, rows_per_batch=s, ctx=False,
                        resid=(pos0, ys0, row(g_ffn_post[0]), 0), n_out=3, name="qkv")
    _, kc, vc = _proj(xc1, mods, 1, row(g_mix_pre[1]), w_qkv[:, d:], tm=TM_CTX, rows_per_batch=l, ctx=True,
                      resid=(pos0, ys0, row(g_ffn_post[0]), n), n_out=2, name="kv_ctx")
    c2 = _bias_pairs(attn_rpb[0])
    o = _natten(q.reshape(b, s, d), k.reshape(b, s, d), v.reshape(b, s, d),
                kc.reshape(b, l, d), vc.reshape(b, l, d), c2)
    tailp1 = (row(g_mix_post[1]), row(g_ffn_pre[1]), rwh, rwl, rb)
    x3, tok1, route1, counts1 = _attn_tail(o.reshape(n, d), x2, mods, 1, attn_w_o[0].astype(BF16), tailp1,
                                           no_counts, tm=tm_lat, rows_per_batch=s, name="attn_tail")
    pos1, ys1 = _moe_layer([tok1], [route1], counts1, 1, exp_w_gate, exp_w_up, exp_w_down)
    out = _final(x3, pos1, ys1, mods, 1, row(g_ffn_post[1]), tm=tm_lat, rows_per_batch=s)
    return out.reshape(b, s, d)
```

```python
import functools

import jax
import jax.numpy as jnp
from jax import lax
from jax.experimental import pallas as pl
from jax.experimental.pallas import tpu as pltpu

F32 = jnp.float32
BF16 = jnp.bfloat16

EPS = 1e-6
GRID_W = 64
WIN_H = 8
WIN_W = 16
N_HEADS = 16
HEAD_DIM = 64
N_EXPERTS = 16
N_GROUPS = 4
GROUP_SIZE = N_EXPERTS // N_GROUPS
N_PAIRS = 6
N_BUCKETS = N_GROUPS * N_PAIRS
PAIR_LO = (0, 0, 0, 1, 1, 2)
PAIR_HI = (1, 2, 3, 2, 3, 3)
CONV_WIDTH = 31
CONV_PAD = (CONV_WIDTH - 1) // 2
HALO = 16
MASK_BIAS = -1e30
Q_ROWS = 4
SLAB_ROWS = Q_ROWS + WIN_H
MOD_ROWS = 16
CNT_ROWS = 32
META_LANES = 256
N_DMA_QUEUES = 1

TM_LAT = 512
TM_CTX = 256
TM_MOE = 256
VMEM_LIMIT = 56 * 1024 * 1024

_NT = (((1,), (1,)), ((), ()))


def _cparams(n_axes):
    return pltpu.CompilerParams(dimension_semantics=("arbitrary",) * n_axes,
                                vmem_limit_bytes=VMEM_LIMIT)


def _sigmoid(v):
    return 1.0 / (1.0 + jnp.exp(-v))


def _rms(v, g):
    return v * lax.rsqrt(jnp.mean(v * v, axis=-1, keepdims=True) + EPS) * g


def _split_bf16(v):
    hi = v.astype(BF16)
    lo = (v - hi.astype(F32)).astype(BF16)
    return hi, lo


def _ada_kernel(c_ref, w_ref, b_ref, o_ref):
    a = c_ref[...]
    a = a * _sigmoid(a)
    a_hi, a_lo = _split_bf16(a)
    w_hi, w_lo = _split_bf16(w_ref[...])
    acc = jnp.dot(a_hi, w_hi, preferred_element_type=F32)
    acc += jnp.dot(a_lo, w_hi, preferred_element_type=F32)
    acc += jnp.dot(a_hi, w_lo, preferred_element_type=F32)
    o_ref[...] = acc + b_ref[...]


def _ada(cc, w_ada, b_ada):
    depth, d, n = w_ada.shape
    bn = 1024
    return pl.pallas_call(
        _ada_kernel,
        out_shape=jax.ShapeDtypeStruct((depth, MOD_ROWS, n), F32),
        grid=(depth, n // bn),
        in_specs=[pl.BlockSpec((MOD_ROWS, d), lambda l, j: (0, 0)),
                  pl.BlockSpec((None, d, bn), lambda l, j: (l, 0, j)),
                  pl.BlockSpec((None, 1, bn), lambda l, j: (l, 0, j))],
        out_specs=pl.BlockSpec((None, MOD_ROWS, bn), lambda l, j: (l, 0, j)),
        compiler_params=_cparams(2),
        name="ada",
    )(cc, w_ada, b_ada.reshape(depth, 1, n))


def _start_row_gather(pos_ref, src_ref, buf, sem, first_row, slot, tm):
    for t in range(tm):
        p = pos_ref[first_row + t]
        pltpu.async_copy(src_ref.at[pl.ds(p, 1)], buf.at[slot, pl.ds(t, 1)], sem.at[slot],
                         priority=t % N_DMA_QUEUES)


def _wait_row_gather(src_ref, buf, sem, slot, tm):
    pltpu.make_async_copy(src_ref.at[pl.ds(0, tm)], buf.at[slot], sem.at[slot]).wait()


def _with_gathered_tile(pos_ref, src_ref, buf, sem, row0, tm, consume):
    i = pl.program_id(0)

    @pl.when(i == 0)
    def _():
        _start_row_gather(pos_ref, src_ref, buf, sem, row0, 0, tm)

    for slot in range(2):
        @pl.when(i % 2 == slot)
        def _(slot=slot):
            _wait_row_gather(src_ref, buf, sem, slot, tm)
            _start_row_gather(pos_ref, src_ref, buf, sem, row0 + (i + 1) * tm, 1 - slot, tm)
            consume(buf[slot])

            @pl.when(i == pl.num_programs(0) - 1)
            def _():
                _wait_row_gather(src_ref, buf, sem, 1 - slot, tm)


def _proj_kernel(*refs, fuse_resid, glu, d, tm, f_row0):
    refs = list(refs)
    if fuse_resid:
        pos_ref = refs.pop(0)
    x_ref = refs.pop(0)
    if fuse_resid:
        ys_ref, modp_ref, gpost_ref = refs.pop(0), refs.pop(0), refs.pop(0)
    mod_ref, gpre_ref, w_ref = refs.pop(0), refs.pop(0), refs.pop(0)
    if glu:
        b_ref = refs.pop(0)
    if fuse_resid:
        xn_ref = refs.pop(0)
        fbuf, sem = refs.pop(-2), refs.pop(-1)

    def project(f):
        x = x_ref[...]
        if fuse_resid:
            x = x + _rms(f, gpost_ref[...] * modp_ref[5:6, :])
            xn_ref[...] = x
        h = _rms(x, gpre_ref[...] * (1.0 + mod_ref[1:2, :])) + mod_ref[0:1, :]
        z = jnp.dot(h.astype(BF16), w_ref[...], preferred_element_type=F32)
        if glu:
            z = z + b_ref[...]
            refs[0][...] = (z[:, :d] * _sigmoid(z[:, d:])).astype(BF16)
        else:
            n_out = len(refs)
            for j, o_ref in enumerate(refs):
                part = z[:, j * d:(j + 1) * d]
                if n_out == 3 and j == 0:
                    part = part * (HEAD_DIM ** -0.5)
                o_ref[...] = part.astype(BF16)

    if fuse_resid:
        _with_gathered_tile(pos_ref, ys_ref, fbuf, sem, f_row0, tm, project)
    else:
        project(None)


def _proj(x, mods, layer, gpre, w, *, tm, rows_per_batch, ctx, bias=None, resid=None, n_out=1, name):
    n, d = x.shape
    tiles_per_batch = rows_per_batch // tm
    if ctx:
        mod_idx = lambda i, *_: (layer, 8, 0, 0)
    else:
        mod_idx = lambda i, *_: (layer, i // tiles_per_batch, 0, 0)
    row = lambda i, *_: (i, 0)
    const = lambda i, *_: (0, 0)
    args, specs, scratch, f_row0 = [x], [pl.BlockSpec((tm, d), row)], [], 0
    if resid is not None:
        pos, ys, gpost, f_row0 = resid
        if ctx:
            modp_idx = lambda i, *_: (layer - 1, 8, 0, 0)
        else:
            modp_idx = lambda i, *_: (layer - 1, i // tiles_per_batch, 0, 0)
        args = [pos] + args + [ys, mods, gpost]
        specs += [pl.BlockSpec(memory_space=pl.ANY), pl.BlockSpec((None, None, 6, d), modp_idx),
                  pl.BlockSpec((1, d), const)]
        scratch = [pltpu.VMEM((2, tm, d), F32), pltpu.SemaphoreType.DMA((2,))]
    args += [mods, gpre, w]
    specs += [pl.BlockSpec((None, None, 6, d), mod_idx), pl.BlockSpec((1, d), const),
              pl.BlockSpec(w.shape, const)]
    glu = bias is not None
    if glu:
        args.append(bias)
        specs.append(pl.BlockSpec(bias.shape, const))
    out_shape, out_specs = [], []
    if resid is not None:
        out_shape.append(jax.ShapeDtypeStruct((n, d), F32))
        out_specs.append(pl.BlockSpec((tm, d), row))
    for _ in range(n_out):
        out_shape.append(jax.ShapeDtypeStruct((n, d), BF16))
        out_specs.append(pl.BlockSpec((tm, d), row))
    grid_spec = pltpu.PrefetchScalarGridSpec(
        num_scalar_prefetch=0 if resid is None else 1, grid=(n // tm,),
        in_specs=specs, out_specs=out_specs, scratch_shapes=scratch)
    return pl.pallas_call(
        functools.partial(_proj_kernel, fuse_resid=resid is not None, glu=glu, d=d, tm=tm, f_row0=f_row0),
        out_shape=out_shape, grid_spec=grid_spec, compiler_params=_cparams(1), name=name,
    )(*args)


def _route(lg, rb):
    s = _sigmoid(lg)
    ssel = s + rb
    sel = [ssel[e:e + 1, :] for e in range(N_EXPERTS)]
    raw = [s[e:e + 1, :] for e in range(N_EXPERTS)]
    scores = []
    for g in range(N_GROUPS):
        v0, v1, v2, v3 = sel[4 * g:4 * g + 4]
        a, b = jnp.maximum(v0, v1), jnp.minimum(v0, v1)
        c, e = jnp.maximum(v2, v3), jnp.minimum(v2, v3)
        scores.append(jnp.maximum(a, c) + jnp.maximum(jnp.minimum(a, c), jnp.maximum(b, e)))
    best = scores[0]
    gi = jnp.zeros_like(best)
    for g in range(1, N_GROUPS):
        better = scores[g] > best
        gi = jnp.where(better, float(g), gi)
        best = jnp.where(better, scores[g], best)

    def pick(rows, j):
        return jnp.where(gi == 0.0, rows[j],
                         jnp.where(gi == 1.0, rows[4 + j], jnp.where(gi == 2.0, rows[8 + j], rows[12 + j])))

    v = [pick(sel, j) for j in range(GROUP_SIZE)]
    r = [pick(raw, j) for j in range(GROUP_SIZE)]
    m = []
    for j in range(GROUP_SIZE):
        ahead = jnp.zeros_like(best)
        for k in range(GROUP_SIZE):
            if k == j:
                continue
            beats = (v[k] >= v[j]) if k < j else (v[k] > v[j])
            ahead = ahead + jnp.where(beats, 1.0, 0.0)
        m.append(ahead < 2.0)
    pair = jnp.where(m[0], jnp.where(m[1], 0.0, jnp.where(m[2], 1.0, 2.0)),
                     jnp.where(m[1], jnp.where(m[2], 3.0, 4.0), 5.0))
    sa = jnp.where(m[0], r[0], jnp.where(m[1], r[1], r[2]))
    sb = jnp.where(m[3], r[3], jnp.where(m[2], r[2], r[1]))
    den = sa + sb
    bucket = gi * float(N_PAIRS) + pair
    zeros = jnp.zeros((5, lg.shape[1]), F32)
    return jnp.concatenate([bucket, sa / den, sb / den, zeros], axis=0)


def _tail(y, x, mod_ref, gpost_ref, gffn_ref, rwh_ref, rwl_ref, rb_ref, tri_ref, cin_ref,
          x1_ref, tok_ref, route_ref, cnt_ref):
    x1 = x + _rms(y, gpost_ref[...] * mod_ref[2:3, :])
    x1_ref[...] = x1
    t = _rms(x1, gffn_ref[...] * (1.0 + mod_ref[4:5, :])) + mod_ref[3:4, :]
    tok_ref[...] = t
    t_hi, t_lo = _split_bf16(t)
    lg = lax.dot_general(rwh_ref[...], t_hi, _NT, preferred_element_type=F32)
    lg += lax.dot_general(rwl_ref[...], t_hi, _NT, preferred_element_type=F32)
    lg += lax.dot_general(rwh_ref[...], t_lo, _NT, preferred_element_type=F32)
    rt = _route(lg, rb_ref[...])
    tm = rt.shape[1]

    @pl.when(pl.program_id(0) == 0)
    def _():
        cnt_ref[...] = cin_ref[...]

    bid = lax.broadcasted_iota(jnp.int32, (CNT_ROWS, tm), 0).astype(F32)
    onehot = jnp.where(bid == rt[0:1, :], 1.0, 0.0)
    cum = jnp.dot(onehot.astype(BF16), tri_ref[...], preferred_element_type=F32)
    rank = jnp.sum(onehot * (cum + cnt_ref[:, 0:1]), axis=0, keepdims=True) - 1.0
    cnt_ref[...] = cnt_ref[...] + cum[:, tm - 1:tm]
    route_ref[...] = jnp.concatenate([rt[0:3, :], rank, jnp.zeros((4, tm), F32)], axis=0)


def _conv_tail_kernel(u_ref, up_ref, un_ref, x_ref, mod_ref, wdw_ref, bdw_ref, lng_ref, lnb_ref,
                      wout_ref, bout_ref, gpost_ref, gffn_ref, rwh_ref, rwl_ref, rb_ref, tri_ref, cin_ref,
                      x1_ref, tok_ref, route_ref, cnt_ref, ubuf, ush, cacc, wb, *, tm, tiles_per_seq, chunk):
    t = pl.program_id(0) % tiles_per_seq
    prev_ok = jnp.where(t > 0, 1.0, 0.0)
    next_ok = jnp.where(t < tiles_per_seq - 1, 1.0, 0.0)
    ubuf[0:HALO, :] = up_ref[...].astype(F32) * prev_ok
    ubuf[HALO:HALO + tm, :] = u_ref[...].astype(F32)
    ubuf[HALO + tm:, :] = un_ref[...].astype(F32) * next_ok

    base = HALO - CONV_PAD
    span = tm + 8 * ((CONV_WIDTH - 1 + base) // 8)
    for s in range(1, 8):
        ush[s - 1, 0:span, :] = ubuf[s:s + span, :]

    @pl.when(pl.program_id(0) == 0)
    def _():
        for k in range(CONV_WIDTH):
            wb[k] = jnp.broadcast_to(wdw_ref[k:k + 1, :], (8, wb.shape[2]))

    n_lane_blocks = wb.shape[2] // 128

    def body(c, carry):
        r0 = pl.multiple_of(c * chunk, chunk)
        for lb in range(n_lane_blocks):
            lanes = slice(lb * 128, (lb + 1) * 128)
            acc = None
            for s in range(8):
                if s == 0:
                    win = ubuf[pl.ds(r0, chunk + span - tm), lanes]
                else:
                    win = ush[s - 1, pl.ds(r0, chunk + span - tm), lanes]
                win = win.reshape((chunk + span - tm) // 8, 8, 128)
                for q in range((span - tm) // 8 + 1):
                    k = 8 * q + s - base
                    if 0 <= k < CONV_WIDTH:
                        term = win[q:q + chunk // 8] * wb[k, :, lanes][None]
                        acc = term if acc is None else acc + term
            cacc[pl.ds(r0, chunk), lanes] = acc.reshape(chunk, 128)
        return carry

    lax.fori_loop(0, tm // chunk, body, 0)

    y = cacc[...] + bdw_ref[...]
    mu = jnp.mean(y, axis=-1, keepdims=True)
    yc = y - mu
    var = jnp.mean(yc * yc, axis=-1, keepdims=True)
    yn = yc * lax.rsqrt(var + EPS) * lng_ref[...] + lnb_ref[...]
    act = yn * _sigmoid(yn)
    z = jnp.dot(act.astype(BF16), wout_ref[...], preferred_element_type=F32) + bout_ref[...]
    _tail(z, x_ref[...], mod_ref, gpost_ref, gffn_ref, rwh_ref, rwl_ref, rb_ref, tri_ref, cin_ref,
          x1_ref, tok_ref, route_ref, cnt_ref)


def _tail_common_specs(d, tm):
    const = lambda i: (0, 0)
    return [pl.BlockSpec((1, d), const), pl.BlockSpec((1, d), const),
            pl.BlockSpec((N_EXPERTS, d), const), pl.BlockSpec((N_EXPERTS, d), const),
            pl.BlockSpec((N_EXPERTS, 1), const), pl.BlockSpec((tm, tm), const),
            pl.BlockSpec((CNT_ROWS, 128), const)]


def _tail_common_args(tailp, counts, tm):
    gpost, gffn, rwh, rwl, rb = tailp
    j = lax.broadcasted_iota(jnp.int32, (tm, tm), 0)
    t = lax.broadcasted_iota(jnp.int32, (tm, tm), 1)
    return [gpost, gffn, rwh, rwl, rb, (j <= t).astype(BF16), counts]


def _tail_outs(n, d, tm):
    out_shape = [jax.ShapeDtypeStruct((n, d), F32), jax.ShapeDtypeStruct((n, d), F32),
                 jax.ShapeDtypeStruct((8, n), F32), jax.ShapeDtypeStruct((CNT_ROWS, 128), F32)]
    out_specs = [pl.BlockSpec((tm, d), lambda i: (i, 0)), pl.BlockSpec((tm, d), lambda i: (i, 0)),
                 pl.BlockSpec((8, tm), lambda i: (0, i)), pl.BlockSpec((CNT_ROWS, 128), lambda i: (0, 0))]
    return out_shape, out_specs


def _conv_tail(u, x, mods, layer, conv, tailp, counts, *, tm, rows_per_batch, ctx, name):
    n, d = x.shape
    wdw, bdw, lng, lnb, wout, bout = conv
    tiles_per_seq = rows_per_batch // tm
    hb = tm // HALO
    n_hb = n // HALO
    if ctx:
        mod_idx = lambda i: (layer, 8, 0, 0)
    else:
        mod_idx = lambda i: (layer, i // tiles_per_seq, 0, 0)
    row = lambda i: (i, 0)
    const = lambda i: (0, 0)
    specs = [pl.BlockSpec((tm, d), row),
             pl.BlockSpec((HALO, d), lambda i: (jnp.maximum(i * hb - 1, 0), 0)),
             pl.BlockSpec((HALO, d), lambda i: (jnp.minimum((i + 1) * hb, n_hb - 1), 0)),
             pl.BlockSpec((tm, d), row),
             pl.BlockSpec((None, None, 6, d), mod_idx),
             pl.BlockSpec(wdw.shape, const), pl.BlockSpec((1, d), const),
             pl.BlockSpec((1, d), const), pl.BlockSpec((1, d), const),
             pl.BlockSpec(wout.shape, const), pl.BlockSpec((1, d), const)]
    specs += _tail_common_specs(d, tm)
    out_shape, out_specs = _tail_outs(n, d, tm)
    chunk = 64
    return pl.pallas_call(
        functools.partial(_conv_tail_kernel, tm=tm, tiles_per_seq=tiles_per_seq, chunk=chunk),
        out_shape=out_shape, grid=(n // tm,), in_specs=specs, out_specs=out_specs,
        scratch_shapes=[pltpu.VMEM((tm + 2 * HALO, d), F32), pltpu.VMEM((7, tm + 2 * HALO, d), F32),
                        pltpu.VMEM((tm, d), F32), pltpu.VMEM((CONV_WIDTH, 8, d), F32)],
        compiler_params=_cparams(1), name=name,
    )(u, u, u, x, mods, wdw, bdw, lng, lnb, wout, bout, *_tail_common_args(tailp, counts, tm))


def _attn_tail_kernel(o_ref, x_ref, mod_ref, wo_ref, gpost_ref, gffn_ref, rwh_ref, rwl_ref, rb_ref, tri_ref,
                      cin_ref, x1_ref, tok_ref, route_ref, cnt_ref):
    y = jnp.dot(o_ref[...], wo_ref[...], preferred_element_type=F32)
    _tail(y, x_ref[...], mod_ref, gpost_ref, gffn_ref, rwh_ref, rwl_ref, rb_ref, tri_ref, cin_ref,
          x1_ref, tok_ref, route_ref, cnt_ref)


def _attn_tail(o, x, mods, layer, wo, tailp, counts, *, tm, rows_per_batch, name):
    n, d = x.shape
    tiles_per_batch = rows_per_batch // tm
    mod_idx = lambda i: (layer, i // tiles_per_batch, 0, 0)
    row = lambda i: (i, 0)
    const = lambda i: (0, 0)
    specs = [pl.BlockSpec((tm, d), row), pl.BlockSpec((tm, d), row),
             pl.BlockSpec((None, None, 6, d), mod_idx), pl.BlockSpec(wo.shape, const)]
    specs += _tail_common_specs(d, tm)
    out_shape, out_specs = _tail_outs(n, d, tm)
    return pl.pallas_call(
        _attn_tail_kernel, out_shape=out_shape, grid=(n // tm,), in_specs=specs, out_specs=out_specs,
        compiler_params=_cparams(1), name=name,
    )(o, x, mods, wo, *_tail_common_args(tailp, counts, tm))


def _window_plan(rows):
    n_blocks = rows // Q_ROWS
    plan = []
    for blk in (0, 1, n_blocks - 1):
        slab = min(max(Q_ROWS * blk - WIN_H // 2, 0), rows - SLAB_ROWS)
        per_row = []
        for rl in range(Q_ROWS):
            r = Q_ROWS * blk + rl
            r0 = min(max(r - WIN_H // 2, 0), rows - WIN_H)
            pairs = []
            for kp in range(SLAB_ROWS // 2):
                kr = slab + 2 * kp
                inside = (r0 <= kr < r0 + WIN_H, r0 <= kr + 1 < r0 + WIN_H)
                pairs.append((min(max(kr - r + WIN_H, 0), 2 * WIN_H - 1), inside))
            per_row.append(pairs)
        plan.append(per_row)
    return plan


def _natten_kernel(q_ref, k_ref, v_ref, kc_ref, vc_ref, c2_ref, o_ref, bias, *, rows):
    hw = 2 * HEAD_DIM
    nq = Q_ROWS * GRID_W
    n_slab = SLAB_ROWS * GRID_W
    n_blocks = rows // Q_ROWS
    lane = lax.broadcasted_iota(jnp.int32, (1, hw), 1)
    low = lane < HEAD_DIM

    @pl.when(pl.program_id(1) == 0)
    def _():
        for var, per_row in enumerate(_window_plan(rows)):
            for h in range(2):
                for rl, pairs in enumerate(per_row):
                    for kp, (entry, inside) in enumerate(pairs):
                        if inside[0] or inside[1]:
                            blk = c2_ref[h, entry]
                            if not inside[0]:
                                blk = jnp.where(low, MASK_BIAS, blk)
                            if not inside[1]:
                                blk = jnp.where(low, blk, MASK_BIAS)
                        else:
                            blk = jnp.full((GRID_W, hw), MASK_BIAS, F32)
                        bias[var, h * nq + rl * GRID_W:h * nq + (rl + 1) * GRID_W, kp * hw:(kp + 1) * hw] = blk

    kc = kc_ref[...]
    vc = vc_ref[...]

    def body(blk, carry):
        slab = jnp.clip(Q_ROWS * blk - WIN_H // 2, 0, rows - SLAB_ROWS)
        var = jnp.where(blk == 0, 0, jnp.where(blk == n_blocks - 1, 2, 1))
        q0 = pl.multiple_of(blk * nq, nq)
        k0 = pl.multiple_of(slab * GRID_W, GRID_W)
        q2 = q_ref[pl.ds(q0, nq), :]
        zero = jnp.zeros_like(q2)
        qm = jnp.concatenate([jnp.where(low, q2, zero), jnp.where(low, zero, q2)], axis=0)
        kw = k_ref[pl.ds(k0, n_slab), :]
        vw = v_ref[pl.ds(k0, n_slab), :]
        s_loc = lax.dot_general(qm, kw, _NT, preferred_element_type=F32) + bias[var]
        s_ctx = lax.dot_general(qm, kc, _NT, preferred_element_type=F32)
        m = jnp.maximum(jnp.max(s_loc, axis=-1, keepdims=True), jnp.max(s_ctx, axis=-1, keepdims=True))
        p_loc = jnp.exp(s_loc - m)
        p_ctx = jnp.exp(s_ctx - m)
        den = jnp.sum(p_loc, axis=-1, keepdims=True) + jnp.sum(p_ctx, axis=-1, keepdims=True)
        acc = jnp.dot(p_loc.astype(BF16), vw, preferred_element_type=F32)
        acc += jnp.dot(p_ctx.astype(BF16), vc, preferred_element_type=F32)
        acc = acc / den
        o_ref[pl.ds(q0, nq), :] = jnp.where(low, acc[:nq], acc[nq:]).astype(BF16)
        return carry

    lax.fori_loop(0, n_blocks, body, 0, unroll=8)


def _natten(q, k, v, kc, vc, c2):
    b, s, d = q.shape
    l = kc.shape[1]
    rows = s // GRID_W
    hw = 2 * HEAD_DIM
    lat = pl.BlockSpec((None, s, hw), lambda hp, bi: (bi, 0, hp))
    cspec = pl.BlockSpec((None, l, hw), lambda hp, bi: (bi, 0, hp))
    return pl.pallas_call(
        functools.partial(_natten_kernel, rows=rows),
        out_shape=jax.ShapeDtypeStruct((b, s, d), BF16),
        grid=(d // hw, b),
        in_specs=[lat, lat, lat, cspec, cspec,
                  pl.BlockSpec((2, 2 * WIN_H, GRID_W, hw), lambda hp, bi: (hp, 0, 0, 0))],
        out_specs=lat,
        scratch_shapes=[pltpu.VMEM((3, 2 * Q_ROWS * GRID_W, SLAB_ROWS * GRID_W), F32)],
        compiler_params=_cparams(2), name="natten",
    )(q, k, v, kc, vc, c2)


def _bias_kernel(r_ref, oh_ref, mask_ref, o_ref):
    r = r_ref[...]
    r1 = r.astype(BF16)
    r2 = (r - r1.astype(F32)).astype(BF16)
    r3 = (r - r1.astype(F32) - r2.astype(F32)).astype(BF16)
    oh = oh_ref[...]
    acc = jnp.dot(r1, oh, preferred_element_type=F32)
    acc += jnp.dot(r2, oh, preferred_element_type=F32)
    acc += jnp.dot(r3, oh, preferred_element_type=F32)
    o_ref[...] = acc + mask_ref[...]


def _bias_pairs(rpb):
    n_h, n_dr, n_dc = rpb.shape
    col = jnp.arange(GRID_W)[:, None]
    kcol = jnp.arange(GRID_W)[None, :]
    cs = jnp.clip(col - WIN_W // 2, 0, GRID_W - WIN_W)
    valid = (kcol >= cs) & (kcol < cs + WIN_W)
    dc = jnp.arange(32)[:, None, None]
    onehot = ((kcol - col + WIN_W - 1)[None] == dc) & valid[None]
    onehot = onehot.reshape(32, GRID_W * GRID_W).astype(BF16)
    maskadd = jnp.where(valid, 0.0, MASK_BIAS).reshape(1, GRID_W * GRID_W).astype(F32)
    rp = jnp.zeros((n_h, 16, 32), F32).at[:, :n_dr, :n_dc].set(rpb)
    flat = pl.pallas_call(
        _bias_kernel,
        out_shape=jax.ShapeDtypeStruct((n_h, 16, GRID_W * GRID_W), F32),
        grid=(n_h,),
        in_specs=[pl.BlockSpec((None, 16, 32), lambda h: (h, 0, 0)),
                  pl.BlockSpec((32, GRID_W * GRID_W), lambda h: (0, 0)),
                  pl.BlockSpec((1, GRID_W * GRID_W), lambda h: (0, 0))],
        out_specs=pl.BlockSpec((None, 16, GRID_W * GRID_W), lambda h: (h, 0, 0)),
        compiler_params=_cparams(1), name="bias_expand",
    )(rp, onehot, maskadd)
    tab = flat[:, :n_dr].reshape(n_h, n_dr, GRID_W, GRID_W)
    masked = jnp.full((n_h, 1, GRID_W, GRID_W), MASK_BIAS, F32)
    ext = jnp.concatenate([masked, tab, masked], axis=1)
    return jnp.concatenate([ext[:, :-1], ext[:, 1:]], axis=-1)


def _moe_kernel(ea_ref, eb_ref, nt_ref, xs_ref, wga_ref, wua_ref, wda_ref, wgb_ref, wub_ref, wdb_ref,
                y_ref, cga, cua, cda, cgb, cub, cdb, *, d):
    i = pl.program_id(0)
    prev = jnp.maximum(i - 1, 0)
    fresh = (i == 0) | (ea_ref[i] != ea_ref[prev]) | (eb_ref[i] != eb_ref[prev])

    @pl.when(fresh)
    def _():
        for src, dst in ((wga_ref, cga), (wua_ref, cua), (wda_ref, cda),
                         (wgb_ref, cgb), (wub_ref, cub), (wdb_ref, cdb)):
            dst[...] = src[...].astype(BF16)

    @pl.when(i < nt_ref[0])
    def _():
        xt = xs_ref[:, 0:d].astype(BF16)

        def expert(wg_ref, wu_ref, wd_ref):
            g = jnp.dot(xt, wg_ref[...], preferred_element_type=F32)
            u = jnp.dot(xt, wu_ref[...], preferred_element_type=F32)
            h = (g * _sigmoid(g)) * u
            return jnp.dot(h.astype(BF16), wd_ref[...], preferred_element_type=F32)

        ya = expert(cga, cua, cda)
        yb = expert(cgb, cub, cdb)
        y_ref[...] = xs_ref[:, d + 1:d + 2] * ya + xs_ref[:, d + 2:d + 3] * yb

    @pl.when(i >= nt_ref[0])
    def _():
        y_ref[...] = jnp.zeros_like(y_ref)


def _moe(ea, eb, nt, xs, layer, wg, wu, wd):
    npad, wide = xs.shape
    d = wide - 128
    de = wg.shape[3]
    sel_a = lambda i, ea, eb, nt: (layer, ea[i], 0, 0)
    sel_b = lambda i, ea, eb, nt: (layer, eb[i], 0, 0)
    up = (None, None, d, de)
    down = (None, None, de, d)
    grid_spec = pltpu.PrefetchScalarGridSpec(
        num_scalar_prefetch=3, grid=(npad // TM_MOE,),
        in_specs=[pl.BlockSpec((TM_MOE, wide), lambda i, ea, eb, nt: (jnp.minimum(i, nt[0] - 1), 0)),
                  pl.BlockSpec(up, sel_a), pl.BlockSpec(up, sel_a), pl.BlockSpec(down, sel_a),
                  pl.BlockSpec(up, sel_b), pl.BlockSpec(up, sel_b), pl.BlockSpec(down, sel_b)],
        out_specs=pl.BlockSpec((TM_MOE, d), lambda i, ea, eb, nt: (i, 0)),
        scratch_shapes=[pltpu.VMEM((d, de), BF16), pltpu.VMEM((d, de), BF16), pltpu.VMEM((de, d), BF16),
                        pltpu.VMEM((d, de), BF16), pltpu.VMEM((d, de), BF16), pltpu.VMEM((de, d), BF16)])
    return pl.pallas_call(
        functools.partial(_moe_kernel, d=d), out_shape=jax.ShapeDtypeStruct((npad, d), F32),
        grid_spec=grid_spec, compiler_params=_cparams(1), name="moe",
    )(ea, eb, nt, xs, wg, wu, wd, wg, wu, wd)


def _plan_kernel(route_ref, cnt_ref, eab_ref, pos_ref, meta_ref):
    cnt = cnt_ref[...]
    tiles = jnp.floor((cnt + float(TM_MOE - 1)) * (1.0 / TM_MOE))
    r = lax.broadcasted_iota(jnp.int32, (CNT_ROWS, CNT_ROWS), 0)
    c = lax.broadcasted_iota(jnp.int32, (CNT_ROWS, CNT_ROWS), 1)
    before = jnp.where(c < r, 1.0, 0.0).astype(BF16)
    first = jnp.dot(before, tiles.astype(BF16), preferred_element_type=F32)
    first_c, tiles_c = first[:, 0:1], tiles[:, 0:1]
    nt = jnp.sum(tiles_c, axis=0, keepdims=True)
    n = route_ref.shape[1]
    bid = lax.broadcasted_iota(jnp.int32, (CNT_ROWS, n), 0).astype(F32)
    base = jnp.sum(jnp.where(bid == route_ref[0:1, :], first_c * float(TM_MOE), 0.0), axis=0, keepdims=True)
    pos_ref[...] = (base + route_ref[3:4, :]).astype(jnp.int32)
    ti = jnp.minimum(lax.broadcasted_iota(jnp.int32, (CNT_ROWS, META_LANES), 1).astype(F32), nt - 1.0)
    mine = (ti >= first_c) & (ti < first_c + tiles_c)
    ea = jnp.sum(jnp.where(mine, eab_ref[:, 0:1], 0.0), axis=0, keepdims=True)
    eb = jnp.sum(jnp.where(mine, eab_ref[:, 1:2], 0.0), axis=0, keepdims=True)
    diag = (lax.broadcasted_iota(jnp.int32, (CNT_ROWS, META_LANES), 0)
            == lax.broadcasted_iota(jnp.int32, (CNT_ROWS, META_LANES), 1))
    first_l = jnp.sum(jnp.where(diag, first_c, 0.0), axis=0, keepdims=True)
    cnt_l = jnp.sum(jnp.where(diag, cnt[:, 0:1], 0.0), axis=0, keepdims=True)
    rows = [ea, eb, jnp.broadcast_to(nt, (1, META_LANES)), first_l, cnt_l, jnp.zeros((3, META_LANES), F32)]
    meta_ref[...] = jnp.concatenate(rows, axis=0).astype(jnp.int32)


def _plan(route, counts):
    n = route.shape[1]
    chunk = max(ck for ck in (2048, 1024, 512) if n % ck == 0)
    eab = jnp.zeros((CNT_ROWS, 128), F32)
    eab = eab.at[:N_BUCKETS, 0].set(jnp.asarray(
        [GROUP_SIZE * (bk // N_PAIRS) + PAIR_LO[bk % N_PAIRS] for bk in range(N_BUCKETS)], F32))
    eab = eab.at[:N_BUCKETS, 1].set(jnp.asarray(
        [GROUP_SIZE * (bk // N_PAIRS) + PAIR_HI[bk % N_PAIRS] for bk in range(N_BUCKETS)], F32))
    const = lambda i: (0, 0)
    return pl.pallas_call(
        _plan_kernel,
        out_shape=[jax.ShapeDtypeStruct((1, n), jnp.int32), jax.ShapeDtypeStruct((8, META_LANES), jnp.int32)],
        grid=(n // chunk,),
        in_specs=[pl.BlockSpec((8, chunk), lambda i: (0, i)), pl.BlockSpec((CNT_ROWS, 128), const),
                  pl.BlockSpec((CNT_ROWS, 128), const)],
        out_specs=[pl.BlockSpec((1, chunk), lambda i: (0, i)), pl.BlockSpec((8, META_LANES), const)],
        compiler_params=_cparams(1), name="plan",
    )(route, counts, eab)


def _fill_padding(first_ref, cnt_ref, nt_ref, xs_ref, zbuf, sem, n_tiles_max, *, issue):
    if issue:
        zbuf[...] = jnp.zeros_like(zbuf)

    def fill(first_row, n_rows):
        cp = pltpu.make_async_copy(zbuf.at[pl.ds(0, n_rows)], xs_ref.at[pl.ds(first_row, n_rows)], sem)
        if issue:
            cp.start()
        else:
            cp.wait()

    def fill_tile(j, carry):
        fill(pl.multiple_of(j * TM_MOE, TM_MOE), TM_MOE)
        return carry

    lax.fori_loop(nt_ref[0], n_tiles_max, fill_tile, 0)

    for bk in range(N_BUCKETS):
        cnt = cnt_ref[bk]
        start = first_ref[bk] * TM_MOE + cnt
        pad = (-cnt) & (TM_MOE - 1)
        head = jnp.minimum((-cnt) & 7, pad)
        for j in range(7):
            @pl.when(j < head)
            def _(j=j, start=start):
                fill(start + j, 1)

        body = pad - head
        piece = TM_MOE // 2
        while piece >= 8:
            done = body & ~(2 * piece - 1)

            @pl.when((body & piece) != 0)
            def _(piece=piece, done=done, start=start, head=head):
                fill(pl.multiple_of(start + head + done, 8), piece)

            piece //= 2


def _sort_kernel(*refs, tm, d, step_ranges, n_tiles_max):
    n_src = len(step_ranges)
    pos_ref, first_ref, cnt_ref, nt_ref = refs[:4]
    srcs = refs[4:4 + 2 * n_src]
    xs_ref, buf, zbuf, sem, zsem = refs[4 + 2 * n_src:]
    n_steps = step_ranges[-1][1]
    i = pl.program_id(0)
    slot = i % 2

    def wait_rows(s):
        pltpu.make_async_copy(buf.at[s], xs_ref.at[pl.ds(0, tm)], sem.at[s]).wait()

    @pl.when(i == 0)
    def _():
        _fill_padding(first_ref, cnt_ref, nt_ref, xs_ref, zbuf, zsem, n_tiles_max, issue=True)

    @pl.when(i >= 2)
    def _():
        wait_rows(slot)

    for k, (s0, s1) in enumerate(step_ranges):
        @pl.when((i >= s0) & (i < s1))
        def _(k=k):
            tok_ref, route_ref = srcs[2 * k], srcs[2 * k + 1]
            cols = jnp.concatenate([route_ref[...], jnp.zeros((128 - 8, tm), F32)], axis=0).T
            buf[slot, :, 0:d] = tok_ref[...]
            buf[slot, :, d:] = cols

    for t in range(tm):
        p = pos_ref[i * tm + t]
        pltpu.async_copy(buf.at[slot, pl.ds(t, 1)], xs_ref.at[pl.ds(p, 1)], sem.at[slot],
                         priority=t % N_DMA_QUEUES)

    @pl.when(i == n_steps - 1)
    def _():
        wait_rows(slot)
        if n_steps >= 2:
            wait_rows(1 - slot)
        _fill_padding(first_ref, cnt_ref, nt_ref, xs_ref, zbuf, zsem, n_tiles_max, issue=False)


def _sort_rows(pos, first, cnt, nt, toks, routes, n_tiles_max):
    d = toks[0].shape[1]
    tm = TM_LAT
    step_ranges, s0 = [], 0
    for tok in toks:
        step_ranges.append((s0, s0 + tok.shape[0] // tm))
        s0 = step_ranges[-1][1]
    in_specs, args = [], []
    for (b0, b1), tok, rt in zip(step_ranges, toks, routes):
        blk = lambda i, *_, b0=b0, b1=b1: jnp.clip(i - b0, 0, b1 - b0 - 1)
        in_specs += [pl.BlockSpec((tm, d), lambda i, *_, blk=blk: (blk(i), 0)),
                     pl.BlockSpec((8, tm), lambda i, *_, blk=blk: (0, blk(i)))]
        args += [tok, rt]
    grid_spec = pltpu.PrefetchScalarGridSpec(
        num_scalar_prefetch=4, grid=(s0,), in_specs=in_specs,
        out_specs=pl.BlockSpec(memory_space=pl.ANY),
        scratch_shapes=[pltpu.VMEM((2, tm, d + 128), F32), pltpu.VMEM((TM_MOE, d + 128), F32),
                        pltpu.SemaphoreType.DMA((2,)), pltpu.SemaphoreType.DMA])
    return pl.pallas_call(
        functools.partial(_sort_kernel, tm=tm, d=d, step_ranges=tuple(step_ranges), n_tiles_max=n_tiles_max),
        out_shape=jax.ShapeDtypeStruct((n_tiles_max * TM_MOE, d + 128), F32), grid_spec=grid_spec,
        compiler_params=_cparams(1), name="sort_rows",
    )(pos, first, cnt, nt, *args)


def _moe_layer(toks, routes, counts, layer, wg, wu, wd):
    d = toks[0].shape[1]
    n = sum(t.shape[0] for t in toks)
    n_tiles_max = n // TM_MOE + N_BUCKETS
    assert n_tiles_max <= META_LANES
    route = routes[0] if len(routes) == 1 else jnp.concatenate(routes, axis=1)
    pos, meta = _plan(route, counts)
    pos = pos.reshape(n)
    xs = _sort_rows(pos, meta[3, :N_BUCKETS], meta[4, :N_BUCKETS], meta[2, :1], toks, routes, n_tiles_max)
    ys = _moe(meta[0, :n_tiles_max], meta[1, :n_tiles_max], meta[2, :1], xs, layer, wg, wu, wd)
    return jnp.concatenate([pos, jnp.zeros((TM_LAT,), jnp.int32)]), ys


def _final_kernel(pos_ref, x_ref, ys_ref, mod_ref, gpost_ref, o_ref, fbuf, sem, *, tm):
    def add_residual(f):
        o_ref[...] = x_ref[...] + _rms(f, gpost_ref[...] * mod_ref[5:6, :])

    _with_gathered_tile(pos_ref, ys_ref, fbuf, sem, 0, tm, add_residual)


def _final(x, pos, ys, mods, layer, gpost, *, tm, rows_per_batch):
    n, d = x.shape
    tiles_per_batch = rows_per_batch // tm
    row = lambda i, *_: (i, 0)
    grid_spec = pltpu.PrefetchScalarGridSpec(
        num_scalar_prefetch=1, grid=(n // tm,),
        in_specs=[pl.BlockSpec((tm, d), row), pl.BlockSpec(memory_space=pl.ANY),
                  pl.BlockSpec((None, None, 6, d), lambda i, *_: (layer, i // tiles_per_batch, 0, 0)),
                  pl.BlockSpec((1, d), lambda i, *_: (0, 0))],
        out_specs=pl.BlockSpec((tm, d), row),
        scratch_shapes=[pltpu.VMEM((2, tm, d), F32), pltpu.SemaphoreType.DMA((2,))])
    return pl.pallas_call(
        functools.partial(_final_kernel, tm=tm), out_shape=jax.ShapeDtypeStruct((n, d), F32),
        grid_spec=grid_spec, compiler_params=_cparams(1), name="final",
    )(pos, x, ys, mods, gpost)


def kernel(x, c, ctx, c_ctx, w_ada, b_ada, g_mix_pre, g_mix_post, g_ffn_pre, g_ffn_post, conv_w_in, conv_b_in, conv_w_dw, conv_b_dw, conv_ln_g, conv_ln_b, conv_w_out, conv_b_out, attn_w_qkv, attn_w_o, attn_rpb, router_w, router_b, exp_w_gate, exp_w_up, exp_w_down):
    b, s, d = x.shape
    l = ctx.shape[1]
    n, nc = b * s, b * l
    depth = w_ada.shape[0]
    rows = s // GRID_W
    assert depth == 2 and b <= 8 and s % TM_LAT == 0 and l == TM_CTX
    assert rows % Q_ROWS == 0 and rows >= SLAB_ROWS + Q_ROWS
    tm_lat = TM_LAT

    cc = jnp.zeros((MOD_ROWS, d), F32).at[:b].set(c).at[8].set(c_ctx)
    mods = _ada(cc, w_ada, b_ada).reshape(depth, MOD_ROWS, 6, d)

    row = lambda v: v.reshape(1, d)
    rwt = router_w.T
    rwh = rwt.astype(BF16)
    rwl = (rwt - rwh.astype(F32)).astype(BF16)
    rb = router_b.reshape(N_EXPERTS, 1)
    no_counts = jnp.zeros((CNT_ROWS, 128), F32)

    xl = x.reshape(n, d)
    xc = ctx.reshape(nc, d)

    w_in = conv_w_in[0].astype(BF16)
    b_in = conv_b_in[0].reshape(1, 2 * d)
    conv = (conv_w_dw[0], row(conv_b_dw[0]), row(conv_ln_g[0]), row(conv_ln_b[0]),
            conv_w_out[0].astype(BF16), row(conv_b_out[0]))
    tailp = (row(g_mix_post[0]), row(g_ffn_pre[0]), rwh, rwl, rb)
    (u,) = _proj(xl, mods, 0, row(g_mix_pre[0]), w_in, tm=tm_lat, rows_per_batch=s, ctx=False,
                 bias=b_in, name="conv_in")
    (uc,) = _proj(xc, mods, 0, row(g_mix_pre[0]), w_in, tm=TM_CTX, rows_per_batch=l, ctx=True,
                  bias=b_in, name="conv_in_ctx")
    x1, tok, route, counts = _conv_tail(u, xl, mods, 0, conv, tailp, no_counts, tm=tm_lat, rows_per_batch=s,
                                        ctx=False, name="conv_tail")
    xc1, tokc, routec, counts = _conv_tail(uc, xc, mods, 0, conv, tailp, counts, tm=TM_CTX, rows_per_batch=l,
                                           ctx=True, name="conv_tail_ctx")
    pos0, ys0 = _moe_layer([tok, tokc], [route, routec], counts, 0, exp_w_gate, exp_w_up, exp_w_down)

    w_qkv = attn_w_qkv[0].astype(BF16)
    x2, q, k, v = _proj(x1, mods, 1, row(g_mix_pre[1]), w_qkv, tm=tm_lat, rows_per_batch=s, ctx=False,
                        resid=(pos0, ys0, row(g_ffn_post[0]), 0), n_out=3, name="qkv")
    _, kc, vc = _proj(xc1, mods, 1, row(g_mix_pre[1]), w_qkv[:, d:], tm=TM_CTX, rows_per_batch=l, ctx=True,
                      resid=(pos0, ys0, row(g_ffn_post[0]), n), n_out=2, name="kv_ctx")
    c2 = _bias_pairs(attn_rpb[0])
    o = _natten(q.reshape(b, s, d), k.reshape(b, s, d), v.reshape(b, s, d),
                kc.reshape(b, l, d), vc.reshape(b, l, d), c2)
    tailp1 = (row(g_mix_post[1]), row(g_ffn_pre[1]), rwh, rwl, rb)
    x3, tok1, route1, counts1 = _attn_tail(o.reshape(n, d), x2, mods, 1, attn_w_o[0].astype(BF16), tailp1,
                                           no_counts, tm=tm_lat, rows_per_batch=s, name="attn_tail")
    pos1, ys1 = _moe_layer([tok1], [route1], counts1, 1, exp_w_gate, exp_w_up, exp_w_down)
    out = _final(x3, pos1, ys1, mods, 1, row(g_ffn_post[1]), tm=tm_lat, rows_per_batch=s)
    return out.reshape(b, s, d)
```

```python
import functools

import jax
import jax.numpy as jnp
from jax import lax
from jax.experimental import pallas as pl
from jax.experimental.pallas import tpu as pltpu

F32 = jnp.float32
BF16 = jnp.bfloat16

EPS = 1e-6
GRID_W = 64
WIN_H = 8
WIN_W = 16
N_HEADS = 16
HEAD_DIM = 64
N_EXPERTS = 16
N_GROUPS = 4
GROUP_SIZE = N_EXPERTS // N_GROUPS
N_PAIRS = 6
N_BUCKETS = N_GROUPS * N_PAIRS
PAIR_LO = (0, 0, 0, 1, 1, 2)
PAIR_HI = (1, 2, 3, 2, 3, 3)
CONV_WIDTH = 31
CONV_PAD = (CONV_WIDTH - 1) // 2
HALO = 16
MASK_BIAS = -1e30
Q_ROWS = 4
SLAB_ROWS = Q_ROWS + WIN_H
MOD_ROWS = 16
CNT_ROWS = 32
META_LANES = 256
N_DMA_QUEUES = 1

TM_LAT = 512
TM_CTX = 256
TM_MOE = 512
VMEM_LIMIT = 56 * 1024 * 1024

_NT = (((1,), (1,)), ((), ()))


def _cparams(n_axes):
    return pltpu.CompilerParams(dimension_semantics=("arbitrary",) * n_axes,
                                vmem_limit_bytes=VMEM_LIMIT)


def _sigmoid(v):
    return 1.0 / (1.0 + jnp.exp(-v))


def _rms(v, g):
    return v * lax.rsqrt(jnp.mean(v * v, axis=-1, keepdims=True) + EPS) * g


def _split_bf16(v):
    hi = v.astype(BF16)
    lo = (v - hi.astype(F32)).astype(BF16)
    return hi, lo


def _ada_kernel(c_ref, w_ref, b_ref, o_ref):
    a = c_ref[...]
    a = a * _sigmoid(a)
    a_hi, a_lo = _split_bf16(a)
    w_hi, w_lo = _split_bf16(w_ref[...])
    acc = jnp.dot(a_hi, w_hi, preferred_element_type=F32)
    acc += jnp.dot(a_lo, w_hi, preferred_element_type=F32)
    acc += jnp.dot(a_hi, w_lo, preferred_element_type=F32)
    o_ref[...] = acc + b_ref[...]


def _ada(cc, w_ada, b_ada):
    depth, d, n = w_ada.shape
    bn = 1024
    return pl.pallas_call(
        _ada_kernel,
        out_shape=jax.ShapeDtypeStruct((depth, MOD_ROWS, n), F32),
        grid=(depth, n // bn),
        in_specs=[pl.BlockSpec((MOD_ROWS, d), lambda l, j: (0, 0)),
                  pl.BlockSpec((None, d, bn), lambda l, j: (l, 0, j)),
                  pl.BlockSpec((None, 1, bn), lambda l, j: (l, 0, j))],
        out_specs=pl.BlockSpec((None, MOD_ROWS, bn), lambda l, j: (l, 0, j)),
        compiler_params=_cparams(2),
        name="ada",
    )(cc, w_ada, b_ada.reshape(depth, 1, n))


def _start_row_gather(pos_ref, src_ref, buf, sem, first_row, slot, tm):
    for t in range(tm):
        p = pos_ref[first_row + t]
        pltpu.async_copy(src_ref.at[pl.ds(p, 1)], buf.at[slot, pl.ds(t, 1)], sem.at[slot],
                         priority=t % N_DMA_QUEUES)


def _wait_row_gather(src_ref, buf, sem, slot, tm):
    pltpu.make_async_copy(src_ref.at[pl.ds(0, tm)], buf.at[slot], sem.at[slot]).wait()


def _with_gathered_tile(pos_ref, src_ref, buf, sem, row0, tm, consume):
    i = pl.program_id(0)

    @pl.when(i == 0)
    def _():
        _start_row_gather(pos_ref, src_ref, buf, sem, row0, 0, tm)

    for slot in range(2):
        @pl.when(i % 2 == slot)
        def _(slot=slot):
            _wait_row_gather(src_ref, buf, sem, slot, tm)
            _start_row_gather(pos_ref, src_ref, buf, sem, row0 + (i + 1) * tm, 1 - slot, tm)
            consume(buf[slot])

            @pl.when(i == pl.num_programs(0) - 1)
            def _():
                _wait_row_gather(src_ref, buf, sem, 1 - slot, tm)


def _proj_kernel(*refs, fuse_resid, glu, d, tm, f_row0):
    refs = list(refs)
    if fuse_resid:
        pos_ref = refs.pop(0)
    x_ref = refs.pop(0)
    if fuse_resid:
        ys_ref, modp_ref, gpost_ref = refs.pop(0), refs.pop(0), refs.pop(0)
    mod_ref, gpre_ref, w_ref = refs.pop(0), refs.pop(0), refs.pop(0)
    if glu:
        b_ref = refs.pop(0)
    if fuse_resid:
        xn_ref = refs.pop(0)
        fbuf, sem = refs.pop(-2), refs.pop(-1)

    def project(f):
        x = x_ref[...]
        if fuse_resid:
            x = x + _rms(f, gpost_ref[...] * modp_ref[5:6, :])
            xn_ref[...] = x
        h = _rms(x, gpre_ref[...] * (1.0 + mod_ref[1:2, :])) + mod_ref[0:1, :]
        z = jnp.dot(h.astype(BF16), w_ref[...], preferred_element_type=F32)
        if glu:
            z = z + b_ref[...]
            refs[0][...] = (z[:, :d] * _sigmoid(z[:, d:])).astype(BF16)
        else:
            n_out = len(refs)
            for j, o_ref in enumerate(refs):
                part = z[:, j * d:(j + 1) * d]
                if n_out == 3 and j == 0:
                    part = part * (HEAD_DIM ** -0.5)
                o_ref[...] = part.astype(BF16)

    if fuse_resid:
        _with_gathered_tile(pos_ref, ys_ref, fbuf, sem, f_row0, tm, project)
    else:
        project(None)


def _proj(x, mods, layer, gpre, w, *, tm, rows_per_batch, ctx, bias=None, resid=None, n_out=1, name):
    n, d = x.shape
    tiles_per_batch = rows_per_batch // tm
    if ctx:
        mod_idx = lambda i, *_: (layer, 8, 0, 0)
    else:
        mod_idx = lambda i, *_: (layer, i // tiles_per_batch, 0, 0)
    row = lambda i, *_: (i, 0)
    const = lambda i, *_: (0, 0)
    args, specs, scratch, f_row0 = [x], [pl.BlockSpec((tm, d), row)], [], 0
    if resid is not None:
        pos, ys, gpost, f_row0 = resid
        if ctx:
            modp_idx = lambda i, *_: (layer - 1, 8, 0, 0)
        else:
            modp_idx = lambda i, *_: (layer - 1, i // tiles_per_batch, 0, 0)
        args = [pos] + args + [ys, mods, gpost]
        specs += [pl.BlockSpec(memory_space=pl.ANY), pl.BlockSpec((None, None, 6, d), modp_idx),
                  pl.BlockSpec((1, d), const)]
        scratch = [pltpu.VMEM((2, tm, d), F32), pltpu.SemaphoreType.DMA((2,))]
    args += [mods, gpre, w]
    specs += [pl.BlockSpec((None, None, 6, d), mod_idx), pl.BlockSpec((1, d), const),
              pl.BlockSpec(w.shape, const)]
    glu = bias is not None
    if glu:
        args.append(bias)
        specs.append(pl.BlockSpec(bias.shape, const))
    out_shape, out_specs = [], []
    if resid is not None:
        out_shape.append(jax.ShapeDtypeStruct((n, d), F32))
        out_specs.append(pl.BlockSpec((tm, d), row))
    for _ in range(n_out):
        out_shape.append(jax.ShapeDtypeStruct((n, d), BF16))
        out_specs.append(pl.BlockSpec((tm, d), row))
    grid_spec = pltpu.PrefetchScalarGridSpec(
        num_scalar_prefetch=0 if resid is None else 1, grid=(n // tm,),
        in_specs=specs, out_specs=out_specs, scratch_shapes=scratch)
    return pl.pallas_call(
        functools.partial(_proj_kernel, fuse_resid=resid is not None, glu=glu, d=d, tm=tm, f_row0=f_row0),
        out_shape=out_shape, grid_spec=grid_spec, compiler_params=_cparams(1), name=name,
    )(*args)


def _route(lg, rb):
    s = _sigmoid(lg)
    ssel = s + rb
    sel = [ssel[e:e + 1, :] for e in range(N_EXPERTS)]
    raw = [s[e:e + 1, :] for e in range(N_EXPERTS)]
    scores = []
    for g in range(N_GROUPS):
        v0, v1, v2, v3 = sel[4 * g:4 * g + 4]
        a, b = jnp.maximum(v0, v1), jnp.minimum(v0, v1)
        c, e = jnp.maximum(v2, v3), jnp.minimum(v2, v3)
        scores.append(jnp.maximum(a, c) + jnp.maximum(jnp.minimum(a, c), jnp.maximum(b, e)))
    best = scores[0]
    gi = jnp.zeros_like(best)
    for g in range(1, N_GROUPS):
        better = scores[g] > best
        gi = jnp.where(better, float(g), gi)
        best = jnp.where(better, scores[g], best)

    def pick(rows, j):
        return jnp.where(gi == 0.0, rows[j],
                         jnp.where(gi == 1.0, rows[4 + j], jnp.where(gi == 2.0, rows[8 + j], rows[12 + j])))

    v = [pick(sel, j) for j in range(GROUP_SIZE)]
    r = [pick(raw, j) for j in range(GROUP_SIZE)]
    m = []
    for j in range(GROUP_SIZE):
        ahead = jnp.zeros_like(best)
        for k in range(GROUP_SIZE):
            if k == j:
                continue
            beats = (v[k] >= v[j]) if k < j else (v[k] > v[j])
            ahead = ahead + jnp.where(beats, 1.0, 0.0)
        m.append(ahead < 2.0)
    pair = jnp.where(m[0], jnp.where(m[1], 0.0, jnp.where(m[2], 1.0, 2.0)),
                     jnp.where(m[1], jnp.where(m[2], 3.0, 4.0), 5.0))
    sa = jnp.where(m[0], r[0], jnp.where(m[1], r[1], r[2]))
    sb = jnp.where(m[3], r[3], jnp.where(m[2], r[2], r[1]))
    den = sa + sb
    bucket = gi * float(N_PAIRS) + pair
    zeros = jnp.zeros((5, lg.shape[1]), F32)
    return jnp.concatenate([bucket, sa / den, sb / den, zeros], axis=0)


def _tail(y, x, mod_ref, gpost_ref, gffn_ref, rwh_ref, rwl_ref, rb_ref, tri_ref, cin_ref,
          x1_ref, tok_ref, route_ref, cnt_ref):
    x1 = x + _rms(y, gpost_ref[...] * mod_ref[2:3, :])
    x1_ref[...] = x1
    t = _rms(x1, gffn_ref[...] * (1.0 + mod_ref[4:5, :])) + mod_ref[3:4, :]
    tok_ref[...] = t
    t_hi, t_lo = _split_bf16(t)
    lg = lax.dot_general(rwh_ref[...], t_hi, _NT, preferred_element_type=F32)
    lg += lax.dot_general(rwl_ref[...], t_hi, _NT, preferred_element_type=F32)
    lg += lax.dot_general(rwh_ref[...], t_lo, _NT, preferred_element_type=F32)
    rt = _route(lg, rb_ref[...])
    tm = rt.shape[1]

    @pl.when(pl.program_id(0) == 0)
    def _():
        cnt_ref[...] = cin_ref[...]

    bid = lax.broadcasted_iota(jnp.int32, (CNT_ROWS, tm), 0).astype(F32)
    onehot = jnp.where(bid == rt[0:1, :], 1.0, 0.0)
    cum = jnp.dot(onehot.astype(BF16), tri_ref[...], preferred_element_type=F32)
    rank = jnp.sum(onehot * (cum + cnt_ref[:, 0:1]), axis=0, keepdims=True) - 1.0
    cnt_ref[...] = cnt_ref[...] + cum[:, tm - 1:tm]
    route_ref[...] = jnp.concatenate([rt[0:3, :], rank, jnp.zeros((4, tm), F32)], axis=0)


def _conv_tail_kernel(u_ref, up_ref, un_ref, x_ref, mod_ref, wdw_ref, bdw_ref, lng_ref, lnb_ref,
                      wout_ref, bout_ref, gpost_ref, gffn_ref, rwh_ref, rwl_ref, rb_ref, tri_ref, cin_ref,
                      x1_ref, tok_ref, route_ref, cnt_ref, ubuf, ush, cacc, wb, *, tm, tiles_per_seq, chunk):
    t = pl.program_id(0) % tiles_per_seq
    prev_ok = jnp.where(t > 0, 1.0, 0.0)
    next_ok = jnp.where(t < tiles_per_seq - 1, 1.0, 0.0)
    ubuf[0:HALO, :] = up_ref[...].astype(F32) * prev_ok
    ubuf[HALO:HALO + tm, :] = u_ref[...].astype(F32)
    ubuf[HALO + tm:, :] = un_ref[...].astype(F32) * next_ok

    base = HALO - CONV_PAD
    span = tm + 8 * ((CONV_WIDTH - 1 + base) // 8)
    for s in range(1, 8):
        ush[s - 1, 0:span, :] = ubuf[s:s + span, :]

    @pl.when(pl.program_id(0) == 0)
    def _():
        for k in range(CONV_WIDTH):
            wb[k] = jnp.broadcast_to(wdw_ref[k:k + 1, :], (8, wb.shape[2]))

    n_lane_blocks = wb.shape[2] // 128

    def body(c, carry):
        r0 = pl.multiple_of(c * chunk, chunk)
        for lb in range(n_lane_blocks):
            lanes = slice(lb * 128, (lb + 1) * 128)
            acc = None
            for s in range(8):
                if s == 0:
                    win = ubuf[pl.ds(r0, chunk + span - tm), lanes]
                else:
                    win = ush[s - 1, pl.ds(r0, chunk + span - tm), lanes]
                win = win.reshape((chunk + span - tm) // 8, 8, 128)
                for q in range((span - tm) // 8 + 1):
                    k = 8 * q + s - base
                    if 0 <= k < CONV_WIDTH:
                        term = win[q:q + chunk // 8] * wb[k, :, lanes][None]
                        acc = term if acc is None else acc + term
            cacc[pl.ds(r0, chunk), lanes] = acc.reshape(chunk, 128)
        return carry

    lax.fori_loop(0, tm // chunk, body, 0)

    y = cacc[...] + bdw_ref[...]
    mu = jnp.mean(y, axis=-1, keepdims=True)
    yc = y - mu
    var = jnp.mean(yc * yc, axis=-1, keepdims=True)
    yn = yc * lax.rsqrt(var + EPS) * lng_ref[...] + lnb_ref[...]
    act = yn * _sigmoid(yn)
    z = jnp.dot(act.astype(BF16), wout_ref[...], preferred_element_type=F32) + bout_ref[...]
    _tail(z, x_ref[...], mod_ref, gpost_ref, gffn_ref, rwh_ref, rwl_ref, rb_ref, tri_ref, cin_ref,
          x1_ref, tok_ref, route_ref, cnt_ref)


def _tail_common_specs(d, tm):
    const = lambda i: (0, 0)
    return [pl.BlockSpec((1, d), const), pl.BlockSpec((1, d), const),
            pl.BlockSpec((N_EXPERTS, d), const), pl.BlockSpec((N_EXPERTS, d), const),
            pl.BlockSpec((N_EXPERTS, 1), const), pl.BlockSpec((tm, tm), const),
            pl.BlockSpec((CNT_ROWS, 128), const)]


def _tail_common_args(tailp, counts, tm):
    gpost, gffn, rwh, rwl, rb = tailp
    j = lax.broadcasted_iota(jnp.int32, (tm, tm), 0)
    t = lax.broadcasted_iota(jnp.int32, (tm, tm), 1)
    return [gpost, gffn, rwh, rwl, rb, (j <= t).astype(BF16), counts]


def _tail_outs(n, d, tm):
    out_shape = [jax.ShapeDtypeStruct((n, d), F32), jax.ShapeDtypeStruct((n, d), F32),
                 jax.ShapeDtypeStruct((8, n), F32), jax.ShapeDtypeStruct((CNT_ROWS, 128), F32)]
    out_specs = [pl.BlockSpec((tm, d), lambda i: (i, 0)), pl.BlockSpec((tm, d), lambda i: (i, 0)),
                 pl.BlockSpec((8, tm), lambda i: (0, i)), pl.BlockSpec((CNT_ROWS, 128), lambda i: (0, 0))]
    return out_shape, out_specs


def _conv_tail(u, x, mods, layer, conv, tailp, counts, *, tm, rows_per_batch, ctx, name):
    n, d = x.shape
    wdw, bdw, lng, lnb, wout, bout = conv
    tiles_per_seq = rows_per_batch // tm
    hb = tm // HALO
    n_hb = n // HALO
    if ctx:
        mod_idx = lambda i: (layer, 8, 0, 0)
    else:
        mod_idx = lambda i: (layer, i // tiles_per_seq, 0, 0)
    row = lambda i: (i, 0)
    const = lambda i: (0, 0)
    specs = [pl.BlockSpec((tm, d), row),
             pl.BlockSpec((HALO, d), lambda i: (jnp.maximum(i * hb - 1, 0), 0)),
             pl.BlockSpec((HALO, d), lambda i: (jnp.minimum((i + 1) * hb, n_hb - 1), 0)),
             pl.BlockSpec((tm, d), row),
             pl.BlockSpec((None, None, 6, d), mod_idx),
             pl.BlockSpec(wdw.shape, const), pl.BlockSpec((1, d), const),
             pl.BlockSpec((1, d), const), pl.BlockSpec((1, d), const),
             pl.BlockSpec(wout.shape, const), pl.BlockSpec((1, d), const)]
    specs += _tail_common_specs(d, tm)
    out_shape, out_specs = _tail_outs(n, d, tm)
    chunk = 64
    return pl.pallas_call(
        functools.partial(_conv_tail_kernel, tm=tm, tiles_per_seq=tiles_per_seq, chunk=chunk),
        out_shape=out_shape, grid=(n // tm,), in_specs=specs, out_specs=out_specs,
        scratch_shapes=[pltpu.VMEM((tm + 2 * HALO, d), F32), pltpu.VMEM((7, tm + 2 * HALO, d), F32),
                        pltpu.VMEM((tm, d), F32), pltpu.VMEM((CONV_WIDTH, 8, d), F32)],
        compiler_params=_cparams(1), name=name,
    )(u, u, u, x, mods, wdw, bdw, lng, lnb, wout, bout, *_tail_common_args(tailp, counts, tm))


def _attn_tail_kernel(o_ref, x_ref, mod_ref, wo_ref, gpost_ref, gffn_ref, rwh_ref, rwl_ref, rb_ref, tri_ref,
                      cin_ref, x1_ref, tok_ref, route_ref, cnt_ref):
    y = jnp.dot(o_ref[...], wo_ref[...], preferred_element_type=F32)
    _tail(y, x_ref[...], mod_ref, gpost_ref, gffn_ref, rwh_ref, rwl_ref, rb_ref, tri_ref, cin_ref,
          x1_ref, tok_ref, route_ref, cnt_ref)


def _attn_tail(o, x, mods, layer, wo, tailp, counts, *, tm, rows_per_batch, name):
    n, d = x.shape
    tiles_per_batch = rows_per_batch // tm
    mod_idx = lambda i: (layer, i // tiles_per_batch, 0, 0)
    row = lambda i: (i, 0)
    const = lambda i: (0, 0)
    specs = [pl.BlockSpec((tm, d), row), pl.BlockSpec((tm, d), row),
             pl.BlockSpec((None, None, 6, d), mod_idx), pl.BlockSpec(wo.shape, const)]
    specs += _tail_common_specs(d, tm)
    out_shape, out_specs = _tail_outs(n, d, tm)
    return pl.pallas_call(
        _attn_tail_kernel, out_shape=out_shape, grid=(n // tm,), in_specs=specs, out_specs=out_specs,
        compiler_params=_cparams(1), name=name,
    )(o, x, mods, wo, *_tail_common_args(tailp, counts, tm))


def _window_plan(rows):
    n_blocks = rows // Q_ROWS
    plan = []
    for blk in (0, 1, n_blocks - 1):
        slab = min(max(Q_ROWS * blk - WIN_H // 2, 0), rows - SLAB_ROWS)
        per_row = []
        for rl in range(Q_ROWS):
            r = Q_ROWS * blk + rl
            r0 = min(max(r - WIN_H // 2, 0), rows - WIN_H)
            pairs = []
            for kp in range(SLAB_ROWS // 2):
                kr = slab + 2 * kp
                inside = (r0 <= kr < r0 + WIN_H, r0 <= kr + 1 < r0 + WIN_H)
                pairs.append((min(max(kr - r + WIN_H, 0), 2 * WIN_H - 1), inside))
            per_row.append(pairs)
        plan.append(per_row)
    return plan


def _natten_kernel(q_ref, k_ref, v_ref, kc_ref, vc_ref, c2_ref, o_ref, bias, *, rows):
    hw = 2 * HEAD_DIM
    nq = Q_ROWS * GRID_W
    n_slab = SLAB_ROWS * GRID_W
    n_blocks = rows // Q_ROWS
    lane = lax.broadcasted_iota(jnp.int32, (1, hw), 1)
    low = lane < HEAD_DIM

    @pl.when(pl.program_id(1) == 0)
    def _():
        for var, per_row in enumerate(_window_plan(rows)):
            for h in range(2):
                for rl, pairs in enumerate(per_row):
                    for kp, (entry, inside) in enumerate(pairs):
                        if inside[0] or inside[1]:
                            blk = c2_ref[h, entry]
                            if not inside[0]:
                                blk = jnp.where(low, MASK_BIAS, blk)
                            if not inside[1]:
                                blk = jnp.where(low, blk, MASK_BIAS)
                        else:
                            blk = jnp.full((GRID_W, hw), MASK_BIAS, F32)
                        bias[var, h * nq + rl * GRID_W:h * nq + (rl + 1) * GRID_W, kp * hw:(kp + 1) * hw] = blk

    kc = kc_ref[...]
    vc = vc_ref[...]

    def body(blk, carry):
        slab = jnp.clip(Q_ROWS * blk - WIN_H // 2, 0, rows - SLAB_ROWS)
        var = jnp.where(blk == 0, 0, jnp.where(blk == n_blocks - 1, 2, 1))
        q0 = pl.multiple_of(blk * nq, nq)
        k0 = pl.multiple_of(slab * GRID_W, GRID_W)
        q2 = q_ref[pl.ds(q0, nq), :]
        zero = jnp.zeros_like(q2)
        qm = jnp.concatenate([jnp.where(low, q2, zero), jnp.where(low, zero, q2)], axis=0)
        kw = k_ref[pl.ds(k0, n_slab), :]
        vw = v_ref[pl.ds(k0, n_slab), :]
        s_loc = lax.dot_general(qm, kw, _NT, preferred_element_type=F32) + bias[var]
        s_ctx = lax.dot_general(qm, kc, _NT, preferred_element_type=F32)
        m = jnp.maximum(jnp.max(s_loc, axis=-1, keepdims=True), jnp.max(s_ctx, axis=-1, keepdims=True))
        p_loc = jnp.exp(s_loc - m)
        p_ctx = jnp.exp(s_ctx - m)
        den = jnp.sum(p_loc, axis=-1, keepdims=True) + jnp.sum(p_ctx, axis=-1, keepdims=True)
        acc = jnp.dot(p_loc.astype(BF16), vw, preferred_element_type=F32)
        acc += jnp.dot(p_ctx.astype(BF16), vc, preferred_element_type=F32)
        acc = acc / den
        o_ref[pl.ds(q0, nq), :] = jnp.where(low, acc[:nq], acc[nq:]).astype(BF16)
        return carry

    lax.fori_loop(0, n_blocks, body, 0, unroll=8)


def _natten(q, k, v, kc, vc, c2):
    b, s, d = q.shape
    l = kc.shape[1]
    rows = s // GRID_W
    hw = 2 * HEAD_DIM
    lat = pl.BlockSpec((None, s, hw), lambda hp, bi: (bi, 0, hp))
    cspec = pl.BlockSpec((None, l, hw), lambda hp, bi: (bi, 0, hp))
    return pl.pallas_call(
        functools.partial(_natten_kernel, rows=rows),
        out_shape=jax.ShapeDtypeStruct((b, s, d), BF16),
        grid=(d // hw, b),
        in_specs=[lat, lat, lat, cspec, cspec,
                  pl.BlockSpec((2, 2 * WIN_H, GRID_W, hw), lambda hp, bi: (hp, 0, 0, 0))],
        out_specs=lat,
        scratch_shapes=[pltpu.VMEM((3, 2 * Q_ROWS * GRID_W, SLAB_ROWS * GRID_W), F32)],
        compiler_params=_cparams(2), name="natten",
    )(q, k, v, kc, vc, c2)


def _bias_kernel(r_ref, oh_ref, mask_ref, o_ref):
    r = r_ref[...]
    r1 = r.astype(BF16)
    r2 = (r - r1.astype(F32)).astype(BF16)
    r3 = (r - r1.astype(F32) - r2.astype(F32)).astype(BF16)
    oh = oh_ref[...]
    acc = jnp.dot(r1, oh, preferred_element_type=F32)
    acc += jnp.dot(r2, oh, preferred_element_type=F32)
    acc += jnp.dot(r3, oh, preferred_element_type=F32)
    o_ref[...] = acc + mask_ref[...]


def _bias_pairs(rpb):
    n_h, n_dr, n_dc = rpb.shape
    col = jnp.arange(GRID_W)[:, None]
    kcol = jnp.arange(GRID_W)[None, :]
    cs = jnp.clip(col - WIN_W // 2, 0, GRID_W - WIN_W)
    valid = (kcol >= cs) & (kcol < cs + WIN_W)
    dc = jnp.arange(32)[:, None, None]
    onehot = ((kcol - col + WIN_W - 1)[None] == dc) & valid[None]
    onehot = onehot.reshape(32, GRID_W * GRID_W).astype(BF16)
    maskadd = jnp.where(valid, 0.0, MASK_BIAS).reshape(1, GRID_W * GRID_W).astype(F32)
    rp = jnp.zeros((n_h, 16, 32), F32).at[:, :n_dr, :n_dc].set(rpb)
    flat = pl.pallas_call(
        _bias_kernel,
        out_shape=jax.ShapeDtypeStruct((n_h, 16, GRID_W * GRID_W), F32),
        grid=(n_h,),
        in_specs=[pl.BlockSpec((None, 16, 32), lambda h: (h, 0, 0)),
                  pl.BlockSpec((32, GRID_W * GRID_W), lambda h: (0, 0)),
                  pl.BlockSpec((1, GRID_W * GRID_W), lambda h: (0, 0))],
        out_specs=pl.BlockSpec((None, 16, GRID_W * GRID_W), lambda h: (h, 0, 0)),
        compiler_params=_cparams(1), name="bias_expand",
    )(rp, onehot, maskadd)
    tab = flat[:, :n_dr].reshape(n_h, n_dr, GRID_W, GRID_W)
    masked = jnp.full((n_h, 1, GRID_W, GRID_W), MASK_BIAS, F32)
    ext = jnp.concatenate([masked, tab, masked], axis=1)
    return jnp.concatenate([ext[:, :-1], ext[:, 1:]], axis=-1)


def _moe_kernel(ea_ref, eb_ref, nt_ref, xs_ref, wga_ref, wua_ref, wda_ref, wgb_ref, wub_ref, wdb_ref,
                y_ref, cga, cua, cda, cgb, cub, cdb, *, d):
    i = pl.program_id(0)
    prev = jnp.maximum(i - 1, 0)
    fresh = (i == 0) | (ea_ref[i] != ea_ref[prev]) | (eb_ref[i] != eb_ref[prev])

    @pl.when(fresh)
    def _():
        for src, dst in ((wga_ref, cga), (wua_ref, cua), (wda_ref, cda),
                         (wgb_ref, cgb), (wub_ref, cub), (wdb_ref, cdb)):
            dst[...] = src[...].astype(BF16)

    @pl.when(i < nt_ref[0])
    def _():
        xt = xs_ref[:, 0:d].astype(BF16)

        def expert(wg_ref, wu_ref, wd_ref):
            g = jnp.dot(xt, wg_ref[...], preferred_element_type=F32)
            u = jnp.dot(xt, wu_ref[...], preferred_element_type=F32)
            h = (g * _sigmoid(g)) * u
            return jnp.dot(h.astype(BF16), wd_ref[...], preferred_element_type=F32)

        ya = expert(cga, cua, cda)
        yb = expert(cgb, cub, cdb)
        y_ref[...] = xs_ref[:, d + 1:d + 2] * ya + xs_ref[:, d + 2:d + 3] * yb

    @pl.when(i >= nt_ref[0])
    def _():
        y_ref[...] = jnp.zeros_like(y_ref)


def _moe(ea, eb, nt, xs, layer, wg, wu, wd):
    npad, wide = xs.shape
    d = wide - 128
    de = wg.shape[3]
    sel_a = lambda i, ea, eb, nt: (layer, ea[i], 0, 0)
    sel_b = lambda i, ea, eb, nt: (layer, eb[i], 0, 0)
    up = (None, None, d, de)
    down = (None, None, de, d)
    grid_spec = pltpu.PrefetchScalarGridSpec(
        num_scalar_prefetch=3, grid=(npad // TM_MOE,),
        in_specs=[pl.BlockSpec((TM_MOE, wide), lambda i, ea, eb, nt: (jnp.minimum(i, nt[0] - 1), 0)),
                  pl.BlockSpec(up, sel_a), pl.BlockSpec(up, sel_a), pl.BlockSpec(down, sel_a),
                  pl.BlockSpec(up, sel_b), pl.BlockSpec(up, sel_b), pl.BlockSpec(down, sel_b)],
        out_specs=pl.BlockSpec((TM_MOE, d), lambda i, ea, eb, nt: (i, 0)),
        scratch_shapes=[pltpu.VMEM((d, de), BF16), pltpu.VMEM((d, de), BF16), pltpu.VMEM((de, d), BF16),
                        pltpu.VMEM((d, de), BF16), pltpu.VMEM((d, de), BF16), pltpu.VMEM((de, d), BF16)])
    return pl.pallas_call(
        functools.partial(_moe_kernel, d=d), out_shape=jax.ShapeDtypeStruct((npad, d), F32),
        grid_spec=grid_spec, compiler_params=_cparams(1), name="moe",
    )(ea, eb, nt, xs, wg, wu, wd, wg, wu, wd)


def _plan_kernel(route_ref, cnt_ref, eab_ref, pos_ref, meta_ref):
    cnt = cnt_ref[...]
    tiles = jnp.floor((cnt + float(TM_MOE - 1)) * (1.0 / TM_MOE))
    r = lax.broadcasted_iota(jnp.int32, (CNT_ROWS, CNT_ROWS), 0)
    c = lax.broadcasted_iota(jnp.int32, (CNT_ROWS, CNT_ROWS), 1)
    before = jnp.where(c < r, 1.0, 0.0).astype(BF16)
    first = jnp.dot(before, tiles.astype(BF16), preferred_element_type=F32)
    first_c, tiles_c = first[:, 0:1], tiles[:, 0:1]
    nt = jnp.sum(tiles_c, axis=0, keepdims=True)
    n = route_ref.shape[1]
    bid = lax.broadcasted_iota(jnp.int32, (CNT_ROWS, n), 0).astype(F32)
    base = jnp.sum(jnp.where(bid == route_ref[0:1, :], first_c * float(TM_MOE), 0.0), axis=0, keepdims=True)
    pos_ref[...] = (base + route_ref[3:4, :]).astype(jnp.int32)
    ti = jnp.minimum(lax.broadcasted_iota(jnp.int32, (CNT_ROWS, META_LANES), 1).astype(F32), nt - 1.0)
    mine = (ti >= first_c) & (ti < first_c + tiles_c)
    ea = jnp.sum(jnp.where(mine, eab_ref[:, 0:1], 0.0), axis=0, keepdims=True)
    eb = jnp.sum(jnp.where(mine, eab_ref[:, 1:2], 0.0), axis=0, keepdims=True)
    diag = (lax.broadcasted_iota(jnp.int32, (CNT_ROWS, META_LANES), 0)
            == lax.broadcasted_iota(jnp.int32, (CNT_ROWS, META_LANES), 1))
    first_l = jnp.sum(jnp.where(diag, first_c, 0.0), axis=0, keepdims=True)
    cnt_l = jnp.sum(jnp.where(diag, cnt[:, 0:1], 0.0), axis=0, keepdims=True)
    rows = [ea, eb, jnp.broadcast_to(nt, (1, META_LANES)), first_l, cnt_l, jnp.zeros((3, META_LANES), F32)]
    meta_ref[...] = jnp.concatenate(rows, axis=0).astype(jnp.int32)


def _plan(route, counts):
    n = route.shape[1]
    chunk = max(ck for ck in (2048, 1024, 512) if n % ck == 0)
    eab = jnp.zeros((CNT_ROWS, 128), F32)
    eab = eab.at[:N_BUCKETS, 0].set(jnp.asarray(
        [GROUP_SIZE * (bk // N_PAIRS) + PAIR_LO[bk % N_PAIRS] for bk in range(N_BUCKETS)], F32))
    eab = eab.at[:N_BUCKETS, 1].set(jnp.asarray(
        [GROUP_SIZE * (bk // N_PAIRS) + PAIR_HI[bk % N_PAIRS] for bk in range(N_BUCKETS)], F32))
    const = lambda i: (0, 0)
    return pl.pallas_call(
        _plan_kernel,
        out_shape=[jax.ShapeDtypeStruct((1, n), jnp.int32), jax.ShapeDtypeStruct((8, META_LANES), jnp.int32)],
        grid=(n // chunk,),
        in_specs=[pl.BlockSpec((8, chunk), lambda i: (0, i)), pl.BlockSpec((CNT_ROWS, 128), const),
                  pl.BlockSpec((CNT_ROWS, 128), const)],
        out_specs=[pl.BlockSpec((1, chunk), lambda i: (0, i)), pl.BlockSpec((8, META_LANES), const)],
        compiler_params=_cparams(1), name="plan",
    )(route, counts, eab)


def _fill_padding(first_ref, cnt_ref, nt_ref, xs_ref, zbuf, sem, n_tiles_max, *, issue):
    if issue:
        zbuf[...] = jnp.zeros_like(zbuf)

    def fill(first_row, n_rows):
        cp = pltpu.make_async_copy(zbuf.at[pl.ds(0, n_rows)], xs_ref.at[pl.ds(first_row, n_rows)], sem)
        if issue:
            cp.start()
        else:
            cp.wait()

    def fill_tile(j, carry):
        fill(pl.multiple_of(j * TM_MOE, TM_MOE), TM_MOE)
        return carry

    lax.fori_loop(nt_ref[0], n_tiles_max, fill_tile, 0)

    for bk in range(N_BUCKETS):
        cnt = cnt_ref[bk]
        start = first_ref[bk] * TM_MOE + cnt
        pad = (-cnt) & (TM_MOE - 1)
        head = jnp.minimum((-cnt) & 7, pad)
        for j in range(7):
            @pl.when(j < head)
            def _(j=j, start=start):
                fill(start + j, 1)

        body = pad - head
        piece = TM_MOE // 2
        while piece >= 8:
            done = body & ~(2 * piece - 1)

            @pl.when((body & piece) != 0)
            def _(piece=piece, done=done, start=start, head=head):
                fill(pl.multiple_of(start + head + done, 8), piece)

            piece //= 2


def _sort_kernel(*refs, tm, d, step_ranges, n_tiles_max):
    n_src = len(step_ranges)
    pos_ref, first_ref, cnt_ref, nt_ref = refs[:4]
    srcs = refs[4:4 + 2 * n_src]
    xs_ref, buf, zbuf, sem, zsem = refs[4 + 2 * n_src:]
    n_steps = step_ranges[-1][1]
    i = pl.program_id(0)
    slot = i % 2

    def wait_rows(s):
        pltpu.make_async_copy(buf.at[s], xs_ref.at[pl.ds(0, tm)], sem.at[s]).wait()

    @pl.when(i == 0)
    def _():
        _fill_padding(first_ref, cnt_ref, nt_ref, xs_ref, zbuf, zsem, n_tiles_max, issue=True)

    @pl.when(i >= 2)
    def _():
        wait_rows(slot)

    for k, (s0, s1) in enumerate(step_ranges):
        @pl.when((i >= s0) & (i < s1))
        def _(k=k):
            tok_ref, route_ref = srcs[2 * k], srcs[2 * k + 1]
            cols = jnp.concatenate([route_ref[...], jnp.zeros((128 - 8, tm), F32)], axis=0).T
            buf[slot, :, 0:d] = tok_ref[...]
            buf[slot, :, d:] = cols

    for t in range(tm):
        p = pos_ref[i * tm + t]
        pltpu.async_copy(buf.at[slot, pl.ds(t, 1)], xs_ref.at[pl.ds(p, 1)], sem.at[slot],
                         priority=t % N_DMA_QUEUES)

    @pl.when(i == n_steps - 1)
    def _():
        wait_rows(slot)
        if n_steps >= 2:
            wait_rows(1 - slot)
        _fill_padding(first_ref, cnt_ref, nt_ref, xs_ref, zbuf, zsem, n_tiles_max, issue=False)


def _sort_rows(pos, first, cnt, nt, toks, routes, n_tiles_max):
    d = toks[0].shape[1]
    tm = TM_LAT
    step_ranges, s0 = [], 0
    for tok in toks:
        step_ranges.append((s0, s0 + tok.shape[0] // tm))
        s0 = step_ranges[-1][1]
    in_specs, args = [], []
    for (b0, b1), tok, rt in zip(step_ranges, toks, routes):
        blk = lambda i, *_, b0=b0, b1=b1: jnp.clip(i - b0, 0, b1 - b0 - 1)
        in_specs += [pl.BlockSpec((tm, d), lambda i, *_, blk=blk: (blk(i), 0)),
                     pl.BlockSpec((8, tm), lambda i, *_, blk=blk: (0, blk(i)))]
        args += [tok, rt]
    grid_spec = pltpu.PrefetchScalarGridSpec(
        num_scalar_prefetch=4, grid=(s0,), in_specs=in_specs,
        out_specs=pl.BlockSpec(memory_space=pl.ANY),
        scratch_shapes=[pltpu.VMEM((2, tm, d + 128), F32), pltpu.VMEM((TM_MOE, d + 128), F32),
                        pltpu.SemaphoreType.DMA((2,)), pltpu.SemaphoreType.DMA])
    return pl.pallas_call(
        functools.partial(_sort_kernel, tm=tm, d=d, step_ranges=tuple(step_ranges), n_tiles_max=n_tiles_max),
        out_shape=jax.ShapeDtypeStruct((n_tiles_max * TM_MOE, d + 128), F32), grid_spec=grid_spec,
        compiler_params=_cparams(1), name="sort_rows",
    )(pos, first, cnt, nt, *args)


def _moe_layer(toks, routes, counts, layer, wg, wu, wd):
    d = toks[0].shape[1]
    n = sum(t.shape[0] for t in toks)
    n_tiles_max = n // TM_MOE + N_BUCKETS
    assert n_tiles_max <= META_LANES
    route = routes[0] if len(routes) == 1 else jnp.concatenate(routes, axis=1)
    pos, meta = _plan(route, counts)
    pos = pos.reshape(n)
    xs = _sort_rows(pos, meta[3, :N_BUCKETS], meta[4, :N_BUCKETS], meta[2, :1], toks, routes, n_tiles_max)
    ys = _moe(meta[0, :n_tiles_max], meta[1, :n_tiles_max], meta[2, :1], xs, layer, wg, wu, wd)
    return jnp.concatenate([pos, jnp.zeros((TM_LAT,), jnp.int32)]), ys


def _final_kernel(pos_ref, x_ref, ys_ref, mod_ref, gpost_ref, o_ref, fbuf, sem, *, tm):
    def add_residual(f):
        o_ref[...] = x_ref[...] + _rms(f, gpost_ref[...] * mod_ref[5:6, :])

    _with_gathered_tile(pos_ref, ys_ref, fbuf, sem, 0, tm, add_residual)


def _final(x, pos, ys, mods, layer, gpost, *, tm, rows_per_batch):
    n, d = x.shape
    tiles_per_batch = rows_per_batch // tm
    row = lambda i, *_: (i, 0)
    grid_spec = pltpu.PrefetchScalarGridSpec(
        num_scalar_prefetch=1, grid=(n // tm,),
        in_specs=[pl.BlockSpec((tm, d), row), pl.BlockSpec(memory_space=pl.ANY),
                  pl.BlockSpec((None, None, 6, d), lambda i, *_: (layer, i // tiles_per_batch, 0, 0)),
                  pl.BlockSpec((1, d), lambda i, *_: (0, 0))],
        out_specs=pl.BlockSpec((tm, d), row),
        scratch_shapes=[pltpu.VMEM((2, tm, d), F32), pltpu.SemaphoreType.DMA((2,))])
    return pl.pallas_call(
        functools.partial(_final_kernel, tm=tm), out_shape=jax.ShapeDtypeStruct((n, d), F32),
        grid_spec=grid_spec, compiler_params=_cparams(1), name="final",
    )(pos, x, ys, mods, gpost)


def kernel(x, c, ctx, c_ctx, w_ada, b_ada, g_mix_pre, g_mix_post, g_ffn_pre, g_ffn_post, conv_w_in, conv_b_in, conv_w_dw, conv_b_dw, conv_ln_g, conv_ln_b, conv_w_out, conv_b_out, attn_w_qkv, attn_w_o, attn_rpb, router_w, router_b, exp_w_gate, exp_w_up, exp_w_down):
    b, s, d = x.shape
    l = ctx.shape[1]
    n, nc = b * s, b * l
    depth = w_ada.shape[0]
    rows = s // GRID_W
    assert depth == 2 and b <= 8 and s % TM_LAT == 0 and l == TM_CTX
    assert rows % Q_ROWS == 0 and rows >= SLAB_ROWS + Q_ROWS
    tm_lat = TM_LAT

    cc = jnp.zeros((MOD_ROWS, d), F32).at[:b].set(c).at[8].set(c_ctx)
    mods = _ada(cc, w_ada, b_ada).reshape(depth, MOD_ROWS, 6, d)

    row = lambda v: v.reshape(1, d)
    rwt = router_w.T
    rwh = rwt.astype(BF16)
    rwl = (rwt - rwh.astype(F32)).astype(BF16)
    rb = router_b.reshape(N_EXPERTS, 1)
    no_counts = jnp.zeros((CNT_ROWS, 128), F32)

    xl = x.reshape(n, d)
    xc = ctx.reshape(nc, d)

    w_in = conv_w_in[0].astype(BF16)
    b_in = conv_b_in[0].reshape(1, 2 * d)
    conv = (conv_w_dw[0], row(conv_b_dw[0]), row(conv_ln_g[0]), row(conv_ln_b[0]),
            conv_w_out[0].astype(BF16), row(conv_b_out[0]))
    tailp = (row(g_mix_post[0]), row(g_ffn_pre[0]), rwh, rwl, rb)
    (u,) = _proj(xl, mods, 0, row(g_mix_pre[0]), w_in, tm=tm_lat, rows_per_batch=s, ctx=False,
                 bias=b_in, name="conv_in")
    (uc,) = _proj(xc, mods, 0, row(g_mix_pre[0]), w_in, tm=TM_CTX, rows_per_batch=l, ctx=True,
                  bias=b_in, name="conv_in_ctx")
    x1, tok, route, counts = _conv_tail(u, xl, mods, 0, conv, tailp, no_counts, tm=tm_lat, rows_per_batch=s,
                                        ctx=False, name="conv_tail")
    xc1, tokc, routec, counts = _conv_tail(uc, xc, mods, 0, conv, tailp, counts, tm=TM_CTX, rows_per_batch=l,
                                           ctx=True, name="conv_tail_ctx")
    pos0, ys0 = _moe_layer([tok, tokc], [route, routec], counts, 0, exp_w_gate, exp_w_up, exp_w_down)

    w_qkv = attn_w_qkv[0].astype(BF16)
    x2, q, k, v = _proj(x1, mods, 1, row(g_mix_pre[1]), w_qkv, tm=tm_lat, rows_per_batch=s, ctx=False,
                        resid=(pos0, ys0, row(g_ffn_post[0]), 0), n_out=3, name="qkv")
    _, kc, vc = _proj(xc1, mods, 1, row(g_mix_pre[1]), w_qkv[:, d:], tm=TM_CTX, rows_per_batch=l, ctx=True,
                      resid=(pos0, ys0, row(g_ffn_post[0]), n), n_out=2, name="kv_ctx")
    c2 = _bias_pairs(attn_rpb[0])
    o = _natten(q.reshape(b, s, d), k.reshape(b, s, d), v.reshape(b, s, d),
                kc.reshape(b, l, d), vc.reshape(b, l, d), c2)
    tailp1 = (row(g_mix_post[1]), row(g_ffn_pre[1]), rwh, rwl, rb)
    x3, tok1, route1, counts1 = _attn_tail(o.reshape(n, d), x2, mods, 1, attn_w_o[0].astype(BF16), tailp1,
                                           no_counts, tm=tm_lat, rows_per_batch=s, name="attn_tail")
    pos1, ys1 = _moe_layer([tok1], [route1], counts1, 1, exp_w_gate, exp_w_up, exp_w_down)
    out = _final(x3, pos1, ys1, mods, 1, row(g_ffn_post[1]), tm=tm_lat, rows_per_batch=s)
    return out.reshape(b, s, d)
```

```python
import functools

import jax
import jax.numpy as jnp
from jax import lax
from jax.experimental import pallas as pl
from jax.experimental.pallas import tpu as pltpu

F32 = jnp.float32
BF16 = jnp.bfloat16

LANES = 128
SUBLANES = 8

EPS = 1e-6
GRID_W = 64
WIN_H = 8
WIN_W = 16
N_HEADS = 16
HEAD_DIM = 64
N_EXPERTS = 16
N_GROUPS = 4
GROUP_SIZE = N_EXPERTS // N_GROUPS
N_PAIRS = 6
N_BUCKETS = N_GROUPS * N_PAIRS
PAIR_LO = (0, 0, 0, 1, 1, 2)
PAIR_HI = (1, 2, 3, 2, 3, 3)
CONV_WIDTH = 31
CONV_PAD = (CONV_WIDTH - 1) // 2
HALO = 16
MASK_BIAS = -1e30
Q_ROWS = 4
SLAB_ROWS = Q_ROWS + WIN_H
MOD_ROWS = 16
CTX_MOD_ROW = 8
ROUTE_ROWS = SUBLANES
CNT_ROWS = 32
META_LANES = 256

TM_LAT = 512
TM_CTX = 256
TM_MOE = 512
VMEM_LIMIT = 56 * 1024 * 1024

_NT = (((1,), (1,)), ((), ()))


def _cparams(n_axes):
    return pltpu.CompilerParams(dimension_semantics=("arbitrary",) * n_axes,
                                vmem_limit_bytes=VMEM_LIMIT)


def _sigmoid(v):
    return 1.0 / (1.0 + jnp.exp(-v))


def _rms(v, g):
    return v * lax.rsqrt(jnp.mean(v * v, axis=-1, keepdims=True) + EPS) * g


def _split_bf16(v):
    hi = v.astype(BF16)
    lo = (v - hi.astype(F32)).astype(BF16)
    return hi, lo


def _ada_kernel(c_ref, w_ref, b_ref, o_ref):
    a = c_ref[...]
    a = a * _sigmoid(a)
    a_hi, a_lo = _split_bf16(a)
    w_hi, w_lo = _split_bf16(w_ref[...])
    acc = jnp.dot(a_hi, w_hi, preferred_element_type=F32)
    acc += jnp.dot(a_lo, w_hi, preferred_element_type=F32)
    acc += jnp.dot(a_hi, w_lo, preferred_element_type=F32)
    o_ref[...] = acc + b_ref[...]


def _ada(cc, w_ada, b_ada):
    depth, d, n = w_ada.shape
    bn = 1024
    return pl.pallas_call(
        _ada_kernel,
        out_shape=jax.ShapeDtypeStruct((depth, MOD_ROWS, n), F32),
        grid=(depth, n // bn),
        in_specs=[pl.BlockSpec((MOD_ROWS, d), lambda l, j: (0, 0)),
                  pl.BlockSpec((None, d, bn), lambda l, j: (l, 0, j)),
                  pl.BlockSpec((None, 1, bn), lambda l, j: (l, 0, j))],
        out_specs=pl.BlockSpec((None, MOD_ROWS, bn), lambda l, j: (l, 0, j)),
        compiler_params=_cparams(2),
        name="ada",
    )(cc, w_ada, b_ada.reshape(depth, 1, n))


def _start_row_gather(pos_ref, src_ref, buf, sem, first_row, slot, tm):
    for t in range(tm):
        p = pos_ref[first_row + t]
        pltpu.make_async_copy(src_ref.at[pl.ds(p, 1)], buf.at[slot, pl.ds(t, 1)], sem.at[slot]).start()


def _wait_row_gather(src_ref, buf, sem, slot, tm):
    pltpu.make_async_copy(src_ref.at[pl.ds(0, tm)], buf.at[slot], sem.at[slot]).wait()


def _with_gathered_tile(pos_ref, src_ref, buf, sem, row0, tm, consume):
    i = pl.program_id(0)

    @pl.when(i == 0)
    def _():
        _start_row_gather(pos_ref, src_ref, buf, sem, row0, 0, tm)

    for slot in range(2):
        @pl.when(i % 2 == slot)
        def _(slot=slot):
            _wait_row_gather(src_ref, buf, sem, slot, tm)
            _start_row_gather(pos_ref, src_ref, buf, sem, row0 + (i + 1) * tm, 1 - slot, tm)
            consume(buf[slot])

            @pl.when(i == pl.num_programs(0) - 1)
            def _():
                _wait_row_gather(src_ref, buf, sem, 1 - slot, tm)


def _proj_kernel(*refs, fuse_resid, glu, d, tm, f_row0):
    refs = list(refs)
    if fuse_resid:
        pos_ref = refs.pop(0)
    x_ref = refs.pop(0)
    if fuse_resid:
        ys_ref, modp_ref, gpost_ref = refs.pop(0), refs.pop(0), refs.pop(0)
    mod_ref, gpre_ref, w_ref = refs.pop(0), refs.pop(0), refs.pop(0)
    if glu:
        b_ref = refs.pop(0)
    if fuse_resid:
        xn_ref = refs.pop(0)
        fbuf, sem = refs.pop(-2), refs.pop(-1)

    def project(f):
        x = x_ref[...]
        if fuse_resid:
            x = x + _rms(f, gpost_ref[...] * modp_ref[5:6, :])
            xn_ref[...] = x
        h = _rms(x, gpre_ref[...] * (1.0 + mod_ref[1:2, :])) + mod_ref[0:1, :]
        z = jnp.dot(h.astype(BF16), w_ref[...], preferred_element_type=F32)
        if glu:
            z = z + b_ref[...]
            refs[0][...] = (z[:, :d] * _sigmoid(z[:, d:])).astype(BF16)
        else:
            n_out = len(refs)
            for j, o_ref in enumerate(refs):
                part = z[:, j * d:(j + 1) * d]
                if n_out == 3 and j == 0:
                    part = part * (HEAD_DIM ** -0.5)
                o_ref[...] = part.astype(BF16)

    if fuse_resid:
        _with_gathered_tile(pos_ref, ys_ref, fbuf, sem, f_row0, tm, project)
    else:
        project(None)


def _proj(x, mods, layer, gpre, w, *, tm, rows_per_batch, ctx, bias=None, resid=None, n_out=1, name):
    n, d = x.shape
    tiles_per_batch = rows_per_batch // tm
    if ctx:
        mod_idx = lambda i, *_: (layer, CTX_MOD_ROW, 0, 0)
    else:
        mod_idx = lambda i, *_: (layer, i // tiles_per_batch, 0, 0)
    row = lambda i, *_: (i, 0)
    const = lambda i, *_: (0, 0)
    args, specs, scratch, f_row0 = [x], [pl.BlockSpec((tm, d), row)], [], 0
    if resid is not None:
        pos, ys, gpost, f_row0 = resid
        if ctx:
            modp_idx = lambda i, *_: (layer - 1, CTX_MOD_ROW, 0, 0)
        else:
            modp_idx = lambda i, *_: (layer - 1, i // tiles_per_batch, 0, 0)
        args = [pos] + args + [ys, mods, gpost]
        specs += [pl.BlockSpec(memory_space=pl.ANY), pl.BlockSpec((None, None, 6, d), modp_idx),
                  pl.BlockSpec((1, d), const)]
        scratch = [pltpu.VMEM((2, tm, d), F32), pltpu.SemaphoreType.DMA((2,))]
    args += [mods, gpre, w]
    specs += [pl.BlockSpec((None, None, 6, d), mod_idx), pl.BlockSpec((1, d), const),
              pl.BlockSpec(w.shape, const)]
    glu = bias is not None
    if glu:
        args.append(bias)
        specs.append(pl.BlockSpec(bias.shape, const))
    out_shape, out_specs = [], []
    if resid is not None:
        out_shape.append(jax.ShapeDtypeStruct((n, d), F32))
        out_specs.append(pl.BlockSpec((tm, d), row))
    for _ in range(n_out):
        out_shape.append(jax.ShapeDtypeStruct((n, d), BF16))
        out_specs.append(pl.BlockSpec((tm, d), row))
    grid_spec = pltpu.PrefetchScalarGridSpec(
        num_scalar_prefetch=0 if resid is None else 1, grid=(n // tm,),
        in_specs=specs, out_specs=out_specs, scratch_shapes=scratch)
    return pl.pallas_call(
        functools.partial(_proj_kernel, fuse_resid=resid is not None, glu=glu, d=d, tm=tm, f_row0=f_row0),
        out_shape=out_shape, grid_spec=grid_spec, compiler_params=_cparams(1), name=name,
    )(*args)


def _route(lg, rb):
    s = _sigmoid(lg)
    ssel = s + rb
    sel = [ssel[e:e + 1, :] for e in range(N_EXPERTS)]
    raw = [s[e:e + 1, :] for e in range(N_EXPERTS)]
    scores = []
    for g in range(N_GROUPS):
        v0, v1, v2, v3 = sel[4 * g:4 * g + 4]
        a, b = jnp.maximum(v0, v1), jnp.minimum(v0, v1)
        c, e = jnp.maximum(v2, v3), jnp.minimum(v2, v3)
        scores.append(jnp.maximum(a, c) + jnp.maximum(jnp.minimum(a, c), jnp.maximum(b, e)))
    best = scores[0]
    gi = jnp.zeros_like(best)
    for g in range(1, N_GROUPS):
        better = scores[g] > best
        gi = jnp.where(better, float(g), gi)
        best = jnp.where(better, scores[g], best)

    def pick(rows, j):
        return jnp.where(gi == 0.0, rows[j],
                         jnp.where(gi == 1.0, rows[4 + j], jnp.where(gi == 2.0, rows[8 + j], rows[12 + j])))

    v = [pick(sel, j) for j in range(GROUP_SIZE)]
    r = [pick(raw, j) for j in range(GROUP_SIZE)]
    m = []
    for j in range(GROUP_SIZE):
        ahead = jnp.zeros_like(best)
        for k in range(GROUP_SIZE):
            if k == j:
                continue
            beats = (v[k] >= v[j]) if k < j else (v[k] > v[j])
            ahead = ahead + jnp.where(beats, 1.0, 0.0)
        m.append(ahead < 2.0)
    pair = jnp.where(m[0], jnp.where(m[1], 0.0, jnp.where(m[2], 1.0, 2.0)),
                     jnp.where(m[1], jnp.where(m[2], 3.0, 4.0), 5.0))
    sa = jnp.where(m[0], r[0], jnp.where(m[1], r[1], r[2]))
    sb = jnp.where(m[3], r[3], jnp.where(m[2], r[2], r[1]))
    den = sa + sb
    bucket = gi * float(N_PAIRS) + pair
    zeros = jnp.zeros((5, lg.shape[1]), F32)
    return jnp.concatenate([bucket, sa / den, sb / den, zeros], axis=0)


def _tail(y, x, mod_ref, gpost_ref, gffn_ref, rwh_ref, rwl_ref, rb_ref, tri_ref, cin_ref,
          x1_ref, tok_ref, route_ref, cnt_ref):
    x1 = x + _rms(y, gpost_ref[...] * mod_ref[2:3, :])
    x1_ref[...] = x1
    t = _rms(x1, gffn_ref[...] * (1.0 + mod_ref[4:5, :])) + mod_ref[3:4, :]
    tok_ref[...] = t
    t_hi, t_lo = _split_bf16(t)
    lg = lax.dot_general(rwh_ref[...], t_hi, _NT, preferred_element_type=F32)
    lg += lax.dot_general(rwl_ref[...], t_hi, _NT, preferred_element_type=F32)
    lg += lax.dot_general(rwh_ref[...], t_lo, _NT, preferred_element_type=F32)
    rt = _route(lg, rb_ref[...])
    tm = rt.shape[1]

    @pl.when(pl.program_id(0) == 0)
    def _():
        cnt_ref[...] = cin_ref[...]

    bid = lax.broadcasted_iota(jnp.int32, (CNT_ROWS, tm), 0).astype(F32)
    onehot = jnp.where(bid == rt[0:1, :], 1.0, 0.0)
    cum = jnp.dot(onehot.astype(BF16), tri_ref[...], preferred_element_type=F32)
    rank = jnp.sum(onehot * (cum + cnt_ref[:, 0:1]), axis=0, keepdims=True) - 1.0
    cnt_ref[...] = cnt_ref[...] + cum[:, tm - 1:tm]
    route_ref[...] = jnp.concatenate([rt[0:3, :], rank, jnp.zeros((4, tm), F32)], axis=0)


def _conv_tail_kernel(u_ref, up_ref, un_ref, x_ref, mod_ref, wdw_ref, bdw_ref, lng_ref, lnb_ref,
                      wout_ref, bout_ref, gpost_ref, gffn_ref, rwh_ref, rwl_ref, rb_ref, tri_ref, cin_ref,
                      x1_ref, tok_ref, route_ref, cnt_ref, ubuf, ush, cacc, wb, *, tm, tiles_per_seq, chunk):
    t = pl.program_id(0) % tiles_per_seq
    prev_ok = jnp.where(t > 0, 1.0, 0.0)
    next_ok = jnp.where(t < tiles_per_seq - 1, 1.0, 0.0)
    ubuf[0:HALO, :] = up_ref[...].astype(F32) * prev_ok
    ubuf[HALO:HALO + tm, :] = u_ref[...].astype(F32)
    ubuf[HALO + tm:, :] = un_ref[...].astype(F32) * next_ok

    sub = SUBLANES
    base = HALO - CONV_PAD
    span = tm + sub * ((CONV_WIDTH - 1 + base) // sub)
    for s in range(1, sub):
        ush[s - 1, 0:span, :] = ubuf[s:s + span, :]

    @pl.when(pl.program_id(0) == 0)
    def _():
        for k in range(CONV_WIDTH):
            wb[k] = jnp.broadcast_to(wdw_ref[k:k + 1, :], (sub, wb.shape[2]))

    n_lane_blocks = wb.shape[2] // LANES

    def body(c, carry):
        r0 = pl.multiple_of(c * chunk, chunk)
        for lb in range(n_lane_blocks):
            lanes = slice(lb * LANES, (lb + 1) * LANES)
            acc = None
            for s in range(sub):
                if s == 0:
                    win = ubuf[pl.ds(r0, chunk + span - tm), lanes]
                else:
                    win = ush[s - 1, pl.ds(r0, chunk + span - tm), lanes]
                win = win.reshape((chunk + span - tm) // sub, sub, LANES)
                for q in range((span - tm) // sub + 1):
                    k = sub * q + s - base
                    if 0 <= k < CONV_WIDTH:
                        term = win[q:q + chunk // sub] * wb[k, :, lanes][None]
                        acc = term if acc is None else acc + term
            cacc[pl.ds(r0, chunk), lanes] = acc.reshape(chunk, LANES)
        return carry

    lax.fori_loop(0, tm // chunk, body, 0)

    y = cacc[...] + bdw_ref[...]
    mu = jnp.mean(y, axis=-1, keepdims=True)
    yc = y - mu
    var = jnp.mean(yc * yc, axis=-1, keepdims=True)
    yn = yc * lax.rsqrt(var + EPS) * lng_ref[...] + lnb_ref[...]
    act = yn * _sigmoid(yn)
    z = jnp.dot(act.astype(BF16), wout_ref[...], preferred_element_type=F32) + bout_ref[...]
    _tail(z, x_ref[...], mod_ref, gpost_ref, gffn_ref, rwh_ref, rwl_ref, rb_ref, tri_ref, cin_ref,
          x1_ref, tok_ref, route_ref, cnt_ref)


def _tail_common_specs(d, tm):
    const = lambda i: (0, 0)
    return [pl.BlockSpec((1, d), const), pl.BlockSpec((1, d), const),
            pl.BlockSpec((N_EXPERTS, d), const), pl.BlockSpec((N_EXPERTS, d), const),
            pl.BlockSpec((N_EXPERTS, 1), const), pl.BlockSpec((tm, tm), const),
            pl.BlockSpec((CNT_ROWS, LANES), const)]


def _tail_common_args(tailp, counts, tm):
    gpost, gffn, rwh, rwl, rb = tailp
    j = lax.broadcasted_iota(jnp.int32, (tm, tm), 0)
    t = lax.broadcasted_iota(jnp.int32, (tm, tm), 1)
    return [gpost, gffn, rwh, rwl, rb, (j <= t).astype(BF16), counts]


def _tail_outs(n, d, tm):
    out_shape = [jax.ShapeDtypeStruct((n, d), F32), jax.ShapeDtypeStruct((n, d), F32),
                 jax.ShapeDtypeStruct((ROUTE_ROWS, n), F32), jax.ShapeDtypeStruct((CNT_ROWS, LANES), F32)]
    out_specs = [pl.BlockSpec((tm, d), lambda i: (i, 0)), pl.BlockSpec((tm, d), lambda i: (i, 0)),
                 pl.BlockSpec((ROUTE_ROWS, tm),lambda i: (0, i)), pl.BlockSpec((CNT_ROWS, LANES), lambda i: (0, 0))]
    return out_shape, out_specs


def _conv_tail(u, x, mods, layer, conv, tailp, counts, *, tm, rows_per_batch, ctx, name):
    n, d = x.shape
    wdw, bdw, lng, lnb, wout, bout = conv
    tiles_per_seq = rows_per_batch // tm
    hb = tm // HALO
    n_hb = n // HALO
    if ctx:
        mod_idx = lambda i: (layer, CTX_MOD_ROW, 0, 0)
    else:
        mod_idx = lambda i: (layer, i // tiles_per_seq, 0, 0)
    row = lambda i: (i, 0)
    const = lambda i: (0, 0)
    specs = [pl.BlockSpec((tm, d), row),
             pl.BlockSpec((HALO, d), lambda i: (jnp.maximum(i * hb - 1, 0), 0)),
             pl.BlockSpec((HALO, d), lambda i: (jnp.minimum((i + 1) * hb, n_hb - 1), 0)),
             pl.BlockSpec((tm, d), row),
             pl.BlockSpec((None, None, 6, d), mod_idx),
             pl.BlockSpec(wdw.shape, const), pl.BlockSpec((1, d), const),
             pl.BlockSpec((1, d), const), pl.BlockSpec((1, d), const),
             pl.BlockSpec(wout.shape, const), pl.BlockSpec((1, d), const)]
    specs += _tail_common_specs(d, tm)
    out_shape, out_specs = _tail_outs(n, d, tm)
    chunk = 64
    return pl.pallas_call(
        functools.partial(_conv_tail_kernel, tm=tm, tiles_per_seq=tiles_per_seq, chunk=chunk),
        out_shape=out_shape, grid=(n // tm,), in_specs=specs, out_specs=out_specs,
        scratch_shapes=[pltpu.VMEM((tm + 2 * HALO, d), F32), pltpu.VMEM((SUBLANES - 1, tm + 2 * HALO, d), F32),
                        pltpu.VMEM((tm, d), F32), pltpu.VMEM((CONV_WIDTH, SUBLANES, d), F32)],
        compiler_params=_cparams(1), name=name,
    )(u, u, u, x, mods, wdw, bdw, lng, lnb, wout, bout, *_tail_common_args(tailp, counts, tm))


def _attn_tail_kernel(o_ref, x_ref, mod_ref, wo_ref, gpost_ref, gffn_ref, rwh_ref, rwl_ref, rb_ref, tri_ref,
                      cin_ref, x1_ref, tok_ref, route_ref, cnt_ref):
    y = jnp.dot(o_ref[...], wo_ref[...], preferred_element_type=F32)
    _tail(y, x_ref[...], mod_ref, gpost_ref, gffn_ref, rwh_ref, rwl_ref, rb_ref, tri_ref, cin_ref,
          x1_ref, tok_ref, route_ref, cnt_ref)


def _attn_tail(o, x, mods, layer, wo, tailp, counts, *, tm, rows_per_batch, name):
    n, d = x.shape
    tiles_per_batch = rows_per_batch // tm
    mod_idx = lambda i: (layer, i // tiles_per_batch, 0, 0)
    row = lambda i: (i, 0)
    const = lambda i: (0, 0)
    specs = [pl.BlockSpec((tm, d), row), pl.BlockSpec((tm, d), row),
             pl.BlockSpec((None, None, 6, d), mod_idx), pl.BlockSpec(wo.shape, const)]
    specs += _tail_common_specs(d, tm)
    out_shape, out_specs = _tail_outs(n, d, tm)
    return pl.pallas_call(
        _attn_tail_kernel, out_shape=out_shape, grid=(n // tm,), in_specs=specs, out_specs=out_specs,
        compiler_params=_cparams(1), name=name,
    )(o, x, mods, wo, *_tail_common_args(tailp, counts, tm))


def _window_plan(rows):
    n_blocks = rows // Q_ROWS
    plan = []
    for blk in (0, 1, n_blocks - 1):
        slab = min(max(Q_ROWS * blk - WIN_H // 2, 0), rows - SLAB_ROWS)
        per_row = []
        for rl in range(Q_ROWS):
            r = Q_ROWS * blk + rl
            r0 = min(max(r - WIN_H // 2, 0), rows - WIN_H)
            pairs = []
            for kp in range(SLAB_ROWS // 2):
                kr = slab + 2 * kp
                inside = (r0 <= kr < r0 + WIN_H, r0 <= kr + 1 < r0 + WIN_H)
                pairs.append((min(max(kr - r + WIN_H, 0), 2 * WIN_H - 1), inside))
            per_row.append(pairs)
        plan.append(per_row)
    return plan


def _natten_kernel(q_ref, k_ref, v_ref, kc_ref, vc_ref, c2_ref, o_ref, bias, *, rows):
    hw = 2 * HEAD_DIM
    nq = Q_ROWS * GRID_W
    n_slab = SLAB_ROWS * GRID_W
    n_blocks = rows // Q_ROWS
    lane = lax.broadcasted_iota(jnp.int32, (1, hw), 1)
    low = lane < HEAD_DIM

    @pl.when(pl.program_id(1) == 0)
    def _():
        for var, per_row in enumerate(_window_plan(rows)):
            for h in range(2):
                for rl, pairs in enumerate(per_row):
                    for kp, (entry, inside) in enumerate(pairs):
                        if inside[0] or inside[1]:
                            blk = c2_ref[h, entry]
                            if not inside[0]:
                                blk = jnp.where(low, MASK_BIAS, blk)
                            if not inside[1]:
                                blk = jnp.where(low, blk, MASK_BIAS)
                        else:
                            blk = jnp.full((GRID_W, hw), MASK_BIAS, F32)
                        bias[var, h * nq + rl * GRID_W:h * nq + (rl + 1) * GRID_W, kp * hw:(kp + 1) * hw] = blk

    kc = kc_ref[...]
    vc = vc_ref[...]

    def body(blk, carry):
        slab = jnp.clip(Q_ROWS * blk - WIN_H // 2, 0, rows - SLAB_ROWS)
        var = jnp.where(blk == 0, 0, jnp.where(blk == n_blocks - 1, 2, 1))
        q0 = pl.multiple_of(blk * nq, nq)
        k0 = pl.multiple_of(slab * GRID_W, GRID_W)
        q2 = q_ref[pl.ds(q0, nq), :]
        zero = jnp.zeros_like(q2)
        qm = jnp.concatenate([jnp.where(low, q2, zero), jnp.where(low, zero, q2)], axis=0)
        kw = k_ref[pl.ds(k0, n_slab), :]
        vw = v_ref[pl.ds(k0, n_slab), :]
        s_loc = lax.dot_general(qm, kw, _NT, preferred_element_type=F32) + bias[var]
        s_ctx = lax.dot_general(qm, kc, _NT, preferred_element_type=F32)
        m = jnp.maximum(jnp.max(s_loc, axis=-1, keepdims=True), jnp.max(s_ctx, axis=-1, keepdims=True))
        p_loc = jnp.exp(s_loc - m)
        p_ctx = jnp.exp(s_ctx - m)
        den = jnp.sum(p_loc, axis=-1, keepdims=True) + jnp.sum(p_ctx, axis=-1, keepdims=True)
        acc = jnp.dot(p_loc.astype(BF16), vw, preferred_element_type=F32)
        acc += jnp.dot(p_ctx.astype(BF16), vc, preferred_element_type=F32)
        acc = acc / den
        o_ref[pl.ds(q0, nq), :] = jnp.where(low, acc[:nq], acc[nq:]).astype(BF16)
        return carry

    lax.fori_loop(0, n_blocks, body, 0, unroll=16)


def _natten(q, k, v, kc, vc, c2):
    b, s, d = q.shape
    l = kc.shape[1]
    rows = s // GRID_W
    hw = 2 * HEAD_DIM
    lat = pl.BlockSpec((None, s, hw), lambda hp, bi: (bi, 0, hp))
    cspec = pl.BlockSpec((None, l, hw), lambda hp, bi: (bi, 0, hp))
    return pl.pallas_call(
        functools.partial(_natten_kernel, rows=rows),
        out_shape=jax.ShapeDtypeStruct((b, s, d), BF16),
        grid=(d // hw, b),
        in_specs=[lat, lat, lat, cspec, cspec,
                  pl.BlockSpec((2, 2 * WIN_H, GRID_W, hw), lambda hp, bi: (hp, 0, 0, 0))],
        out_specs=lat,
        scratch_shapes=[pltpu.VMEM((3, 2 * Q_ROWS * GRID_W, SLAB_ROWS * GRID_W), F32)],
        compiler_params=_cparams(2), name="natten",
    )(q, k, v, kc, vc, c2)


def _bias_kernel(r_ref, oh_ref, mask_ref, o_ref):
    r = r_ref[...]
    r1 = r.astype(BF16)
    r2 = (r - r1.astype(F32)).astype(BF16)
    r3 = (r - r1.astype(F32) - r2.astype(F32)).astype(BF16)
    oh = oh_ref[...]
    acc = jnp.dot(r1, oh, preferred_element_type=F32)
    acc += jnp.dot(r2, oh, preferred_element_type=F32)
    acc += jnp.dot(r3, oh, preferred_element_type=F32)
    o_ref[...] = acc + mask_ref[...]


def _bias_pairs(rpb):
    n_h, n_dr, n_dc = rpb.shape
    col = jnp.arange(GRID_W)[:, None]
    kcol = jnp.arange(GRID_W)[None, :]
    cs = jnp.clip(col - WIN_W // 2, 0, GRID_W - WIN_W)
    valid = (kcol >= cs) & (kcol < cs + WIN_W)
    dc = jnp.arange(32)[:, None, None]
    onehot = ((kcol - col + WIN_W - 1)[None] == dc) & valid[None]
    onehot = onehot.reshape(32, GRID_W * GRID_W).astype(BF16)
    maskadd = jnp.where(valid, 0.0, MASK_BIAS).reshape(1, GRID_W * GRID_W).astype(F32)
    rp = jnp.zeros((n_h, 16, 32), F32).at[:, :n_dr, :n_dc].set(rpb)
    flat = pl.pallas_call(
        _bias_kernel,
        out_shape=jax.ShapeDtypeStruct((n_h, 16, GRID_W * GRID_W), F32),
        grid=(n_h,),
        in_specs=[pl.BlockSpec((None, 16, 32), lambda h: (h, 0, 0)),
                  pl.BlockSpec((32, GRID_W * GRID_W), lambda h: (0, 0)),
                  pl.BlockSpec((1, GRID_W * GRID_W), lambda h: (0, 0))],
        out_specs=pl.BlockSpec((None, 16, GRID_W * GRID_W), lambda h: (h, 0, 0)),
        compiler_params=_cparams(1), name="bias_expand",
    )(rp, onehot, maskadd)
    tab = flat[:, :n_dr].reshape(n_h, n_dr, GRID_W, GRID_W)
    masked = jnp.full((n_h, 1, GRID_W, GRID_W), MASK_BIAS, F32)
    ext = jnp.concatenate([masked, tab, masked], axis=1)
    return jnp.concatenate([ext[:, :-1], ext[:, 1:]], axis=-1)


def _moe_kernel(ea_ref, eb_ref, nt_ref, xs_ref, wga_ref, wua_ref, wda_ref, wgb_ref, wub_ref, wdb_ref,
                y_ref, cga, cua, cda, cgb, cub, cdb, *, d):
    i = pl.program_id(0)
    prev = jnp.maximum(i - 1, 0)
    fresh = (i == 0) | (ea_ref[i] != ea_ref[prev]) | (eb_ref[i] != eb_ref[prev])

    @pl.when(fresh)
    def _():
        for src, dst in ((wga_ref, cga), (wua_ref, cua), (wda_ref, cda),
                         (wgb_ref, cgb), (wub_ref, cub), (wdb_ref, cdb)):
            dst[...] = src[...].astype(BF16)

    @pl.when(i < nt_ref[0])
    def _():
        xt = xs_ref[:, 0:d].astype(BF16)

        def expert(wg_ref, wu_ref, wd_ref):
            g = jnp.dot(xt, wg_ref[...], preferred_element_type=F32)
            u = jnp.dot(xt, wu_ref[...], preferred_element_type=F32)
            h = (g * _sigmoid(g)) * u
            return jnp.dot(h.astype(BF16), wd_ref[...], preferred_element_type=F32)

        ya = expert(cga, cua, cda)
        yb = expert(cgb, cub, cdb)
        y_ref[...] = xs_ref[:, d + 1:d + 2] * ya + xs_ref[:, d + 2:d + 3] * yb

    @pl.when(i >= nt_ref[0])
    def _():
        y_ref[...] = jnp.zeros_like(y_ref)


def _moe(ea, eb, nt, xs, layer, wg, wu, wd):
    npad, wide = xs.shape
    d = wide - LANES
    de = wg.shape[3]
    sel_a = lambda i, ea, eb, nt: (layer, ea[i], 0, 0)
    sel_b = lambda i, ea, eb, nt: (layer, eb[i], 0, 0)
    up = (None, None, d, de)
    down = (None, None, de, d)
    grid_spec = pltpu.PrefetchScalarGridSpec(
        num_scalar_prefetch=3, grid=(npad // TM_MOE,),
        in_specs=[pl.BlockSpec((TM_MOE, wide), lambda i, ea, eb, nt: (jnp.minimum(i, nt[0] - 1), 0)),
                  pl.BlockSpec(up, sel_a), pl.BlockSpec(up, sel_a), pl.BlockSpec(down, sel_a),
                  pl.BlockSpec(up, sel_b), pl.BlockSpec(up, sel_b), pl.BlockSpec(down, sel_b)],
        out_specs=pl.BlockSpec((TM_MOE, d), lambda i, ea, eb, nt: (i, 0)),
        scratch_shapes=[pltpu.VMEM((d, de), BF16), pltpu.VMEM((d, de), BF16), pltpu.VMEM((de, d), BF16),
                        pltpu.VMEM((d, de), BF16), pltpu.VMEM((d, de), BF16), pltpu.VMEM((de, d), BF16)])
    return pl.pallas_call(
        functools.partial(_moe_kernel, d=d), out_shape=jax.ShapeDtypeStruct((npad, d), F32),
        grid_spec=grid_spec, compiler_params=_cparams(1), name="moe",
    )(ea, eb, nt, xs, wg, wu, wd, wg, wu, wd)


def _plan_kernel(route_ref, cnt_ref, eab_ref, pos_ref, meta_ref):
    cnt = cnt_ref[...]
    tiles = jnp.floor((cnt + float(TM_MOE - 1)) * (1.0 / TM_MOE))
    r = lax.broadcasted_iota(jnp.int32, (CNT_ROWS, CNT_ROWS), 0)
    c = lax.broadcasted_iota(jnp.int32, (CNT_ROWS, CNT_ROWS), 1)
    before = jnp.where(c < r, 1.0, 0.0).astype(BF16)
    first = jnp.dot(before, tiles.astype(BF16), preferred_element_type=F32)
    first_c, tiles_c = first[:, 0:1], tiles[:, 0:1]
    nt = jnp.sum(tiles_c, axis=0, keepdims=True)
    n = route_ref.shape[1]
    bid = lax.broadcasted_iota(jnp.int32, (CNT_ROWS, n), 0).astype(F32)
    base = jnp.sum(jnp.where(bid == route_ref[0:1, :], first_c * float(TM_MOE), 0.0), axis=0, keepdims=True)
    pos_ref[...] = (base + route_ref[3:4, :]).astype(jnp.int32)
    ti = jnp.minimum(lax.broadcasted_iota(jnp.int32, (CNT_ROWS, META_LANES), 1).astype(F32), nt - 1.0)
    mine = (ti >= first_c) & (ti < first_c + tiles_c)
    ea = jnp.sum(jnp.where(mine, eab_ref[:, 0:1], 0.0), axis=0, keepdims=True)
    eb = jnp.sum(jnp.where(mine, eab_ref[:, 1:2], 0.0), axis=0, keepdims=True)
    diag = (lax.broadcasted_iota(jnp.int32, (CNT_ROWS, META_LANES), 0)
            == lax.broadcasted_iota(jnp.int32, (CNT_ROWS, META_LANES), 1))
    first_l = jnp.sum(jnp.where(diag, first_c, 0.0), axis=0, keepdims=True)
    cnt_l = jnp.sum(jnp.where(diag, cnt[:, 0:1], 0.0), axis=0, keepdims=True)
    rows = [ea, eb, jnp.broadcast_to(nt, (1, META_LANES)), first_l, cnt_l, jnp.zeros((3, META_LANES), F32)]
    meta_ref[...] = jnp.concatenate(rows, axis=0).astype(jnp.int32)


def _plan(route, counts):
    n = route.shape[1]
    chunk = max(ck for ck in (2048, 1024, 512) if n % ck == 0)
    eab = jnp.zeros((CNT_ROWS, LANES), F32)
    eab = eab.at[:N_BUCKETS, 0].set(jnp.asarray(
        [GROUP_SIZE * (bk // N_PAIRS) + PAIR_LO[bk % N_PAIRS] for bk in range(N_BUCKETS)], F32))
    eab = eab.at[:N_BUCKETS, 1].set(jnp.asarray(
        [GROUP_SIZE * (bk // N_PAIRS) + PAIR_HI[bk % N_PAIRS] for bk in range(N_BUCKETS)], F32))
    const = lambda i: (0, 0)
    return pl.pallas_call(
        _plan_kernel,
        out_shape=[jax.ShapeDtypeStruct((1, n), jnp.int32),
                   jax.ShapeDtypeStruct((SUBLANES, META_LANES), jnp.int32)],
        grid=(n // chunk,),
        in_specs=[pl.BlockSpec((ROUTE_ROWS, chunk), lambda i: (0, i)), pl.BlockSpec((CNT_ROWS, LANES), const),
                  pl.BlockSpec((CNT_ROWS, LANES), const)],
        out_specs=[pl.BlockSpec((1, chunk), lambda i: (0, i)), pl.BlockSpec((SUBLANES, META_LANES), const)],
        compiler_params=_cparams(1), name="plan",
    )(route, counts, eab)


def _fill_padding(first_ref, cnt_ref, nt_ref, xs_ref, zbuf, sem, n_tiles_max, *, issue):
    if issue:
        zbuf[...] = jnp.zeros_like(zbuf)

    def fill(first_row, n_rows):
        cp = pltpu.make_async_copy(zbuf.at[pl.ds(0, n_rows)], xs_ref.at[pl.ds(first_row, n_rows)], sem)
        if issue:
            cp.start()
        else:
            cp.wait()

    def fill_tile(j, carry):
        fill(pl.multiple_of(j * TM_MOE, TM_MOE), TM_MOE)
        return carry

    lax.fori_loop(nt_ref[0], n_tiles_max, fill_tile, 0)

    for bk in range(N_BUCKETS):
        cnt = cnt_ref[bk]
        start = first_ref[bk] * TM_MOE + cnt
        pad = (-cnt) & (TM_MOE - 1)
        head = jnp.minimum((-cnt) & (SUBLANES - 1), pad)
        for j in range(SUBLANES - 1):
            @pl.when(j < head)
            def _(j=j, start=start):
                fill(start + j, 1)

        body = pad - head
        piece = TM_MOE // 2
        while piece >= SUBLANES:
            done = body & ~(2 * piece - 1)

            @pl.when((body & piece) != 0)
            def _(piece=piece, done=done, start=start, head=head):
                fill(pl.multiple_of(start + head + done, SUBLANES), piece)

            piece //= 2


def _sort_kernel(*refs, tm, d, step_ranges, n_tiles_max):
    n_src = len(step_ranges)
    pos_ref, first_ref, cnt_ref, nt_ref = refs[:4]
    srcs = refs[4:4 + 2 * n_src]
    xs_ref, buf, zbuf, sem, zsem = refs[4 + 2 * n_src:]
    n_steps = step_ranges[-1][1]
    i = pl.program_id(0)
    slot = i % 2

    def wait_rows(s):
        pltpu.make_async_copy(buf.at[s], xs_ref.at[pl.ds(0, tm)], sem.at[s]).wait()

    @pl.when(i == 0)
    def _():
        _fill_padding(first_ref, cnt_ref, nt_ref, xs_ref, zbuf, zsem, n_tiles_max, issue=True)

    @pl.when(i >= 2)
    def _():
        wait_rows(slot)

    for k, (s0, s1) in enumerate(step_ranges):
        @pl.when((i >= s0) & (i < s1))
        def _(k=k):
            tok_ref, route_ref = srcs[2 * k], srcs[2 * k + 1]
            cols = jnp.concatenate([route_ref[...], jnp.zeros((LANES - ROUTE_ROWS, tm), F32)], axis=0).T
            buf[slot, :, 0:d] = tok_ref[...]
            buf[slot, :, d:] = cols

    for t in range(tm):
        p = pos_ref[i * tm + t]
        pltpu.make_async_copy(buf.at[slot, pl.ds(t, 1)], xs_ref.at[pl.ds(p, 1)], sem.at[slot]).start()

    @pl.when(i == n_steps - 1)
    def _():
        wait_rows(slot)
        if n_steps >= 2:
            wait_rows(1 - slot)
        _fill_padding(first_ref, cnt_ref, nt_ref, xs_ref, zbuf, zsem, n_tiles_max, issue=False)


def _sort_rows(pos, first, cnt, nt, toks, routes, n_tiles_max):
    d = toks[0].shape[1]
    tm = TM_LAT
    step_ranges, s0 = [], 0
    for tok in toks:
        step_ranges.append((s0, s0 + tok.shape[0] // tm))
        s0 = step_ranges[-1][1]
    in_specs, args = [], []
    for (b0, b1), tok, rt in zip(step_ranges, toks, routes):
        blk = lambda i, *_, b0=b0, b1=b1: jnp.clip(i - b0, 0, b1 - b0 - 1)
        in_specs += [pl.BlockSpec((tm, d), lambda i, *_, blk=blk: (blk(i), 0)),
                     pl.BlockSpec((ROUTE_ROWS, tm),lambda i, *_, blk=blk: (0, blk(i)))]
        args += [tok, rt]
    grid_spec = pltpu.PrefetchScalarGridSpec(
        num_scalar_prefetch=4, grid=(s0,), in_specs=in_specs,
        out_specs=pl.BlockSpec(memory_space=pl.ANY),
        scratch_shapes=[pltpu.VMEM((2, tm, d + LANES), F32), pltpu.VMEM((TM_MOE, d + LANES), F32),
                        pltpu.SemaphoreType.DMA((2,)), pltpu.SemaphoreType.DMA])
    return pl.pallas_call(
        functools.partial(_sort_kernel, tm=tm, d=d, step_ranges=tuple(step_ranges), n_tiles_max=n_tiles_max),
        out_shape=jax.ShapeDtypeStruct((n_tiles_max * TM_MOE, d + LANES), F32), grid_spec=grid_spec,
        compiler_params=_cparams(1), name="sort_rows",
    )(pos, first, cnt, nt, *args)


def _moe_layer(toks, routes, counts, layer, wg, wu, wd):
    d = toks[0].shape[1]
    n = sum(t.shape[0] for t in toks)
    n_tiles_max = n // TM_MOE + N_BUCKETS
    assert n_tiles_max <= META_LANES
    route = routes[0] if len(routes) == 1 else jnp.concatenate(routes, axis=1)
    pos, meta = _plan(route, counts)
    pos = pos.reshape(n)
    xs = _sort_rows(pos, meta[3, :N_BUCKETS], meta[4, :N_BUCKETS], meta[2, :1], toks, routes, n_tiles_max)
    ys = _moe(meta[0, :n_tiles_max], meta[1, :n_tiles_max], meta[2, :1], xs, layer, wg, wu, wd)
    return jnp.concatenate([pos, jnp.zeros((TM_LAT,), jnp.int32)]), ys


def _final_kernel(pos_ref, x_ref, ys_ref, mod_ref, gpost_ref, o_ref, fbuf, sem, *, tm):
    def add_residual(f):
        o_ref[...] = x_ref[...] + _rms(f, gpost_ref[...] * mod_ref[5:6, :])

    _with_gathered_tile(pos_ref, ys_ref, fbuf, sem, 0, tm, add_residual)


def _final(x, pos, ys, mods, layer, gpost, *, tm, rows_per_batch):
    n, d = x.shape
    tiles_per_batch = rows_per_batch // tm
    row = lambda i, *_: (i, 0)
    grid_spec = pltpu.PrefetchScalarGridSpec(
        num_scalar_prefetch=1, grid=(n // tm,),
        in_specs=[pl.BlockSpec((tm, d), row), pl.BlockSpec(memory_space=pl.ANY),
                  pl.BlockSpec((None, None, 6, d), lambda i, *_: (layer, i // tiles_per_batch, 0, 0)),
                  pl.BlockSpec((1, d), lambda i, *_: (0, 0))],
        out_specs=pl.BlockSpec((tm, d), row),
        scratch_shapes=[pltpu.VMEM((2, tm, d), F32), pltpu.SemaphoreType.DMA((2,))])
    return pl.pallas_call(
        functools.partial(_final_kernel, tm=tm), out_shape=jax.ShapeDtypeStruct((n, d), F32),
        grid_spec=grid_spec, compiler_params=_cparams(1), name="final",
    )(pos, x, ys, mods, gpost)


def kernel(x, c, ctx, c_ctx, w_ada, b_ada, g_mix_pre, g_mix_post, g_ffn_pre, g_ffn_post, conv_w_in, conv_b_in, conv_w_dw, conv_b_dw, conv_ln_g, conv_ln_b, conv_w_out, conv_b_out, attn_w_qkv, attn_w_o, attn_rpb, router_w, router_b, exp_w_gate, exp_w_up, exp_w_down):
    b, s, d = x.shape
    l = ctx.shape[1]
    n, nc = b * s, b * l
    depth = w_ada.shape[0]
    rows = s // GRID_W
    assert depth == 2 and b <= CTX_MOD_ROW and s % TM_LAT == 0 and l == TM_CTX
    assert rows % Q_ROWS == 0 and rows >= SLAB_ROWS + Q_ROWS
    tm_lat = TM_LAT

    cc = jnp.zeros((MOD_ROWS, d), F32).at[:b].set(c).at[CTX_MOD_ROW].set(c_ctx)
    mods = _ada(cc, w_ada, b_ada).reshape(depth, MOD_ROWS, 6, d)

    row = lambda v: v.reshape(1, d)
    rwt = router_w.T
    rwh = rwt.astype(BF16)
    rwl = (rwt - rwh.astype(F32)).astype(BF16)
    rb = router_b.reshape(N_EXPERTS, 1)
    no_counts = jnp.zeros((CNT_ROWS, LANES), F32)

    xl = x.reshape(n, d)
    xc = ctx.reshape(nc, d)

    w_in = conv_w_in[0].astype(BF16)
    b_in = conv_b_in[0].reshape(1, 2 * d)
    conv = (conv_w_dw[0], row(conv_b_dw[0]), row(conv_ln_g[0]), row(conv_ln_b[0]),
            conv_w_out[0].astype(BF16), row(conv_b_out[0]))
    tailp = (row(g_mix_post[0]), row(g_ffn_pre[0]), rwh, rwl, rb)
    (u,) = _proj(xl, mods, 0, row(g_mix_pre[0]), w_in, tm=tm_lat, rows_per_batch=s, ctx=False,
                 bias=b_in, name="conv_in")
    (uc,) = _proj(xc, mods, 0, row(g_mix_pre[0]), w_in, tm=TM_CTX, rows_per_batch=l, ctx=True,
                  bias=b_in, name="conv_in_ctx")
    x1, tok, route, counts = _conv_tail(u, xl, mods, 0, conv, tailp, no_counts, tm=tm_lat, rows_per_batch=s,
                                        ctx=False, name="conv_tail")
    xc1, tokc, routec, counts = _conv_tail(uc, xc, mods, 0, conv, tailp, counts, tm=TM_CTX, rows_per_batch=l,
                                           ctx=True, name="conv_tail_ctx")
    pos0, ys0 = _moe_layer([tok, tokc], [route, routec], counts, 0, exp_w_gate, exp_w_up, exp_w_down)

    w_qkv = attn_w_qkv[0].astype(BF16)
    x2, q, k, v = _proj(x1, mods, 1, row(g_mix_pre[1]), w_qkv, tm=tm_lat, rows_per_batch=s, ctx=False,
                        resid=(pos0, ys0, row(g_ffn_post[0]), 0), n_out=3, name="qkv")
    _, kc, vc = _proj(xc1, mods, 1, row(g_mix_pre[1]), w_qkv[:, d:], tm=TM_CTX, rows_per_batch=l, ctx=True,
                      resid=(pos0, ys0, row(g_ffn_post[0]), n), n_out=2, name="kv_ctx")
    c2 = _bias_pairs(attn_rpb[0])
    o = _natten(q.reshape(b, s, d), k.reshape(b, s, d), v.reshape(b, s, d),
                kc.reshape(b, l, d), vc.reshape(b, l, d), c2)
    tailp1 = (row(g_mix_post[1]), row(g_ffn_pre[1]), rwh, rwl, rb)
    x3, tok1, route1, counts1 = _attn_tail(o.reshape(n, d), x2, mods, 1, attn_w_o[0].astype(BF16), tailp1,
                                           no_counts, tm=tm_lat, rows_per_batch=s, name="attn_tail")
    pos1, ys1 = _moe_layer([tok1], [route1], counts1, 1, exp_w_gate, exp_w_up, exp_w_down)
    out = _final(x3, pos1, ys1, mods, 1, row(g_ffn_post[1]), tm=tm_lat, rows_per_batch=s)
    return out.reshape(b, s, d)
```

```python
import functools

import jax
import jax.numpy as jnp
from jax import lax
from jax.experimental import pallas as pl
from jax.experimental.pallas import tpu as pltpu

F32 = jnp.float32
BF16 = jnp.bfloat16

LANES = 128
SUBLANES = 8

EPS = 1e-6
GRID_W = 64
WIN_H = 8
WIN_W = 16
N_HEADS = 16
HEAD_DIM = 64
N_EXPERTS = 16
N_GROUPS = 4
GROUP_SIZE = N_EXPERTS // N_GROUPS
N_PAIRS = 6
N_BUCKETS = N_GROUPS * N_PAIRS
PAIR_LO = (0, 0, 0, 1, 1, 2)
PAIR_HI = (1, 2, 3, 2, 3, 3)
CONV_WIDTH = 31
CONV_PAD = (CONV_WIDTH - 1) // 2
HALO = 16
MASK_BIAS = -1e30
Q_ROWS = 4
SLAB_ROWS = Q_ROWS + WIN_H
MOD_ROWS = 16
CTX_MOD_ROW = 8
ROUTE_ROWS = SUBLANES
CNT_ROWS = 32
META_LANES = 256

TM_LAT = 512
TM_WIDE = 1024
TM_CTX = 256
TM_MOE = 512
VMEM_LIMIT = 56 * 1024 * 1024

_NT = (((1,), (1,)), ((), ()))


def _cparams(n_axes):
    return pltpu.CompilerParams(dimension_semantics=("arbitrary",) * n_axes,
                                vmem_limit_bytes=VMEM_LIMIT)


def _sigmoid(v):
    return 1.0 / (1.0 + jnp.exp(-v))


def _rms(v, g):
    return v * lax.rsqrt(jnp.mean(v * v, axis=-1, keepdims=True) + EPS) * g


def _split_bf16(v):
    hi = v.astype(BF16)
    lo = (v - hi.astype(F32)).astype(BF16)
    return hi, lo


def _ada_kernel(c_ref, w_ref, b_ref, o_ref):
    a = c_ref[...]
    a = a * _sigmoid(a)
    a_hi, a_lo = _split_bf16(a)
    w_hi, w_lo = _split_bf16(w_ref[...])
    acc = jnp.dot(a_hi, w_hi, preferred_element_type=F32)
    acc += jnp.dot(a_lo, w_hi, preferred_element_type=F32)
    acc += jnp.dot(a_hi, w_lo, preferred_element_type=F32)
    o_ref[...] = acc + b_ref[...]


def _ada(cc, w_ada, b_ada):
    depth, d, n = w_ada.shape
    bn = 1024
    return pl.pallas_call(
        _ada_kernel,
        out_shape=jax.ShapeDtypeStruct((depth, MOD_ROWS, n), F32),
        grid=(depth, n // bn),
        in_specs=[pl.BlockSpec((MOD_ROWS, d), lambda l, j: (0, 0)),
                  pl.BlockSpec((None, d, bn), lambda l, j: (l, 0, j)),
                  pl.BlockSpec((None, 1, bn), lambda l, j: (l, 0, j))],
        out_specs=pl.BlockSpec((None, MOD_ROWS, bn), lambda l, j: (l, 0, j)),
        compiler_params=_cparams(2),
        name="ada",
    )(cc, w_ada, b_ada.reshape(depth, 1, n))


def _start_row_gather(pos_ref, src_ref, buf, sem, first_row, slot, tm):
    for t in range(tm):
        p = pos_ref[first_row + t]
        pltpu.make_async_copy(src_ref.at[pl.ds(p, 1)], buf.at[slot, pl.ds(t, 1)], sem.at[slot]).start()


def _wait_row_gather(src_ref, buf, sem, slot, tm):
    pltpu.make_async_copy(src_ref.at[pl.ds(0, tm)], buf.at[slot], sem.at[slot]).wait()


def _with_gathered_tile(pos_ref, src_ref, buf, sem, row0, tm, consume):
    i = pl.program_id(0)

    @pl.when(i == 0)
    def _():
        _start_row_gather(pos_ref, src_ref, buf, sem, row0, 0, tm)

    for slot in range(2):
        @pl.when(i % 2 == slot)
        def _(slot=slot):
            _wait_row_gather(src_ref, buf, sem, slot, tm)
            _start_row_gather(pos_ref, src_ref, buf, sem, row0 + (i + 1) * tm, 1 - slot, tm)
            consume(buf[slot])

            @pl.when(i == pl.num_programs(0) - 1)
            def _():
                _wait_row_gather(src_ref, buf, sem, 1 - slot, tm)


def _proj_kernel(*refs, fuse_resid, glu, d, tm, f_row0):
    refs = list(refs)
    if fuse_resid:
        pos_ref = refs.pop(0)
    x_ref = refs.pop(0)
    if fuse_resid:
        ys_ref, modp_ref, gpost_ref = refs.pop(0), refs.pop(0), refs.pop(0)
    mod_ref, gpre_ref, w_ref = refs.pop(0), refs.pop(0), refs.pop(0)
    if glu:
        b_ref = refs.pop(0)
    if fuse_resid:
        xn_ref = refs.pop(0)
        fbuf, sem = refs.pop(-2), refs.pop(-1)

    def project(f):
        x = x_ref[...]
        if fuse_resid:
            x = x + _rms(f, gpost_ref[...] * modp_ref[5:6, :])
            xn_ref[...] = x
        h = _rms(x, gpre_ref[...] * (1.0 + mod_ref[1:2, :])) + mod_ref[0:1, :]
        z = jnp.dot(h.astype(BF16), w_ref[...], preferred_element_type=F32)
        if glu:
            z = z + b_ref[...]
            refs[0][...] = (z[:, :d] * _sigmoid(z[:, d:])).astype(BF16)
        else:
            n_out = len(refs)
            for j, o_ref in enumerate(refs):
                part = z[:, j * d:(j + 1) * d]
                if n_out == 3 and j == 0:
                    part = part * (HEAD_DIM ** -0.5)
                o_ref[...] = part.astype(BF16)

    if fuse_resid:
        _with_gathered_tile(pos_ref, ys_ref, fbuf, sem, f_row0, tm, project)
    else:
        project(None)


def _proj(x, mods, layer, gpre, w, *, tm, rows_per_batch, ctx, bias=None, resid=None, n_out=1, name):
    n, d = x.shape
    tiles_per_batch = rows_per_batch // tm
    if ctx:
        mod_idx = lambda i, *_: (layer, CTX_MOD_ROW, 0, 0)
    else:
        mod_idx = lambda i, *_: (layer, i // tiles_per_batch, 0, 0)
    row = lambda i, *_: (i, 0)
    const = lambda i, *_: (0, 0)
    args, specs, scratch, f_row0 = [x], [pl.BlockSpec((tm, d), row)], [], 0
    if resid is not None:
        pos, ys, gpost, f_row0 = resid
        if ctx:
            modp_idx = lambda i, *_: (layer - 1, CTX_MOD_ROW, 0, 0)
        else:
            modp_idx = lambda i, *_: (layer - 1, i // tiles_per_batch, 0, 0)
        args = [pos] + args + [ys, mods, gpost]
        specs += [pl.BlockSpec(memory_space=pl.ANY), pl.BlockSpec((None, None, 6, d), modp_idx),
                  pl.BlockSpec((1, d), const)]
        scratch = [pltpu.VMEM((2, tm, d), F32), pltpu.SemaphoreType.DMA((2,))]
    args += [mods, gpre, w]
    specs += [pl.BlockSpec((None, None, 6, d), mod_idx), pl.BlockSpec((1, d), const),
              pl.BlockSpec(w.shape, const)]
    glu = bias is not None
    if glu:
        args.append(bias)
        specs.append(pl.BlockSpec(bias.shape, const))
    out_shape, out_specs = [], []
    if resid is not None:
        out_shape.append(jax.ShapeDtypeStruct((n, d), F32))
        out_specs.append(pl.BlockSpec((tm, d), row))
    for _ in range(n_out):
        out_shape.append(jax.ShapeDtypeStruct((n, d), BF16))
        out_specs.append(pl.BlockSpec((tm, d), row))
    grid_spec = pltpu.PrefetchScalarGridSpec(
        num_scalar_prefetch=0 if resid is None else 1, grid=(n // tm,),
        in_specs=specs, out_specs=out_specs, scratch_shapes=scratch)
    return pl.pallas_call(
        functools.partial(_proj_kernel, fuse_resid=resid is not None, glu=glu, d=d, tm=tm, f_row0=f_row0),
        out_shape=out_shape, grid_spec=grid_spec, compiler_params=_cparams(1), name=name,
    )(*args)


def _route(lg, rb):
    s = _sigmoid(lg)
    ssel = s + rb
    sel = [ssel[e:e + 1, :] for e in range(N_EXPERTS)]
    raw = [s[e:e + 1, :] for e in range(N_EXPERTS)]
    scores = []
    for g in range(N_GROUPS):
        v0, v1, v2, v3 = sel[4 * g:4 * g + 4]
        a, b = jnp.maximum(v0, v1), jnp.minimum(v0, v1)
        c, e = jnp.maximum(v2, v3), jnp.minimum(v2, v3)
        scores.append(jnp.maximum(a, c) + jnp.maximum(jnp.minimum(a, c), jnp.maximum(b, e)))
    best = scores[0]
    gi = jnp.zeros_like(best)
    for g in range(1, N_GROUPS):
        better = scores[g] > best
        gi = jnp.where(better, float(g), gi)
        best = jnp.where(better, scores[g], best)

    def pick(rows, j):
        return jnp.where(gi == 0.0, rows[j],
                         jnp.where(gi == 1.0, rows[4 + j], jnp.where(gi == 2.0, rows[8 + j], rows[12 + j])))

    v = [pick(sel, j) for j in range(GROUP_SIZE)]
    r = [pick(raw, j) for j in range(GROUP_SIZE)]
    m = []
    for j in range(GROUP_SIZE):
        ahead = jnp.zeros_like(best)
        for k in range(GROUP_SIZE):
            if k == j:
                continue
            beats = (v[k] >= v[j]) if k < j else (v[k] > v[j])
            ahead = ahead + jnp.where(beats, 1.0, 0.0)
        m.append(ahead < 2.0)
    pair = jnp.where(m[0], jnp.where(m[1], 0.0, jnp.where(m[2], 1.0, 2.0)),
                     jnp.where(m[1], jnp.where(m[2], 3.0, 4.0), 5.0))
    sa = jnp.where(m[0], r[0], jnp.where(m[1], r[1], r[2]))
    sb = jnp.where(m[3], r[3], jnp.where(m[2], r[2], r[1]))
    den = sa + sb
    bucket = gi * float(N_PAIRS) + pair
    zeros = jnp.zeros((5, lg.shape[1]), F32)
    return jnp.concatenate([bucket, sa / den, sb / den, zeros], axis=0)


def _tail(y, x, mod_ref, gpost_ref, gffn_ref, rwh_ref, rwl_ref, rb_ref, tri_ref, cin_ref,
          x1_ref, tok_ref, route_ref, cnt_ref):
    x1 = x + _rms(y, gpost_ref[...] * mod_ref[2:3, :])
    x1_ref[...] = x1
    t = _rms(x1, gffn_ref[...] * (1.0 + mod_ref[4:5, :])) + mod_ref[3:4, :]
    tok_ref[...] = t
    t_hi, t_lo = _split_bf16(t)
    lg = lax.dot_general(rwh_ref[...], t_hi, _NT, preferred_element_type=F32)
    lg += lax.dot_general(rwl_ref[...], t_hi, _NT, preferred_element_type=F32)
    lg += lax.dot_general(rwh_ref[...], t_lo, _NT, preferred_element_type=F32)
    rt = _route(lg, rb_ref[...])
    tm = rt.shape[1]

    @pl.when(pl.program_id(0) == 0)
    def _():
        cnt_ref[...] = cin_ref[...]

    bid = lax.broadcasted_iota(jnp.int32, (CNT_ROWS, tm), 0).astype(F32)
    onehot = jnp.where(bid == rt[0:1, :], 1.0, 0.0)
    cum = jnp.dot(onehot.astype(BF16), tri_ref[...], preferred_element_type=F32)
    rank = jnp.sum(onehot * (cum + cnt_ref[:, 0:1]), axis=0, keepdims=True) - 1.0
    cnt_ref[...] = cnt_ref[...] + cum[:, tm - 1:tm]
    route_ref[...] = jnp.concatenate([rt[0:3, :], rank, jnp.zeros((4, tm), F32)], axis=0)


def _conv_tail_kernel(u_ref, up_ref, un_ref, x_ref, mod_ref, wdw_ref, bdw_ref, lng_ref, lnb_ref,
                      wout_ref, bout_ref, gpost_ref, gffn_ref, rwh_ref, rwl_ref, rb_ref, tri_ref, cin_ref,
                      x1_ref, tok_ref, route_ref, cnt_ref, ubuf, ush, cacc, wb, *, tm, tiles_per_seq, chunk):
    t = pl.program_id(0) % tiles_per_seq
    prev_ok = jnp.where(t > 0, 1.0, 0.0)
    next_ok = jnp.where(t < tiles_per_seq - 1, 1.0, 0.0)
    ubuf[0:HALO, :] = up_ref[...].astype(F32) * prev_ok
    ubuf[HALO:HALO + tm, :] = u_ref[...].astype(F32)
    ubuf[HALO + tm:, :] = un_ref[...].astype(F32) * next_ok

    sub = SUBLANES
    base = HALO - CONV_PAD
    span = tm + sub * ((CONV_WIDTH - 1 + base) // sub)
    for s in range(1, sub):
        ush[s - 1, 0:span, :] = ubuf[s:s + span, :]

    @pl.when(pl.program_id(0) == 0)
    def _():
        for k in range(CONV_WIDTH):
            wb[k] = jnp.broadcast_to(wdw_ref[k:k + 1, :], (sub, wb.shape[2]))

    n_lane_blocks = wb.shape[2] // LANES

    def body(c, carry):
        r0 = pl.multiple_of(c * chunk, chunk)
        for lb in range(n_lane_blocks):
            lanes = slice(lb * LANES, (lb + 1) * LANES)
            acc = None
            for s in range(sub):
                if s == 0:
                    win = ubuf[pl.ds(r0, chunk + span - tm), lanes]
                else:
                    win = ush[s - 1, pl.ds(r0, chunk + span - tm), lanes]
                win = win.reshape((chunk + span - tm) // sub, sub, LANES)
                for q in range((span - tm) // sub + 1):
                    k = sub * q + s - base
                    if 0 <= k < CONV_WIDTH:
                        term = win[q:q + chunk // sub] * wb[k, :, lanes][None]
                        acc = term if acc is None else acc + term
            cacc[pl.ds(r0, chunk), lanes] = acc.reshape(chunk, LANES)
        return carry

    lax.fori_loop(0, tm // chunk, body, 0)

    y = cacc[...] + bdw_ref[...]
    mu = jnp.mean(y, axis=-1, keepdims=True)
    yc = y - mu
    var = jnp.mean(yc * yc, axis=-1, keepdims=True)
    yn = yc * lax.rsqrt(var + EPS) * lng_ref[...] + lnb_ref[...]
    act = yn * _sigmoid(yn)
    z = jnp.dot(act.astype(BF16), wout_ref[...], preferred_element_type=F32) + bout_ref[...]
    _tail(z, x_ref[...], mod_ref, gpost_ref, gffn_ref, rwh_ref, rwl_ref, rb_ref, tri_ref, cin_ref,
          x1_ref, tok_ref, route_ref, cnt_ref)


def _tail_common_specs(d, tm):
    const = lambda i: (0, 0)
    return [pl.BlockSpec((1, d), const), pl.BlockSpec((1, d), const),
            pl.BlockSpec((N_EXPERTS, d), const), pl.BlockSpec((N_EXPERTS, d), const),
            pl.BlockSpec((N_EXPERTS, 1), const), pl.BlockSpec((tm, tm), const),
            pl.BlockSpec((CNT_ROWS, LANES), const)]


def _tail_common_args(tailp, counts, tm):
    gpost, gffn, rwh, rwl, rb = tailp
    j = lax.broadcasted_iota(jnp.int32, (tm, tm), 0)
    t = lax.broadcasted_iota(jnp.int32, (tm, tm), 1)
    return [gpost, gffn, rwh, rwl, rb, (j <= t).astype(BF16), counts]


def _tail_outs(n, d, tm):
    out_shape = [jax.ShapeDtypeStruct((n, d), F32), jax.ShapeDtypeStruct((n, d), F32),
                 jax.ShapeDtypeStruct((ROUTE_ROWS, n), F32), jax.ShapeDtypeStruct((CNT_ROWS, LANES), F32)]
    out_specs = [pl.BlockSpec((tm, d), lambda i: (i, 0)), pl.BlockSpec((tm, d), lambda i: (i, 0)),
                 pl.BlockSpec((ROUTE_ROWS, tm),lambda i: (0, i)), pl.BlockSpec((CNT_ROWS, LANES), lambda i: (0, 0))]
    return out_shape, out_specs


def _conv_tail(u, x, mods, layer, conv, tailp, counts, *, tm, rows_per_batch, ctx, name):
    n, d = x.shape
    wdw, bdw, lng, lnb, wout, bout = conv
    tiles_per_seq = rows_per_batch // tm
    hb = tm // HALO
    n_hb = n // HALO
    if ctx:
        mod_idx = lambda i: (layer, CTX_MOD_ROW, 0, 0)
    else:
        mod_idx = lambda i: (layer, i // tiles_per_seq, 0, 0)
    row = lambda i: (i, 0)
    const = lambda i: (0, 0)
    specs = [pl.BlockSpec((tm, d), row),
             pl.BlockSpec((HALO, d), lambda i: (jnp.maximum(i * hb - 1, 0), 0)),
             pl.BlockSpec((HALO, d), lambda i: (jnp.minimum((i + 1) * hb, n_hb - 1), 0)),
             pl.BlockSpec((tm, d), row),
             pl.BlockSpec((None, None, 6, d), mod_idx),
             pl.BlockSpec(wdw.shape, const), pl.BlockSpec((1, d), const),
             pl.BlockSpec((1, d), const), pl.BlockSpec((1, d), const),
             pl.BlockSpec(wout.shape, const), pl.BlockSpec((1, d), const)]
    specs += _tail_common_specs(d, tm)
    out_shape, out_specs = _tail_outs(n, d, tm)
    chunk = 64
    return pl.pallas_call(
        functools.partial(_conv_tail_kernel, tm=tm, tiles_per_seq=tiles_per_seq, chunk=chunk),
        out_shape=out_shape, grid=(n // tm,), in_specs=specs, out_specs=out_specs,
        scratch_shapes=[pltpu.VMEM((tm + 2 * HALO, d), F32), pltpu.VMEM((SUBLANES - 1, tm + 2 * HALO, d), F32),
                        pltpu.VMEM((tm, d), F32), pltpu.VMEM((CONV_WIDTH, SUBLANES, d), F32)],
        compiler_params=_cparams(1), name=name,
    )(u, u, u, x, mods, wdw, bdw, lng, lnb, wout, bout, *_tail_common_args(tailp, counts, tm))


def _attn_tail_kernel(o_ref, x_ref, mod_ref, wo_ref, gpost_ref, gffn_ref, rwh_ref, rwl_ref, rb_ref, tri_ref,
                      cin_ref, x1_ref, tok_ref, route_ref, cnt_ref):
    y = jnp.dot(o_ref[...], wo_ref[...], preferred_element_type=F32)
    _tail(y, x_ref[...], mod_ref, gpost_ref, gffn_ref, rwh_ref, rwl_ref, rb_ref, tri_ref, cin_ref,
          x1_ref, tok_ref, route_ref, cnt_ref)


def _attn_tail(o, x, mods, layer, wo, tailp, counts, *, tm, rows_per_batch, name):
    n, d = x.shape
    tiles_per_batch = rows_per_batch // tm
    mod_idx = lambda i: (layer, i // tiles_per_batch, 0, 0)
    row = lambda i: (i, 0)
    const = lambda i: (0, 0)
    specs = [pl.BlockSpec((tm, d), row), pl.BlockSpec((tm, d), row),
             pl.BlockSpec((None, None, 6, d), mod_idx), pl.BlockSpec(wo.shape, const)]
    specs += _tail_common_specs(d, tm)
    out_shape, out_specs = _tail_outs(n, d, tm)
    return pl.pallas_call(
        _attn_tail_kernel, out_shape=out_shape, grid=(n // tm,), in_specs=specs, out_specs=out_specs,
        compiler_params=_cparams(1), name=name,
    )(o, x, mods, wo, *_tail_common_args(tailp, counts, tm))


def _window_plan(rows):
    n_blocks = rows // Q_ROWS
    plan = []
    for blk in (0, 1, n_blocks - 1):
        slab = min(max(Q_ROWS * blk - WIN_H // 2, 0), rows - SLAB_ROWS)
        per_row = []
        for rl in range(Q_ROWS):
            r = Q_ROWS * blk + rl
            r0 = min(max(r - WIN_H // 2, 0), rows - WIN_H)
            pairs = []
            for kp in range(SLAB_ROWS // 2):
                kr = slab + 2 * kp
                inside = (r0 <= kr < r0 + WIN_H, r0 <= kr + 1 < r0 + WIN_H)
                pairs.append((min(max(kr - r + WIN_H, 0), 2 * WIN_H - 1), inside))
            per_row.append(pairs)
        plan.append(per_row)
    return plan


def _natten_kernel(q_ref, k_ref, v_ref, kc_ref, vc_ref, c2_ref, o_ref, bias, *, rows):
    hw = 2 * HEAD_DIM
    nq = Q_ROWS * GRID_W
    n_slab = SLAB_ROWS * GRID_W
    n_blocks = rows // Q_ROWS
    lane = lax.broadcasted_iota(jnp.int32, (1, hw), 1)
    low = lane < HEAD_DIM

    @pl.when(pl.program_id(1) == 0)
    def _():
        for var, per_row in enumerate(_window_plan(rows)):
            for h in range(2):
                for rl, pairs in enumerate(per_row):
                    for kp, (entry, inside) in enumerate(pairs):
                        if inside[0] or inside[1]:
                            blk = c2_ref[h, entry]
                            if not inside[0]:
                                blk = jnp.where(low, MASK_BIAS, blk)
                            if not inside[1]:
                                blk = jnp.where(low, blk, MASK_BIAS)
                        else:
                            blk = jnp.full((GRID_W, hw), MASK_BIAS, F32)
                        bias[var, h * nq + rl * GRID_W:h * nq + (rl + 1) * GRID_W, kp * hw:(kp + 1) * hw] = blk

    kc = kc_ref[...]
    vc = vc_ref[...]

    def body(blk, carry):
        slab = jnp.clip(Q_ROWS * blk - WIN_H // 2, 0, rows - SLAB_ROWS)
        var = jnp.where(blk == 0, 0, jnp.where(blk == n_blocks - 1, 2, 1))
        q0 = pl.multiple_of(blk * nq, nq)
        k0 = pl.multiple_of(slab * GRID_W, GRID_W)
        q2 = q_ref[pl.ds(q0, nq), :]
        zero = jnp.zeros_like(q2)
        qm = jnp.concatenate([jnp.where(low, q2, zero), jnp.where(low, zero, q2)], axis=0)
        kw = k_ref[pl.ds(k0, n_slab), :]
        vw = v_ref[pl.ds(k0, n_slab), :]
        s_loc = lax.dot_general(qm, kw, _NT, preferred_element_type=F32) + bias[var]
        s_ctx = lax.dot_general(qm, kc, _NT, preferred_element_type=F32)
        m = jnp.maximum(jnp.max(s_loc, axis=-1, keepdims=True), jnp.max(s_ctx, axis=-1, keepdims=True))
        p_loc = jnp.exp(s_loc - m)
        p_ctx = jnp.exp(s_ctx - m)
        den = jnp.sum(p_loc, axis=-1, keepdims=True) + jnp.sum(p_ctx, axis=-1, keepdims=True)
        acc = jnp.dot(p_loc.astype(BF16), vw, preferred_element_type=F32)
        acc += jnp.dot(p_ctx.astype(BF16), vc, preferred_element_type=F32)
        acc = acc / den
        o_ref[pl.ds(q0, nq), :] = jnp.where(low, acc[:nq], acc[nq:]).astype(BF16)
        return carry

    lax.fori_loop(0, n_blocks, body, 0, unroll=16)


def _natten(q, k, v, kc, vc, c2):
    b, s, d = q.shape
    l = kc.shape[1]
    rows = s // GRID_W
    hw = 2 * HEAD_DIM
    lat = pl.BlockSpec((None, s, hw), lambda hp, bi: (bi, 0, hp))
    cspec = pl.BlockSpec((None, l, hw), lambda hp, bi: (bi, 0, hp))
    return pl.pallas_call(
        functools.partial(_natten_kernel, rows=rows),
        out_shape=jax.ShapeDtypeStruct((b, s, d), BF16),
        grid=(d // hw, b),
        in_specs=[lat, lat, lat, cspec, cspec,
                  pl.BlockSpec((2, 2 * WIN_H, GRID_W, hw), lambda hp, bi: (hp, 0, 0, 0))],
        out_specs=lat,
        scratch_shapes=[pltpu.VMEM((3, 2 * Q_ROWS * GRID_W, SLAB_ROWS * GRID_W), F32)],
        compiler_params=_cparams(2), name="natten",
    )(q, k, v, kc, vc, c2)


def _bias_kernel(r_ref, oh_ref, mask_ref, o_ref):
    r = r_ref[...]
    r1 = r.astype(BF16)
    r2 = (r - r1.astype(F32)).astype(BF16)
    r3 = (r - r1.astype(F32) - r2.astype(F32)).astype(BF16)
    oh = oh_ref[...]
    acc = jnp.dot(r1, oh, preferred_element_type=F32)
    acc += jnp.dot(r2, oh, preferred_element_type=F32)
    acc += jnp.dot(r3, oh, preferred_element_type=F32)
    o_ref[...] = acc + mask_ref[...]


def _bias_pairs(rpb):
    n_h, n_dr, n_dc = rpb.shape
    col = jnp.arange(GRID_W)[:, None]
    kcol = jnp.arange(GRID_W)[None, :]
    cs = jnp.clip(col - WIN_W // 2, 0, GRID_W - WIN_W)
    valid = (kcol >= cs) & (kcol < cs + WIN_W)
    dc = jnp.arange(32)[:, None, None]
    onehot = ((kcol - col + WIN_W - 1)[None] == dc) & valid[None]
    onehot = onehot.reshape(32, GRID_W * GRID_W).astype(BF16)
    maskadd = jnp.where(valid, 0.0, MASK_BIAS).reshape(1, GRID_W * GRID_W).astype(F32)
    rp = jnp.zeros((n_h, 16, 32), F32).at[:, :n_dr, :n_dc].set(rpb)
    flat = pl.pallas_call(
        _bias_kernel,
        out_shape=jax.ShapeDtypeStruct((n_h, 16, GRID_W * GRID_W), F32),
        grid=(n_h,),
        in_specs=[pl.BlockSpec((None, 16, 32), lambda h: (h, 0, 0)),
                  pl.BlockSpec((32, GRID_W * GRID_W), lambda h: (0, 0)),
                  pl.BlockSpec((1, GRID_W * GRID_W), lambda h: (0, 0))],
        out_specs=pl.BlockSpec((None, 16, GRID_W * GRID_W), lambda h: (h, 0, 0)),
        compiler_params=_cparams(1), name="bias_expand",
    )(rp, onehot, maskadd)
    tab = flat[:, :n_dr].reshape(n_h, n_dr, GRID_W, GRID_W)
    masked = jnp.full((n_h, 1, GRID_W, GRID_W), MASK_BIAS, F32)
    ext = jnp.concatenate([masked, tab, masked], axis=1)
    return jnp.concatenate([ext[:, :-1], ext[:, 1:]], axis=-1)


def _moe_kernel(ea_ref, eb_ref, nt_ref, xs_ref, wga_ref, wua_ref, wda_ref, wgb_ref, wub_ref, wdb_ref,
                y_ref, cga, cua, cda, cgb, cub, cdb, *, d):
    i = pl.program_id(0)
    prev = jnp.maximum(i - 1, 0)
    fresh = (i == 0) | (ea_ref[i] != ea_ref[prev]) | (eb_ref[i] != eb_ref[prev])

    @pl.when(fresh)
    def _():
        for src, dst in ((wga_ref, cga), (wua_ref, cua), (wda_ref, cda),
                         (wgb_ref, cgb), (wub_ref, cub), (wdb_ref, cdb)):
            dst[...] = src[...].astype(BF16)

    @pl.when(i < nt_ref[0])
    def _():
        xt = xs_ref[:, 0:d].astype(BF16)

        def expert(wg_ref, wu_ref, wd_ref):
            g = jnp.dot(xt, wg_ref[...], preferred_element_type=F32)
            u = jnp.dot(xt, wu_ref[...], preferred_element_type=F32)
            h = (g * _sigmoid(g)) * u
            return jnp.dot(h.astype(BF16), wd_ref[...], preferred_element_type=F32)

        ya = expert(cga, cua, cda)
        yb = expert(cgb, cub, cdb)
        y_ref[...] = xs_ref[:, d + 1:d + 2] * ya + xs_ref[:, d + 2:d + 3] * yb

    @pl.when(i >= nt_ref[0])
    def _():
        y_ref[...] = jnp.zeros_like(y_ref)


def _moe(ea, eb, nt, xs, layer, wg, wu, wd):
    npad, wide = xs.shape
    d = wide - LANES
    de = wg.shape[3]
    sel_a = lambda i, ea, eb, nt: (layer, ea[i], 0, 0)
    sel_b = lambda i, ea, eb, nt: (layer, eb[i], 0, 0)
    up = (None, None, d, de)
    down = (None, None, de, d)
    grid_spec = pltpu.PrefetchScalarGridSpec(
        num_scalar_prefetch=3, grid=(npad // TM_MOE,),
        in_specs=[pl.BlockSpec((TM_MOE, wide), lambda i, ea, eb, nt: (jnp.minimum(i, nt[0] - 1), 0)),
                  pl.BlockSpec(up, sel_a), pl.BlockSpec(up, sel_a), pl.BlockSpec(down, sel_a),
                  pl.BlockSpec(up, sel_b), pl.BlockSpec(up, sel_b), pl.BlockSpec(down, sel_b)],
        out_specs=pl.BlockSpec((TM_MOE, d), lambda i, ea, eb, nt: (i, 0)),
        scratch_shapes=[pltpu.VMEM((d, de), BF16), pltpu.VMEM((d, de), BF16), pltpu.VMEM((de, d), BF16),
                        pltpu.VMEM((d, de), BF16), pltpu.VMEM((d, de), BF16), pltpu.VMEM((de, d), BF16)])
    return pl.pallas_call(
        functools.partial(_moe_kernel, d=d), out_shape=jax.ShapeDtypeStruct((npad, d), F32),
        grid_spec=grid_spec, compiler_params=_cparams(1), name="moe",
    )(ea, eb, nt, xs, wg, wu, wd, wg, wu, wd)


def _plan_kernel(route_ref, cnt_ref, eab_ref, pos_ref, meta_ref):
    cnt = cnt_ref[...]
    tiles = jnp.floor((cnt + float(TM_MOE - 1)) * (1.0 / TM_MOE))
    r = lax.broadcasted_iota(jnp.int32, (CNT_ROWS, CNT_ROWS), 0)
    c = lax.broadcasted_iota(jnp.int32, (CNT_ROWS, CNT_ROWS), 1)
    before = jnp.where(c < r, 1.0, 0.0).astype(BF16)
    first = jnp.dot(before, tiles.astype(BF16), preferred_element_type=F32)
    first_c, tiles_c = first[:, 0:1], tiles[:, 0:1]
    nt = jnp.sum(tiles_c, axis=0, keepdims=True)
    n = route_ref.shape[1]
    bid = lax.broadcasted_iota(jnp.int32, (CNT_ROWS, n), 0).astype(F32)
    base = jnp.sum(jnp.where(bid == route_ref[0:1, :], first_c * float(TM_MOE), 0.0), axis=0, keepdims=True)
    pos_ref[...] = (base + route_ref[3:4, :]).astype(jnp.int32)
    ti = jnp.minimum(lax.broadcasted_iota(jnp.int32, (CNT_ROWS, META_LANES), 1).astype(F32), nt - 1.0)
    mine = (ti >= first_c) & (ti < first_c + tiles_c)
    ea = jnp.sum(jnp.where(mine, eab_ref[:, 0:1], 0.0), axis=0, keepdims=True)
    eb = jnp.sum(jnp.where(mine, eab_ref[:, 1:2], 0.0), axis=0, keepdims=True)
    diag = (lax.broadcasted_iota(jnp.int32, (CNT_ROWS, META_LANES), 0)
            == lax.broadcasted_iota(jnp.int32, (CNT_ROWS, META_LANES), 1))
    first_l = jnp.sum(jnp.where(diag, first_c, 0.0), axis=0, keepdims=True)
    cnt_l = jnp.sum(jnp.where(diag, cnt[:, 0:1], 0.0), axis=0, keepdims=True)
    rows = [ea, eb, jnp.broadcast_to(nt, (1, META_LANES)), first_l, cnt_l, jnp.zeros((3, META_LANES), F32)]
    meta_ref[...] = jnp.concatenate(rows, axis=0).astype(jnp.int32)


def _plan(route, counts):
    n = route.shape[1]
    chunk = max(ck for ck in (2048, 1024, 512) if n % ck == 0)
    eab = jnp.zeros((CNT_ROWS, LANES), F32)
    eab = eab.at[:N_BUCKETS, 0].set(jnp.asarray(
        [GROUP_SIZE * (bk // N_PAIRS) + PAIR_LO[bk % N_PAIRS] for bk in range(N_BUCKETS)], F32))
    eab = eab.at[:N_BUCKETS, 1].set(jnp.asarray(
        [GROUP_SIZE * (bk // N_PAIRS) + PAIR_HI[bk % N_PAIRS] for bk in range(N_BUCKETS)], F32))
    const = lambda i: (0, 0)
    return pl.pallas_call(
        _plan_kernel,
        out_shape=[jax.ShapeDtypeStruct((1, n), jnp.int32),
                   jax.ShapeDtypeStruct((SUBLANES, META_LANES), jnp.int32)],
        grid=(n // chunk,),
        in_specs=[pl.BlockSpec((ROUTE_ROWS, chunk), lambda i: (0, i)), pl.BlockSpec((CNT_ROWS, LANES), const),
                  pl.BlockSpec((CNT_ROWS, LANES), const)],
        out_specs=[pl.BlockSpec((1, chunk), lambda i: (0, i)), pl.BlockSpec((SUBLANES, META_LANES), const)],
        compiler_params=_cparams(1), name="plan",
    )(route, counts, eab)


def _fill_padding(first_ref, cnt_ref, nt_ref, xs_ref, zbuf, sem, n_tiles_max, *, issue):
    if issue:
        zbuf[...] = jnp.zeros_like(zbuf)

    def fill(first_row, n_rows):
        cp = pltpu.make_async_copy(zbuf.at[pl.ds(0, n_rows)], xs_ref.at[pl.ds(first_row, n_rows)], sem)
        if issue:
            cp.start()
        else:
            cp.wait()

    def fill_tile(j, carry):
        fill(pl.multiple_of(j * TM_MOE, TM_MOE), TM_MOE)
        return carry

    lax.fori_loop(nt_ref[0], n_tiles_max, fill_tile, 0)

    for bk in range(N_BUCKETS):
        cnt = cnt_ref[bk]
        start = first_ref[bk] * TM_MOE + cnt
        pad = (-cnt) & (TM_MOE - 1)
        head = jnp.minimum((-cnt) & (SUBLANES - 1), pad)
        for j in range(SUBLANES - 1):
            @pl.when(j < head)
            def _(j=j, start=start):
                fill(start + j, 1)

        body = pad - head
        piece = TM_MOE // 2
        while piece >= SUBLANES:
            done = body & ~(2 * piece - 1)

            @pl.when((body & piece) != 0)
            def _(piece=piece, done=done, start=start, head=head):
                fill(pl.multiple_of(start + head + done, SUBLANES), piece)

            piece //= 2


def _sort_kernel(*refs, tm, d, step_ranges, n_tiles_max):
    n_src = len(step_ranges)
    pos_ref, first_ref, cnt_ref, nt_ref = refs[:4]
    srcs = refs[4:4 + 2 * n_src]
    xs_ref, buf, zbuf, sem, zsem = refs[4 + 2 * n_src:]
    n_steps = step_ranges[-1][1]
    i = pl.program_id(0)
    slot = i % 2

    def wait_rows(s):
        pltpu.make_async_copy(buf.at[s], xs_ref.at[pl.ds(0, tm)], sem.at[s]).wait()

    @pl.when(i == 0)
    def _():
        _fill_padding(first_ref, cnt_ref, nt_ref, xs_ref, zbuf, zsem, n_tiles_max, issue=True)

    @pl.when(i >= 2)
    def _():
        wait_rows(slot)

    for k, (s0, s1) in enumerate(step_ranges):
        @pl.when((i >= s0) & (i < s1))
        def _(k=k):
            tok_ref, route_ref = srcs[2 * k], srcs[2 * k + 1]
            cols = jnp.concatenate([route_ref[...], jnp.zeros((LANES - ROUTE_ROWS, tm), F32)], axis=0).T
            buf[slot, :, 0:d] = tok_ref[...]
            buf[slot, :, d:] = cols

    for t in range(tm):
        p = pos_ref[i * tm + t]
        pltpu.make_async_copy(buf.at[slot, pl.ds(t, 1)], xs_ref.at[pl.ds(p, 1)], sem.at[slot]).start()

    @pl.when(i == n_steps - 1)
    def _():
        wait_rows(slot)
        if n_steps >= 2:
            wait_rows(1 - slot)
        _fill_padding(first_ref, cnt_ref, nt_ref, xs_ref, zbuf, zsem, n_tiles_max, issue=False)


def _sort_rows(pos, first, cnt, nt, toks, routes, n_tiles_max):
    d = toks[0].shape[1]
    tm = TM_LAT
    step_ranges, s0 = [], 0
    for tok in toks:
        step_ranges.append((s0, s0 + tok.shape[0] // tm))
        s0 = step_ranges[-1][1]
    in_specs, args = [], []
    for (b0, b1), tok, rt in zip(step_ranges, toks, routes):
        blk = lambda i, *_, b0=b0, b1=b1: jnp.clip(i - b0, 0, b1 - b0 - 1)
        in_specs += [pl.BlockSpec((tm, d), lambda i, *_, blk=blk: (blk(i), 0)),
                     pl.BlockSpec((ROUTE_ROWS, tm),lambda i, *_, blk=blk: (0, blk(i)))]
        args += [tok, rt]
    grid_spec = pltpu.PrefetchScalarGridSpec(
        num_scalar_prefetch=4, grid=(s0,), in_specs=in_specs,
        out_specs=pl.BlockSpec(memory_space=pl.ANY),
        scratch_shapes=[pltpu.VMEM((2, tm, d + LANES), F32), pltpu.VMEM((TM_MOE, d + LANES), F32),
                        pltpu.SemaphoreType.DMA((2,)), pltpu.SemaphoreType.DMA])
    return pl.pallas_call(
        functools.partial(_sort_kernel, tm=tm, d=d, step_ranges=tuple(step_ranges), n_tiles_max=n_tiles_max),
        out_shape=jax.ShapeDtypeStruct((n_tiles_max * TM_MOE, d + LANES), F32), grid_spec=grid_spec,
        compiler_params=_cparams(1), name="sort_rows",
    )(pos, first, cnt, nt, *args)


def _moe_layer(toks, routes, counts, layer, wg, wu, wd):
    d = toks[0].shape[1]
    n = sum(t.shape[0] for t in toks)
    n_tiles_max = n // TM_MOE + N_BUCKETS
    assert n_tiles_max <= META_LANES
    route = routes[0] if len(routes) == 1 else jnp.concatenate(routes, axis=1)
    pos, meta = _plan(route, counts)
    pos = pos.reshape(n)
    xs = _sort_rows(pos, meta[3, :N_BUCKETS], meta[4, :N_BUCKETS], meta[2, :1], toks, routes, n_tiles_max)
    ys = _moe(meta[0, :n_tiles_max], meta[1, :n_tiles_max], meta[2, :1], xs, layer, wg, wu, wd)
    return jnp.concatenate([pos, jnp.zeros((TM_WIDE,), jnp.int32)]), ys


def _final_kernel(pos_ref, x_ref, ys_ref, mod_ref, gpost_ref, o_ref, fbuf, sem, *, tm):
    def add_residual(f):
        o_ref[...] = x_ref[...] + _rms(f, gpost_ref[...] * mod_ref[5:6, :])

    _with_gathered_tile(pos_ref, ys_ref, fbuf, sem, 0, tm, add_residual)


def _final(x, pos, ys, mods, layer, gpost, *, tm, rows_per_batch):
    n, d = x.shape
    tiles_per_batch = rows_per_batch // tm
    row = lambda i, *_: (i, 0)
    grid_spec = pltpu.PrefetchScalarGridSpec(
        num_scalar_prefetch=1, grid=(n // tm,),
        in_specs=[pl.BlockSpec((tm, d), row), pl.BlockSpec(memory_space=pl.ANY),
                  pl.BlockSpec((None, None, 6, d), lambda i, *_: (layer, i // tiles_per_batch, 0, 0)),
                  pl.BlockSpec((1, d), lambda i, *_: (0, 0))],
        out_specs=pl.BlockSpec((tm, d), row),
        scratch_shapes=[pltpu.VMEM((2, tm, d), F32), pltpu.SemaphoreType.DMA((2,))])
    return pl.pallas_call(
        functools.partial(_final_kernel, tm=tm), out_shape=jax.ShapeDtypeStruct((n, d), F32),
        grid_spec=grid_spec, compiler_params=_cparams(1), name="final",
    )(pos, x, ys, mods, gpost)


def kernel(x, c, ctx, c_ctx, w_ada, b_ada, g_mix_pre, g_mix_post, g_ffn_pre, g_ffn_post, conv_w_in, conv_b_in, conv_w_dw, conv_b_dw, conv_ln_g, conv_ln_b, conv_w_out, conv_b_out, attn_w_qkv, attn_w_o, attn_rpb, router_w, router_b, exp_w_gate, exp_w_up, exp_w_down):
    b, s, d = x.shape
    l = ctx.shape[1]
    n, nc = b * s, b * l
    depth = w_ada.shape[0]
    rows = s // GRID_W
    assert depth == 2 and b <= CTX_MOD_ROW and s % TM_WIDE == 0 and l == TM_CTX
    assert rows % Q_ROWS == 0 and rows >= SLAB_ROWS + Q_ROWS
    tm_lat = TM_LAT

    cc = jnp.zeros((MOD_ROWS, d), F32).at[:b].set(c).at[CTX_MOD_ROW].set(c_ctx)
    mods = _ada(cc, w_ada, b_ada).reshape(depth, MOD_ROWS, 6, d)

    row = lambda v: v.reshape(1, d)
    rwt = router_w.T
    rwh = rwt.astype(BF16)
    rwl = (rwt - rwh.astype(F32)).astype(BF16)
    rb = router_b.reshape(N_EXPERTS, 1)
    no_counts = jnp.zeros((CNT_ROWS, LANES), F32)

    xl = x.reshape(n, d)
    xc = ctx.reshape(nc, d)

    w_in = conv_w_in[0].astype(BF16)
    b_in = conv_b_in[0].reshape(1, 2 * d)
    conv = (conv_w_dw[0], row(conv_b_dw[0]), row(conv_ln_g[0]), row(conv_ln_b[0]),
            conv_w_out[0].astype(BF16), row(conv_b_out[0]))
    tailp = (row(g_mix_post[0]), row(g_ffn_pre[0]), rwh, rwl, rb)
    (u,) = _proj(xl, mods, 0, row(g_mix_pre[0]), w_in, tm=TM_WIDE, rows_per_batch=s, ctx=False,
                 bias=b_in, name="conv_in")
    (uc,) = _proj(xc, mods, 0, row(g_mix_pre[0]), w_in, tm=TM_CTX, rows_per_batch=l, ctx=True,
                  bias=b_in, name="conv_in_ctx")
    x1, tok, route, counts = _conv_tail(u, xl, mods, 0, conv, tailp, no_counts, tm=tm_lat, rows_per_batch=s,
                                        ctx=False, name="conv_tail")
    xc1, tokc, routec, counts = _conv_tail(uc, xc, mods, 0, conv, tailp, counts, tm=TM_CTX, rows_per_batch=l,
                                           ctx=True, name="conv_tail_ctx")
    pos0, ys0 = _moe_layer([tok, tokc], [route, routec], counts, 0, exp_w_gate, exp_w_up, exp_w_down)

    w_qkv = attn_w_qkv[0].astype(BF16)
    x2, q, k, v = _proj(x1, mods, 1, row(g_mix_pre[1]), w_qkv, tm=tm_lat, rows_per_batch=s, ctx=False,
                        resid=(pos0, ys0, row(g_ffn_post[0]), 0), n_out=3, name="qkv")
    _, kc, vc = _proj(xc1, mods, 1, row(g_mix_pre[1]), w_qkv[:, d:], tm=TM_CTX, rows_per_batch=l, ctx=True,
                      resid=(pos0, ys0, row(g_ffn_post[0]), n), n_out=2, name="kv_ctx")
    c2 = _bias_pairs(attn_rpb[0])
    o = _natten(q.reshape(b, s, d), k.reshape(b, s, d), v.reshape(b, s, d),
                kc.reshape(b, l, d), vc.reshape(b, l, d), c2)
    tailp1 = (row(g_mix_post[1]), row(g_ffn_pre[1]), rwh, rwl, rb)
    x3, tok1, route1, counts1 = _attn_tail(o.reshape(n, d), x2, mods, 1, attn_w_o[0].astype(BF16), tailp1,
                                           no_counts, tm=tm_lat, rows_per_batch=s, name="attn_tail")
    pos1, ys1 = _moe_layer([tok1], [route1], counts1, 1, exp_w_gate, exp_w_up, exp_w_down)
    out = _final(x3, pos1, ys1, mods, 1, row(g_ffn_post[1]), tm=TM_WIDE, rows_per_batch=s)
    return out.reshape(b, s, d)
```

```python
import functools

import jax
import jax.numpy as jnp
from jax import lax
from jax.experimental import pallas as pl
from jax.experimental.pallas import tpu as pltpu

F32 = jnp.float32
BF16 = jnp.bfloat16

LANES = 128
SUBLANES = 8

EPS = 1e-6
GRID_W = 64
WIN_H = 8
WIN_W = 16
N_HEADS = 16
HEAD_DIM = 64
N_EXPERTS = 16
N_GROUPS = 4
GROUP_SIZE = N_EXPERTS // N_GROUPS
N_PAIRS = 6
N_BUCKETS = N_GROUPS * N_PAIRS
PAIR_LO = (0, 0, 0, 1, 1, 2)
PAIR_HI = (1, 2, 3, 2, 3, 3)
CONV_WIDTH = 31
CONV_PAD = (CONV_WIDTH - 1) // 2
HALO = 16
MASK_BIAS = -1e30
Q_ROWS = 4
SLAB_ROWS = Q_ROWS + WIN_H
MOD_ROWS = 16
CTX_MOD_ROW = 8
ROUTE_ROWS = SUBLANES
CNT_ROWS = 32
META_LANES = 256
GATHER_SLOTS = 3

TM_LAT = 512
TM_WIDE = 1024
TM_CTX = 256
TM_MOE = 512
VMEM_LIMIT = 56 * 1024 * 1024

_NT = (((1,), (1,)), ((), ()))


def _cparams(n_axes):
    return pltpu.CompilerParams(dimension_semantics=("arbitrary",) * n_axes,
                                vmem_limit_bytes=VMEM_LIMIT)


def _sigmoid(v):
    return 1.0 / (1.0 + jnp.exp(-v))


def _rms(v, g):
    return v * lax.rsqrt(jnp.mean(v * v, axis=-1, keepdims=True) + EPS) * g


def _split_bf16(v):
    hi = v.astype(BF16)
    lo = (v - hi.astype(F32)).astype(BF16)
    return hi, lo


def _ada_kernel(c_ref, w_ref, b_ref, o_ref):
    a = c_ref[...]
    a = a * _sigmoid(a)
    a_hi, a_lo = _split_bf16(a)
    w_hi, w_lo = _split_bf16(w_ref[...])
    acc = jnp.dot(a_hi, w_hi, preferred_element_type=F32)
    acc += jnp.dot(a_lo, w_hi, preferred_element_type=F32)
    acc += jnp.dot(a_hi, w_lo, preferred_element_type=F32)
    o_ref[...] = acc + b_ref[...]


def _ada(cc, w_ada, b_ada):
    depth, d, n = w_ada.shape
    bn = 1024
    return pl.pallas_call(
        _ada_kernel,
        out_shape=jax.ShapeDtypeStruct((depth, MOD_ROWS, n), F32),
        grid=(depth, n // bn),
        in_specs=[pl.BlockSpec((MOD_ROWS, d), lambda l, j: (0, 0)),
                  pl.BlockSpec((None, d, bn), lambda l, j: (l, 0, j)),
                  pl.BlockSpec((None, 1, bn), lambda l, j: (l, 0, j))],
        out_specs=pl.BlockSpec((None, MOD_ROWS, bn), lambda l, j: (l, 0, j)),
        compiler_params=_cparams(2),
        name="ada",
    )(cc, w_ada, b_ada.reshape(depth, 1, n))


def _start_row_gather(pos_ref, src_ref, buf, sem, first_row, slot, tm):
    for t in range(tm):
        p = pos_ref[first_row + t]
        pltpu.make_async_copy(src_ref.at[pl.ds(p, 1)], buf.at[slot, pl.ds(t, 1)], sem.at[slot]).start()


def _wait_row_gather(src_ref, buf, sem, slot, tm):
    pltpu.make_async_copy(src_ref.at[pl.ds(0, tm)], buf.at[slot], sem.at[slot]).wait()


def _with_gathered_tile(pos_ref, src_ref, buf, sem, row0, tm, consume):
    i = pl.program_id(0)
    ahead = GATHER_SLOTS - 1

    @pl.when(i == 0)
    def _():
        for j in range(ahead):
            _start_row_gather(pos_ref, src_ref, buf, sem, row0 + j * tm, j, tm)

    for slot in range(GATHER_SLOTS):
        @pl.when(i % GATHER_SLOTS == slot)
        def _(slot=slot):
            _wait_row_gather(src_ref, buf, sem, slot, tm)
            consume(buf[slot])
            _start_row_gather(pos_ref, src_ref, buf, sem, row0 + (i + ahead) * tm, (slot + ahead) % GATHER_SLOTS, tm)

            @pl.when(i == pl.num_programs(0) - 1)
            def _():
                for j in range(1, GATHER_SLOTS):
                    _wait_row_gather(src_ref, buf, sem, (slot + j) % GATHER_SLOTS, tm)


def _proj_kernel(*refs, fuse_resid, glu, d, tm, f_row0):
    refs = list(refs)
    if fuse_resid:
        pos_ref = refs.pop(0)
    x_ref = refs.pop(0)
    if fuse_resid:
        ys_ref, modp_ref, gpost_ref = refs.pop(0), refs.pop(0), refs.pop(0)
    mod_ref, gpre_ref, w_ref = refs.pop(0), refs.pop(0), refs.pop(0)
    if glu:
        b_ref = refs.pop(0)
    if fuse_resid:
        xn_ref = refs.pop(0)
        fbuf, sem = refs.pop(-2), refs.pop(-1)

    def project(f):
        x = x_ref[...]
        if fuse_resid:
            x = x + _rms(f, gpost_ref[...] * modp_ref[5:6, :])
            xn_ref[...] = x
        h = _rms(x, gpre_ref[...] * (1.0 + mod_ref[1:2, :])) + mod_ref[0:1, :]
        z = jnp.dot(h.astype(BF16), w_ref[...], preferred_element_type=F32)
        if glu:
            z = z + b_ref[...]
            refs[0][...] = (z[:, :d] * _sigmoid(z[:, d:])).astype(BF16)
        else:
            n_out = len(refs)
            for j, o_ref in enumerate(refs):
                part = z[:, j * d:(j + 1) * d]
                if n_out == 3 and j == 0:
                    part = part * (HEAD_DIM ** -0.5)
                o_ref[...] = part.astype(BF16)

    if fuse_resid:
        _with_gathered_tile(pos_ref, ys_ref, fbuf, sem, f_row0, tm, project)
    else:
        project(None)


def _proj(x, mods, layer, gpre, w, *, tm, rows_per_batch, ctx, bias=None, resid=None, n_out=1, name):
    n, d = x.shape
    tiles_per_batch = rows_per_batch // tm
    if ctx:
        mod_idx = lambda i, *_: (layer, CTX_MOD_ROW, 0, 0)
    else:
        mod_idx = lambda i, *_: (layer, i // tiles_per_batch, 0, 0)
    row = lambda i, *_: (i, 0)
    const = lambda i, *_: (0, 0)
    args, specs, scratch, f_row0 = [x], [pl.BlockSpec((tm, d), row)], [], 0
    if resid is not None:
        pos, ys, gpost, f_row0 = resid
        if ctx:
            modp_idx = lambda i, *_: (layer - 1, CTX_MOD_ROW, 0, 0)
        else:
            modp_idx = lambda i, *_: (layer - 1, i // tiles_per_batch, 0, 0)
        args = [pos] + args + [ys, mods, gpost]
        specs += [pl.BlockSpec(memory_space=pl.ANY), pl.BlockSpec((None, None, 6, d), modp_idx),
                  pl.BlockSpec((1, d), const)]
        scratch = [pltpu.VMEM((GATHER_SLOTS, tm, d), F32), pltpu.SemaphoreType.DMA((GATHER_SLOTS,))]
    args += [mods, gpre, w]
    specs += [pl.BlockSpec((None, None, 6, d), mod_idx), pl.BlockSpec((1, d), const),
              pl.BlockSpec(w.shape, const)]
    glu = bias is not None
    if glu:
        args.append(bias)
        specs.append(pl.BlockSpec(bias.shape, const))
    out_shape, out_specs = [], []
    if resid is not None:
        out_shape.append(jax.ShapeDtypeStruct((n, d), F32))
        out_specs.append(pl.BlockSpec((tm, d), row))
    for _ in range(n_out):
        out_shape.append(jax.ShapeDtypeStruct((n, d), BF16))
        out_specs.append(pl.BlockSpec((tm, d), row))
    grid_spec = pltpu.PrefetchScalarGridSpec(
        num_scalar_prefetch=0 if resid is None else 1, grid=(n // tm,),
        in_specs=specs, out_specs=out_specs, scratch_shapes=scratch)
    return pl.pallas_call(
        functools.partial(_proj_kernel, fuse_resid=resid is not None, glu=glu, d=d, tm=tm, f_row0=f_row0),
        out_shape=out_shape, grid_spec=grid_spec, compiler_params=_cparams(1), name=name,
    )(*args)


def _route(lg, rb):
    s = _sigmoid(lg)
    ssel = s + rb
    sel = [ssel[e:e + 1, :] for e in range(N_EXPERTS)]
    raw = [s[e:e + 1, :] for e in range(N_EXPERTS)]
    scores = []
    for g in range(N_GROUPS):
        v0, v1, v2, v3 = sel[4 * g:4 * g + 4]
        a, b = jnp.maximum(v0, v1), jnp.minimum(v0, v1)
        c, e = jnp.maximum(v2, v3), jnp.minimum(v2, v3)
        scores.append(jnp.maximum(a, c) + jnp.maximum(jnp.minimum(a, c), jnp.maximum(b, e)))
    best = scores[0]
    gi = jnp.zeros_like(best)
    for g in range(1, N_GROUPS):
        better = scores[g] > best
        gi = jnp.where(better, float(g), gi)
        best = jnp.where(better, scores[g], best)

    def pick(rows, j):
        return jnp.where(gi == 0.0, rows[j],
                         jnp.where(gi == 1.0, rows[4 + j], jnp.where(gi == 2.0, rows[8 + j], rows[12 + j])))

    v = [pick(sel, j) for j in range(GROUP_SIZE)]
    r = [pick(raw, j) for j in range(GROUP_SIZE)]
    m = []
    for j in range(GROUP_SIZE):
        ahead = jnp.zeros_like(best)
        for k in range(GROUP_SIZE):
            if k == j:
                continue
            beats = (v[k] >= v[j]) if k < j else (v[k] > v[j])
            ahead = ahead + jnp.where(beats, 1.0, 0.0)
        m.append(ahead < 2.0)
    pair = jnp.where(m[0], jnp.where(m[1], 0.0, jnp.where(m[2], 1.0, 2.0)),
                     jnp.where(m[1], jnp.where(m[2], 3.0, 4.0), 5.0))
    sa = jnp.where(m[0], r[0], jnp.where(m[1], r[1], r[2]))
    sb = jnp.where(m[3], r[3], jnp.where(m[2], r[2], r[1]))
    den = sa + sb
    bucket = gi * float(N_PAIRS) + pair
    zeros = jnp.zeros((5, lg.shape[1]), F32)
    return jnp.concatenate([bucket, sa / den, sb / den, zeros], axis=0)


def _tail(y, x, mod_ref, gpost_ref, gffn_ref, rwh_ref, rwl_ref, rb_ref, tri_ref, cin_ref,
          x1_ref, tok_ref, route_ref, cnt_ref):
    x1 = x + _rms(y, gpost_ref[...] * mod_ref[2:3, :])
    x1_ref[...] = x1
    t = _rms(x1, gffn_ref[...] * (1.0 + mod_ref[4:5, :])) + mod_ref[3:4, :]
    tok_ref[...] = t
    t_hi, t_lo = _split_bf16(t)
    lg = lax.dot_general(rwh_ref[...], t_hi, _NT, preferred_element_type=F32)
    lg += lax.dot_general(rwl_ref[...], t_hi, _NT, preferred_element_type=F32)
    lg += lax.dot_general(rwh_ref[...], t_lo, _NT, preferred_element_type=F32)
    rt = _route(lg, rb_ref[...])
    tm = rt.shape[1]

    @pl.when(pl.program_id(0) == 0)
    def _():
        cnt_ref[...] = cin_ref[...]

    bid = lax.broadcasted_iota(jnp.int32, (CNT_ROWS, tm), 0).astype(F32)
    onehot = jnp.where(bid == rt[0:1, :], 1.0, 0.0)
    cum = jnp.dot(onehot.astype(BF16), tri_ref[...], preferred_element_type=F32)
    rank = jnp.sum(onehot * (cum + cnt_ref[:, 0:1]), axis=0, keepdims=True) - 1.0
    cnt_ref[...] = cnt_ref[...] + cum[:, tm - 1:tm]
    route_ref[...] = jnp.concatenate([rt[0:3, :], rank, jnp.zeros((4, tm), F32)], axis=0)


def _conv_tail_kernel(u_ref, up_ref, un_ref, x_ref, mod_ref, wdw_ref, bdw_ref, lng_ref, lnb_ref,
                      wout_ref, bout_ref, gpost_ref, gffn_ref, rwh_ref, rwl_ref, rb_ref, tri_ref, cin_ref,
                      x1_ref, tok_ref, route_ref, cnt_ref, ubuf, ush, cacc, wb, *, tm, tiles_per_seq, chunk):
    t = pl.program_id(0) % tiles_per_seq
    prev_ok = jnp.where(t > 0, 1.0, 0.0)
    next_ok = jnp.where(t < tiles_per_seq - 1, 1.0, 0.0)
    ubuf[0:HALO, :] = up_ref[...].astype(F32) * prev_ok
    ubuf[HALO:HALO + tm, :] = u_ref[...].astype(F32)
    ubuf[HALO + tm:, :] = un_ref[...].astype(F32) * next_ok

    sub = SUBLANES
    base = HALO - CONV_PAD
    span = tm + sub * ((CONV_WIDTH - 1 + base) // sub)
    for s in range(1, sub):
        ush[s - 1, 0:span, :] = ubuf[s:s + span, :]

    @pl.when(pl.program_id(0) == 0)
    def _():
        for k in range(CONV_WIDTH):
            wb[k] = jnp.broadcast_to(wdw_ref[k:k + 1, :], (sub, wb.shape[2]))

    n_lane_blocks = wb.shape[2] // LANES

    def body(c, carry):
        r0 = pl.multiple_of(c * chunk, chunk)
        for lb in range(n_lane_blocks):
            lanes = slice(lb * LANES, (lb + 1) * LANES)
            acc = None
            for s in range(sub):
                if s == 0:
                    win = ubuf[pl.ds(r0, chunk + span - tm), lanes]
                else:
                    win = ush[s - 1, pl.ds(r0, chunk + span - tm), lanes]
                win = win.reshape((chunk + span - tm) // sub, sub, LANES)
                for q in range((span - tm) // sub + 1):
                    k = sub * q + s - base
                    if 0 <= k < CONV_WIDTH:
                        term = win[q:q + chunk // sub] * wb[k, :, lanes][None]
                        acc = term if acc is None else acc + term
            cacc[pl.ds(r0, chunk), lanes] = acc.reshape(chunk, LANES)
        return carry

    lax.fori_loop(0, tm // chunk, body, 0)

    y = cacc[...] + bdw_ref[...]
    mu = jnp.mean(y, axis=-1, keepdims=True)
    yc = y - mu
    var = jnp.mean(yc * yc, axis=-1, keepdims=True)
    yn = yc * lax.rsqrt(var + EPS) * lng_ref[...] + lnb_ref[...]
    act = yn * _sigmoid(yn)
    z = jnp.dot(act.astype(BF16), wout_ref[...], preferred_element_type=F32) + bout_ref[...]
    _tail(z, x_ref[...], mod_ref, gpost_ref, gffn_ref, rwh_ref, rwl_ref, rb_ref, tri_ref, cin_ref,
          x1_ref, tok_ref, route_ref, cnt_ref)


def _tail_common_specs(d, tm):
    const = lambda i: (0, 0)
    return [pl.BlockSpec((1, d), const), pl.BlockSpec((1, d), const),
            pl.BlockSpec((N_EXPERTS, d), const), pl.BlockSpec((N_EXPERTS, d), const),
            pl.BlockSpec((N_EXPERTS, 1), const), pl.BlockSpec((tm, tm), const),
            pl.BlockSpec((CNT_ROWS, LANES), const)]


def _tail_common_args(tailp, counts, tm):
    gpost, gffn, rwh, rwl, rb = tailp
    j = lax.broadcasted_iota(jnp.int32, (tm, tm), 0)
    t = lax.broadcasted_iota(jnp.int32, (tm, tm), 1)
    return [gpost, gffn, rwh, rwl, rb, (j <= t).astype(BF16), counts]


def _tail_outs(n, d, tm):
    out_shape = [jax.ShapeDtypeStruct((n, d), F32), jax.ShapeDtypeStruct((n, d), F32),
                 jax.ShapeDtypeStruct((ROUTE_ROWS, n), F32), jax.ShapeDtypeStruct((CNT_ROWS, LANES), F32)]
    out_specs = [pl.BlockSpec((tm, d), lambda i: (i, 0)), pl.BlockSpec((tm, d), lambda i: (i, 0)),
                 pl.BlockSpec((ROUTE_ROWS, tm),lambda i: (0, i)), pl.BlockSpec((CNT_ROWS, LANES), lambda i: (0, 0))]
    return out_shape, out_specs


def _conv_tail(u, x, mods, layer, conv, tailp, counts, *, tm, rows_per_batch, ctx, name):
    n, d = x.shape
    wdw, bdw, lng, lnb, wout, bout = conv
    tiles_per_seq = rows_per_batch // tm
    hb = tm // HALO
    n_hb = n // HALO
    if ctx:
        mod_idx = lambda i: (layer, CTX_MOD_ROW, 0, 0)
    else:
        mod_idx = lambda i: (layer, i // tiles_per_seq, 0, 0)
    row = lambda i: (i, 0)
    const = lambda i: (0, 0)
    specs = [pl.BlockSpec((tm, d), row),
             pl.BlockSpec((HALO, d), lambda i: (jnp.maximum(i * hb - 1, 0), 0)),
             pl.BlockSpec((HALO, d), lambda i: (jnp.minimum((i + 1) * hb, n_hb - 1), 0)),
             pl.BlockSpec((tm, d), row),
             pl.BlockSpec((None, None, 6, d), mod_idx),
             pl.BlockSpec(wdw.shape, const), pl.BlockSpec((1, d), const),
             pl.BlockSpec((1, d), const), pl.BlockSpec((1, d), const),
             pl.BlockSpec(wout.shape, const), pl.BlockSpec((1, d), const)]
    specs += _tail_common_specs(d, tm)
    out_shape, out_specs = _tail_outs(n, d, tm)
    chunk = 64
    return pl.pallas_call(
        functools.partial(_conv_tail_kernel, tm=tm, tiles_per_seq=tiles_per_seq, chunk=chunk),
        out_shape=out_shape, grid=(n // tm,), in_specs=specs, out_specs=out_specs,
        scratch_shapes=[pltpu.VMEM((tm + 2 * HALO, d), F32), pltpu.VMEM((SUBLANES - 1, tm + 2 * HALO, d), F32),
                        pltpu.VMEM((tm, d), F32), pltpu.VMEM((CONV_WIDTH, SUBLANES, d), F32)],
        compiler_params=_cparams(1), name=name,
    )(u, u, u, x, mods, wdw, bdw, lng, lnb, wout, bout, *_tail_common_args(tailp, counts, tm))


def _attn_tail_kernel(o_ref, x_ref, mod_ref, wo_ref, gpost_ref, gffn_ref, rwh_ref, rwl_ref, rb_ref, tri_ref,
                      cin_ref, x1_ref, tok_ref, route_ref, cnt_ref):
    y = jnp.dot(o_ref[...], wo_ref[...], preferred_element_type=F32)
    _tail(y, x_ref[...], mod_ref, gpost_ref, gffn_ref, rwh_ref, rwl_ref, rb_ref, tri_ref, cin_ref,
          x1_ref, tok_ref, route_ref, cnt_ref)


def _attn_tail(o, x, mods, layer, wo, tailp, counts, *, tm, rows_per_batch, name):
    n, d = x.shape
    tiles_per_batch = rows_per_batch // tm
    mod_idx = lambda i: (layer, i // tiles_per_batch, 0, 0)
    row = lambda i: (i, 0)
    const = lambda i: (0, 0)
    specs = [pl.BlockSpec((tm, d), row), pl.BlockSpec((tm, d), row),
             pl.BlockSpec((None, None, 6, d), mod_idx), pl.BlockSpec(wo.shape, const)]
    specs += _tail_common_specs(d, tm)
    out_shape, out_specs = _tail_outs(n, d, tm)
    return pl.pallas_call(
        _attn_tail_kernel, out_shape=out_shape, grid=(n // tm,), in_specs=specs, out_specs=out_specs,
        compiler_params=_cparams(1), name=name,
    )(o, x, mods, wo, *_tail_common_args(tailp, counts, tm))


def _window_plan(rows):
    n_blocks = rows // Q_ROWS
    plan = []
    for blk in (0, 1, n_blocks - 1):
        slab = min(max(Q_ROWS * blk - WIN_H // 2, 0), rows - SLAB_ROWS)
        per_row = []
        for rl in range(Q_ROWS):
            r = Q_ROWS * blk + rl
            r0 = min(max(r - WIN_H // 2, 0), rows - WIN_H)
            pairs = []
            for kp in range(SLAB_ROWS // 2):
                kr = slab + 2 * kp
                inside = (r0 <= kr < r0 + WIN_H, r0 <= kr + 1 < r0 + WIN_H)
                pairs.append((min(max(kr - r + WIN_H, 0), 2 * WIN_H - 1), inside))
            per_row.append(pairs)
        plan.append(per_row)
    return plan


def _natten_kernel(q_ref, k_ref, v_ref, kc_ref, vc_ref, c2_ref, o_ref, bias, *, rows):
    hw = 2 * HEAD_DIM
    nq = Q_ROWS * GRID_W
    n_slab = SLAB_ROWS * GRID_W
    n_blocks = rows // Q_ROWS
    lane = lax.broadcasted_iota(jnp.int32, (1, hw), 1)
    low = lane < HEAD_DIM

    @pl.when(pl.program_id(1) == 0)
    def _():
        for var, per_row in enumerate(_window_plan(rows)):
            for h in range(2):
                for rl, pairs in enumerate(per_row):
                    for kp, (entry, inside) in enumerate(pairs):
                        if inside[0] or inside[1]:
                            blk = c2_ref[h, entry]
                            if not inside[0]:
                                blk = jnp.where(low, MASK_BIAS, blk)
                            if not inside[1]:
                                blk = jnp.where(low, blk, MASK_BIAS)
                        else:
                            blk = jnp.full((GRID_W, hw), MASK_BIAS, F32)
                        bias[var, h * nq + rl * GRID_W:h * nq + (rl + 1) * GRID_W, kp * hw:(kp + 1) * hw] = blk

    kc = kc_ref[...]
    vc = vc_ref[...]

    def body(blk, carry):
        slab = jnp.clip(Q_ROWS * blk - WIN_H // 2, 0, rows - SLAB_ROWS)
        var = jnp.where(blk == 0, 0, jnp.where(blk == n_blocks - 1, 2, 1))
        q0 = pl.multiple_of(blk * nq, nq)
        k0 = pl.multiple_of(slab * GRID_W, GRID_W)
        q2 = q_ref[pl.ds(q0, nq), :]
        zero = jnp.zeros_like(q2)
        qm = jnp.concatenate([jnp.where(low, q2, zero), jnp.where(low, zero, q2)], axis=0)
        kw = k_ref[pl.ds(k0, n_slab), :]
        vw = v_ref[pl.ds(k0, n_slab), :]
        s_loc = lax.dot_general(qm, kw, _NT, preferred_element_type=F32) + bias[var]
        s_ctx = lax.dot_general(qm, kc, _NT, preferred_element_type=F32)
        m = jnp.maximum(jnp.max(s_loc, axis=-1, keepdims=True), jnp.max(s_ctx, axis=-1, keepdims=True))
        p_loc = jnp.exp(s_loc - m)
        p_ctx = jnp.exp(s_ctx - m)
        den = jnp.sum(p_loc, axis=-1, keepdims=True) + jnp.sum(p_ctx, axis=-1, keepdims=True)
        acc = jnp.dot(p_loc.astype(BF16), vw, preferred_element_type=F32)
        acc += jnp.dot(p_ctx.astype(BF16), vc, preferred_element_type=F32)
        acc = acc / den
        o_ref[pl.ds(q0, nq), :] = jnp.where(low, acc[:nq], acc[nq:]).astype(BF16)
        return carry

    lax.fori_loop(0, n_blocks, body, 0, unroll=16)


def _natten(q, k, v, kc, vc, c2):
    b, s, d = q.shape
    l = kc.shape[1]
    rows = s // GRID_W
    hw = 2 * HEAD_DIM
    lat = pl.BlockSpec((None, s, hw), lambda hp, bi: (bi, 0, hp))
    cspec = pl.BlockSpec((None, l, hw), lambda hp, bi: (bi, 0, hp))
    return pl.pallas_call(
        functools.partial(_natten_kernel, rows=rows),
        out_shape=jax.ShapeDtypeStruct((b, s, d), BF16),
        grid=(d // hw, b),
        in_specs=[lat, lat, lat, cspec, cspec,
                  pl.BlockSpec((2, 2 * WIN_H, GRID_W, hw), lambda hp, bi: (hp, 0, 0, 0))],
        out_specs=lat,
        scratch_shapes=[pltpu.VMEM((3, 2 * Q_ROWS * GRID_W, SLAB_ROWS * GRID_W), F32)],
        compiler_params=_cparams(2), name="natten",
    )(q, k, v, kc, vc, c2)


def _bias_kernel(r_ref, oh_ref, mask_ref, o_ref):
    r = r_ref[...]
    r1 = r.astype(BF16)
    r2 = (r - r1.astype(F32)).astype(BF16)
    r3 = (r - r1.astype(F32) - r2.astype(F32)).astype(BF16)
    oh = oh_ref[...]
    acc = jnp.dot(r1, oh, preferred_element_type=F32)
    acc += jnp.dot(r2, oh, preferred_element_type=F32)
    acc += jnp.dot(r3, oh, preferred_element_type=F32)
    o_ref[...] = acc + mask_ref[...]


def _bias_pairs(rpb):
    n_h, n_dr, n_dc = rpb.shape
    col = jnp.arange(GRID_W)[:, None]
    kcol = jnp.arange(GRID_W)[None, :]
    cs = jnp.clip(col - WIN_W // 2, 0, GRID_W - WIN_W)
    valid = (kcol >= cs) & (kcol < cs + WIN_W)
    dc = jnp.arange(32)[:, None, None]
    onehot = ((kcol - col + WIN_W - 1)[None] == dc) & valid[None]
    onehot = onehot.reshape(32, GRID_W * GRID_W).astype(BF16)
    maskadd = jnp.where(valid, 0.0, MASK_BIAS).reshape(1, GRID_W * GRID_W).astype(F32)
    rp = jnp.zeros((n_h, 16, 32), F32).at[:, :n_dr, :n_dc].set(rpb)
    flat = pl.pallas_call(
        _bias_kernel,
        out_shape=jax.ShapeDtypeStruct((n_h, 16, GRID_W * GRID_W), F32),
        grid=(n_h,),
        in_specs=[pl.BlockSpec((None, 16, 32), lambda h: (h, 0, 0)),
                  pl.BlockSpec((32, GRID_W * GRID_W), lambda h: (0, 0)),
                  pl.BlockSpec((1, GRID_W * GRID_W), lambda h: (0, 0))],
        out_specs=pl.BlockSpec((None, 16, GRID_W * GRID_W), lambda h: (h, 0, 0)),
        compiler_params=_cparams(1), name="bias_expand",
    )(rp, onehot, maskadd)
    tab = flat[:, :n_dr].reshape(n_h, n_dr, GRID_W, GRID_W)
    masked = jnp.full((n_h, 1, GRID_W, GRID_W), MASK_BIAS, F32)
    ext = jnp.concatenate([masked, tab, masked], axis=1)
    return jnp.concatenate([ext[:, :-1], ext[:, 1:]], axis=-1)


def _moe_kernel(ea_ref, eb_ref, nt_ref, xs_ref, wga_ref, wua_ref, wda_ref, wgb_ref, wub_ref, wdb_ref,
                y_ref, cga, cua, cda, cgb, cub, cdb, *, d):
    i = pl.program_id(0)
    prev = jnp.maximum(i - 1, 0)
    fresh = (i == 0) | (ea_ref[i] != ea_ref[prev]) | (eb_ref[i] != eb_ref[prev])

    @pl.when(fresh)
    def _():
        for src, dst in ((wga_ref, cga), (wua_ref, cua), (wda_ref, cda),
                         (wgb_ref, cgb), (wub_ref, cub), (wdb_ref, cdb)):
            dst[...] = src[...].astype(BF16)

    @pl.when(i < nt_ref[0])
    def _():
        xt = xs_ref[:, 0:d].astype(BF16)

        def expert(wg_ref, wu_ref, wd_ref):
            g = jnp.dot(xt, wg_ref[...], preferred_element_type=F32)
            u = jnp.dot(xt, wu_ref[...], preferred_element_type=F32)
            h = (g * _sigmoid(g)) * u
            return jnp.dot(h.astype(BF16), wd_ref[...], preferred_element_type=F32)

        ya = expert(cga, cua, cda)
        yb = expert(cgb, cub, cdb)
        y_ref[...] = xs_ref[:, d + 1:d + 2] * ya + xs_ref[:, d + 2:d + 3] * yb

    @pl.when(i >= nt_ref[0])
    def _():
        y_ref[...] = jnp.zeros_like(y_ref)


def _moe(ea, eb, nt, xs, layer, wg, wu, wd):
    npad, wide = xs.shape
    d = wide - LANES
    de = wg.shape[3]
    sel_a = lambda i, ea, eb, nt: (layer, ea[i], 0, 0)
    sel_b = lambda i, ea, eb, nt: (layer, eb[i], 0, 0)
    up = (None, None, d, de)
    down = (None, None, de, d)
    grid_spec = pltpu.PrefetchScalarGridSpec(
        num_scalar_prefetch=3, grid=(npad // TM_MOE,),
        in_specs=[pl.BlockSpec((TM_MOE, wide), lambda i, ea, eb, nt: (jnp.minimum(i, nt[0] - 1), 0)),
                  pl.BlockSpec(up, sel_a), pl.BlockSpec(up, sel_a), pl.BlockSpec(down, sel_a),
                  pl.BlockSpec(up, sel_b), pl.BlockSpec(up, sel_b), pl.BlockSpec(down, sel_b)],
        out_specs=pl.BlockSpec((TM_MOE, d), lambda i, ea, eb, nt: (i, 0)),
        scratch_shapes=[pltpu.VMEM((d, de), BF16), pltpu.VMEM((d, de), BF16), pltpu.VMEM((de, d), BF16),
                        pltpu.VMEM((d, de), BF16), pltpu.VMEM((d, de), BF16), pltpu.VMEM((de, d), BF16)])
    return pl.pallas_call(
        functools.partial(_moe_kernel, d=d), out_shape=jax.ShapeDtypeStruct((npad, d), F32),
        grid_spec=grid_spec, compiler_params=_cparams(1), name="moe",
    )(ea, eb, nt, xs, wg, wu, wd, wg, wu, wd)


def _plan_kernel(route_ref, cnt_ref, eab_ref, pos_ref, meta_ref):
    cnt = cnt_ref[...]
    tiles = jnp.floor((cnt + float(TM_MOE - 1)) * (1.0 / TM_MOE))
    r = lax.broadcasted_iota(jnp.int32, (CNT_ROWS, CNT_ROWS), 0)
    c = lax.broadcasted_iota(jnp.int32, (CNT_ROWS, CNT_ROWS), 1)
    before = jnp.where(c < r, 1.0, 0.0).astype(BF16)
    first = jnp.dot(before, tiles.astype(BF16), preferred_element_type=F32)
    first_c, tiles_c = first[:, 0:1], tiles[:, 0:1]
    nt = jnp.sum(tiles_c, axis=0, keepdims=True)
    n = route_ref.shape[1]
    bid = lax.broadcasted_iota(jnp.int32, (CNT_ROWS, n), 0).astype(F32)
    base = jnp.sum(jnp.where(bid == route_ref[0:1, :], first_c * float(TM_MOE), 0.0), axis=0, keepdims=True)
    pos_ref[...] = (base + route_ref[3:4, :]).astype(jnp.int32)
    ti = jnp.minimum(lax.broadcasted_iota(jnp.int32, (CNT_ROWS, META_LANES), 1).astype(F32), nt - 1.0)
    mine = (ti >= first_c) & (ti < first_c + tiles_c)
    ea = jnp.sum(jnp.where(mine, eab_ref[:, 0:1], 0.0), axis=0, keepdims=True)
    eb = jnp.sum(jnp.where(mine, eab_ref[:, 1:2], 0.0), axis=0, keepdims=True)
    diag = (lax.broadcasted_iota(jnp.int32, (CNT_ROWS, META_LANES), 0)
            == lax.broadcasted_iota(jnp.int32, (CNT_ROWS, META_LANES), 1))
    first_l = jnp.sum(jnp.where(diag, first_c, 0.0), axis=0, keepdims=True)
    cnt_l = jnp.sum(jnp.where(diag, cnt[:, 0:1], 0.0), axis=0, keepdims=True)
    rows = [ea, eb, jnp.broadcast_to(nt, (1, META_LANES)), first_l, cnt_l, jnp.zeros((3, META_LANES), F32)]
    meta_ref[...] = jnp.concatenate(rows, axis=0).astype(jnp.int32)


def _plan(route, counts):
    n = route.shape[1]
    chunk = max(ck for ck in (2048, 1024, 512) if n % ck == 0)
    eab = jnp.zeros((CNT_ROWS, LANES), F32)
    eab = eab.at[:N_BUCKETS, 0].set(jnp.asarray(
        [GROUP_SIZE * (bk // N_PAIRS) + PAIR_LO[bk % N_PAIRS] for bk in range(N_BUCKETS)], F32))
    eab = eab.at[:N_BUCKETS, 1].set(jnp.asarray(
        [GROUP_SIZE * (bk // N_PAIRS) + PAIR_HI[bk % N_PAIRS] for bk in range(N_BUCKETS)], F32))
    const = lambda i: (0, 0)
    return pl.pallas_call(
        _plan_kernel,
        out_shape=[jax.ShapeDtypeStruct((1, n), jnp.int32),
                   jax.ShapeDtypeStruct((SUBLANES, META_LANES), jnp.int32)],
        grid=(n // chunk,),
        in_specs=[pl.BlockSpec((ROUTE_ROWS, chunk), lambda i: (0, i)), pl.BlockSpec((CNT_ROWS, LANES), const),
                  pl.BlockSpec((CNT_ROWS, LANES), const)],
        out_specs=[pl.BlockSpec((1, chunk), lambda i: (0, i)), pl.BlockSpec((SUBLANES, META_LANES), const)],
        compiler_params=_cparams(1), name="plan",
    )(route, counts, eab)


def _fill_padding(first_ref, cnt_ref, nt_ref, xs_ref, zbuf, sem, n_tiles_max, *, issue):
    if issue:
        zbuf[...] = jnp.zeros_like(zbuf)

    def fill(first_row, n_rows):
        cp = pltpu.make_async_copy(zbuf.at[pl.ds(0, n_rows)], xs_ref.at[pl.ds(first_row, n_rows)], sem)
        if issue:
            cp.start()
        else:
            cp.wait()

    def fill_tile(j, carry):
        fill(pl.multiple_of(j * TM_MOE, TM_MOE), TM_MOE)
        return carry

    lax.fori_loop(nt_ref[0], n_tiles_max, fill_tile, 0)

    for bk in range(N_BUCKETS):
        cnt = cnt_ref[bk]
        start = first_ref[bk] * TM_MOE + cnt
        pad = (-cnt) & (TM_MOE - 1)
        head = jnp.minimum((-cnt) & (SUBLANES - 1), pad)
        for j in range(SUBLANES - 1):
            @pl.when(j < head)
            def _(j=j, start=start):
                fill(start + j, 1)

        body = pad - head
        piece = TM_MOE // 2
        while piece >= SUBLANES:
            done = body & ~(2 * piece - 1)

            @pl.when((body & piece) != 0)
            def _(piece=piece, done=done, start=start, head=head):
                fill(pl.multiple_of(start + head + done, SUBLANES), piece)

            piece //= 2


def _sort_kernel(*refs, tm, d, step_ranges, n_tiles_max):
    n_src = len(step_ranges)
    pos_ref, first_ref, cnt_ref, nt_ref = refs[:4]
    srcs = refs[4:4 + 2 * n_src]
    xs_ref, buf, zbuf, sem, zsem = refs[4 + 2 * n_src:]
    n_steps = step_ranges[-1][1]
    i = pl.program_id(0)
    slot = i % 2

    def wait_rows(s):
        pltpu.make_async_copy(buf.at[s], xs_ref.at[pl.ds(0, tm)], sem.at[s]).wait()

    @pl.when(i == 0)
    def _():
        _fill_padding(first_ref, cnt_ref, nt_ref, xs_ref, zbuf, zsem, n_tiles_max, issue=True)

    @pl.when(i >= 2)
    def _():
        wait_rows(slot)

    for k, (s0, s1) in enumerate(step_ranges):
        @pl.when((i >= s0) & (i < s1))
        def _(k=k):
            tok_ref, route_ref = srcs[2 * k], srcs[2 * k + 1]
            cols = jnp.concatenate([route_ref[...], jnp.zeros((LANES - ROUTE_ROWS, tm), F32)], axis=0).T
            buf[slot, :, 0:d] = tok_ref[...]
            buf[slot, :, d:] = cols

    for t in range(tm):
        p = pos_ref[i * tm + t]
        pltpu.make_async_copy(buf.at[slot, pl.ds(t, 1)], xs_ref.at[pl.ds(p, 1)], sem.at[slot]).start()

    @pl.when(i == n_steps - 1)
    def _():
        wait_rows(slot)
        if n_steps >= 2:
            wait_rows(1 - slot)
        _fill_padding(first_ref, cnt_ref, nt_ref, xs_ref, zbuf, zsem, n_tiles_max, issue=False)


def _sort_rows(pos, first, cnt, nt, toks, routes, n_tiles_max):
    d = toks[0].shape[1]
    tm = TM_LAT
    step_ranges, s0 = [], 0
    for tok in toks:
        step_ranges.append((s0, s0 + tok.shape[0] // tm))
        s0 = step_ranges[-1][1]
    in_specs, args = [], []
    for (b0, b1), tok, rt in zip(step_ranges, toks, routes):
        blk = lambda i, *_, b0=b0, b1=b1: jnp.clip(i - b0, 0, b1 - b0 - 1)
        in_specs += [pl.BlockSpec((tm, d), lambda i, *_, blk=blk: (blk(i), 0)),
                     pl.BlockSpec((ROUTE_ROWS, tm),lambda i, *_, blk=blk: (0, blk(i)))]
        args += [tok, rt]
    grid_spec = pltpu.PrefetchScalarGridSpec(
        num_scalar_prefetch=4, grid=(s0,), in_specs=in_specs,
        out_specs=pl.BlockSpec(memory_space=pl.ANY),
        scratch_shapes=[pltpu.VMEM((2, tm, d + LANES), F32), pltpu.VMEM((TM_MOE, d + LANES), F32),
                        pltpu.SemaphoreType.DMA((2,)), pltpu.SemaphoreType.DMA])
    return pl.pallas_call(
        functools.partial(_sort_kernel, tm=tm, d=d, step_ranges=tuple(step_ranges), n_tiles_max=n_tiles_max),
        out_shape=jax.ShapeDtypeStruct((n_tiles_max * TM_MOE, d + LANES), F32), grid_spec=grid_spec,
        compiler_params=_cparams(1), name="sort_rows",
    )(pos, first, cnt, nt, *args)


def _moe_layer(toks, routes, counts, layer, wg, wu, wd):
    d = toks[0].shape[1]
    n = sum(t.shape[0] for t in toks)
    n_tiles_max = n // TM_MOE + N_BUCKETS
    assert n_tiles_max <= META_LANES
    route = routes[0] if len(routes) == 1 else jnp.concatenate(routes, axis=1)
    pos, meta = _plan(route, counts)
    pos = pos.reshape(n)
    xs = _sort_rows(pos, meta[3, :N_BUCKETS], meta[4, :N_BUCKETS], meta[2, :1], toks, routes, n_tiles_max)
    ys = _moe(meta[0, :n_tiles_max], meta[1, :n_tiles_max], meta[2, :1], xs, layer, wg, wu, wd)
    return jnp.concatenate([pos, jnp.zeros(((GATHER_SLOTS - 1) * TM_WIDE,), jnp.int32)]), ys


def _final_kernel(pos_ref, x_ref, ys_ref, mod_ref, gpost_ref, o_ref, fbuf, sem, *, tm):
    def add_residual(f):
        o_ref[...] = x_ref[...] + _rms(f, gpost_ref[...] * mod_ref[5:6, :])

    _with_gathered_tile(pos_ref, ys_ref, fbuf, sem, 0, tm, add_residual)


def _final(x, pos, ys, mods, layer, gpost, *, tm, rows_per_batch):
    n, d = x.shape
    tiles_per_batch = rows_per_batch // tm
    row = lambda i, *_: (i, 0)
    grid_spec = pltpu.PrefetchScalarGridSpec(
        num_scalar_prefetch=1, grid=(n // tm,),
        in_specs=[pl.BlockSpec((tm, d), row), pl.BlockSpec(memory_space=pl.ANY),
                  pl.BlockSpec((None, None, 6, d), lambda i, *_: (layer, i // tiles_per_batch, 0, 0)),
                  pl.BlockSpec((1, d), lambda i, *_: (0, 0))],
        out_specs=pl.BlockSpec((tm, d), row),
        scratch_shapes=[pltpu.VMEM((GATHER_SLOTS, tm, d), F32), pltpu.SemaphoreType.DMA((GATHER_SLOTS,))])
    return pl.pallas_call(
        functools.partial(_final_kernel, tm=tm), out_shape=jax.ShapeDtypeStruct((n, d), F32),
        grid_spec=grid_spec, compiler_params=_cparams(1), name="final",
    )(pos, x, ys, mods, gpost)


def kernel(x, c, ctx, c_ctx, w_ada, b_ada, g_mix_pre, g_mix_post, g_ffn_pre, g_ffn_post, conv_w_in, conv_b_in, conv_w_dw, conv_b_dw, conv_ln_g, conv_ln_b, conv_w_out, conv_b_out, attn_w_qkv, attn_w_o, attn_rpb, router_w, router_b, exp_w_gate, exp_w_up, exp_w_down):
    b, s, d = x.shape
    l = ctx.shape[1]
    n, nc = b * s, b * l
    depth = w_ada.shape[0]
    rows = s // GRID_W
    assert depth == 2 and b <= CTX_MOD_ROW and s % TM_WIDE == 0 and l == TM_CTX
    assert rows % Q_ROWS == 0 and rows >= SLAB_ROWS + Q_ROWS
    tm_lat = TM_LAT

    cc = jnp.zeros((MOD_ROWS, d), F32).at[:b].set(c).at[CTX_MOD_ROW].set(c_ctx)
    mods = _ada(cc, w_ada, b_ada).reshape(depth, MOD_ROWS, 6, d)

    row = lambda v: v.reshape(1, d)
    rwt = router_w.T
    rwh = rwt.astype(BF16)
    rwl = (rwt - rwh.astype(F32)).astype(BF16)
    rb = router_b.reshape(N_EXPERTS, 1)
    no_counts = jnp.zeros((CNT_ROWS, LANES), F32)

    xl = x.reshape(n, d)
    xc = ctx.reshape(nc, d)

    w_in = conv_w_in[0].astype(BF16)
    b_in = conv_b_in[0].reshape(1, 2 * d)
    conv = (conv_w_dw[0], row(conv_b_dw[0]), row(conv_ln_g[0]), row(conv_ln_b[0]),
            conv_w_out[0].astype(BF16), row(conv_b_out[0]))
    tailp = (row(g_mix_post[0]), row(g_ffn_pre[0]), rwh, rwl, rb)
    (u,) = _proj(xl, mods, 0, row(g_mix_pre[0]), w_in, tm=TM_WIDE, rows_per_batch=s, ctx=False,
                 bias=b_in, name="conv_in")
    (uc,) = _proj(xc, mods, 0, row(g_mix_pre[0]), w_in, tm=TM_CTX, rows_per_batch=l, ctx=True,
                  bias=b_in, name="conv_in_ctx")
    x1, tok, route, counts = _conv_tail(u, xl, mods, 0, conv, tailp, no_counts, tm=tm_lat, rows_per_batch=s,
                                        ctx=False, name="conv_tail")
    xc1, tokc, routec, counts = _conv_tail(uc, xc, mods, 0, conv, tailp, counts, tm=TM_CTX, rows_per_batch=l,
                                           ctx=True, name="conv_tail_ctx")
    pos0, ys0 = _moe_layer([tok, tokc], [route, routec], counts, 0, exp_w_gate, exp_w_up, exp_w_down)

    w_qkv = attn_w_qkv[0].astype(BF16)
    x2, q, k, v = _proj(x1, mods, 1, row(g_mix_pre[1]), w_qkv, tm=tm_lat, rows_per_batch=s, ctx=False,
                        resid=(pos0, ys0, row(g_ffn_post[0]), 0), n_out=3, name="qkv")
    _, kc, vc = _proj(xc1, mods, 1, row(g_mix_pre[1]), w_qkv[:, d:], tm=TM_CTX, rows_per_batch=l, ctx=True,
                      resid=(pos0, ys0, row(g_ffn_post[0]), n), n_out=2, name="kv_ctx")
    c2 = _bias_pairs(attn_rpb[0])
    o = _natten(q.reshape(b, s, d), k.reshape(b, s, d), v.reshape(b, s, d),
                kc.reshape(b, l, d), vc.reshape(b, l, d), c2)
    tailp1 = (row(g_mix_post[1]), row(g_ffn_pre[1]), rwh, rwl, rb)
    x3, tok1, route1, counts1 = _attn_tail(o.reshape(n, d), x2, mods, 1, attn_w_o[0].astype(BF16), tailp1,
                                           no_counts, tm=tm_lat, rows_per_batch=s, name="attn_tail")
    pos1, ys1 = _moe_layer([tok1], [route1], counts1, 1, exp_w_gate, exp_w_up, exp_w_down)
    out = _final(x3, pos1, ys1, mods, 1, row(g_ffn_post[1]), tm=TM_WIDE, rows_per_batch=s)
    return out.reshape(b, s, d)
```
